```python
import math
import jax, jax.numpy as jnp
from jax import lax
import numpy as np

D_MODEL = 2048
BATCH = 2
SEQ = 4096
DEPTH = 2

GRID_W = 64
CTX_LEN = 256
EPS = 1e-6

POOL_GROUPS = 4
POOL_WINDOWS = (2, 4, 8, 16)
POOL_W = D_MODEL // 2
POOL_GC = POOL_W // POOL_GROUPS
POOL_OUT = D_MODEL // POOL_GROUPS

FOUR_GROUPS = 4
FOUR_W = D_MODEL // 2
FOUR_GC = FOUR_W // FOUR_GROUPS

D_SSD = D_MODEL
HEAD_DIM = 64
SSD_HEADS = D_SSD // HEAD_DIM
SSD_GROUPS = 4
HEADS_PER_GROUP = SSD_HEADS // SSD_GROUPS
D_STATE = 128
CONV_W = 5
CHUNK = 128

D_FF = ((8 * D_MODEL // 3 + 255) // 256) * 256

XBC = D_SSD + 2 * SSD_GROUPS * D_STATE
SSD_IN = XBC + 2 * SSD_HEADS
Z_OFF = SSD_IN
POOL_OFF = Z_OFF + D_SSD
FOUR_OFF = POOL_OFF + POOL_W
GATE_OFF = FOUR_OFF + FOUR_W
N_IN = GATE_OFF + 3 * D_MODEL

kernel_name = "hybrid_pool_fourier_ssd_prefix_dit"


def _rmsnorm(x, w):
    xf = x.astype(jnp.float32)
    y = xf * lax.rsqrt(jnp.mean(xf * xf, axis=-1, keepdims=True) + EPS)
    return y.astype(x.dtype) * w


def _swiglu(h, w_gate, w_up, w_down):
    return (jax.nn.silu(h @ w_gate) * (h @ w_up)) @ w_down


def _box_mean(v, w, axis):
    n = v.shape[axis]
    idx = jnp.arange(n)
    lo = jnp.clip(idx - w // 2, 0, n)
    hi = jnp.clip(idx + (w - w // 2), 0, n)
    pad = [(0, 0)] * v.ndim
    pad[axis] = (1, 0)
    cs = jnp.pad(jnp.cumsum(v, axis=axis), pad)
    s = jnp.take(cs, hi, axis=axis) - jnp.take(cs, lo, axis=axis)
    shape = [1] * v.ndim
    shape[axis] = n
    return s / (hi - lo).astype(jnp.float32).reshape(shape)


def _pool_branch(u, rows, w_pool, pool_scale):
    b, L, _ = u.shape
    ug = u.astype(jnp.float32).reshape(b, L, POOL_GROUPS, POOL_GC)
    outs = []
    for g, w in enumerate(POOL_WINDOWS):
        v = ug[:, :, g]
        if rows is None:
            m = _box_mean(v, w, 1)
        else:
            vg = v.reshape(b, rows, GRID_W, POOL_GC)
            m = _box_mean(_box_mean(vg, w, 1), w, 2).reshape(b, L, POOL_GC)
        outs.append(m - v)
    d = jnp.stack(outs, axis=2).astype(u.dtype)
    y = jnp.einsum('blgc,gce->blge', d, w_pool).reshape(b, L, D_MODEL)
    return y * pool_scale


def _fourier_branch(u, w_fourier):
    b, L, _ = u.shape
    ug = u.astype(jnp.float32).reshape(b, L, FOUR_GROUPS, FOUR_GC)
    f = jnp.fft.fftn(ug, axes=(1, 3), norm="ortho").real
    return f.reshape(b, L, FOUR_W).astype(u.dtype) @ w_fourier


def _dwconv(u, w, bias):
    y = lax.conv_general_dilated(
        u, w[:, None, :], window_strides=(1,),
        padding=[(CONV_W // 2, CONV_W // 2)],
        dimension_numbers=('NWC', 'WIO', 'NWC'),
        feature_group_count=u.shape[-1])
    return y + bias


def _segsum(a):
    T = a.shape[-1]
    ar = jnp.broadcast_to(a[..., :, None], a.shape + (T,))
    ar = jnp.where(jnp.tril(jnp.ones((T, T), bool), -1), ar, 0.0)
    cs = jnp.cumsum(ar, axis=-2)
    return jnp.where(jnp.tril(jnp.ones((T, T), bool)), cs, -jnp.inf)


def _ssd_scan(xs, dt, a, bm, cm, h0, with_output):
    b, L, G, J, P = xs.shape
    N = bm.shape[-1]
    nc = L // CHUNK
    xd = (xs.astype(jnp.float32) * dt[..., None]).reshape(b, nc, CHUNK, G, J, P)
    bc = bm.astype(jnp.float32).reshape(b, nc, CHUNK, G, N)
    cc = cm.astype(jnp.float32).reshape(b, nc, CHUNK, G, N)
    da = (dt * a).reshape(b, nc, CHUNK, G, J).transpose(0, 3, 4, 1, 2)
    acs = jnp.cumsum(da, axis=-1)
    decay_to_end = jnp.exp(acs[..., -1:] - acs).transpose(0, 3, 4, 1, 2)
    states = jnp.einsum('bclgn,bclgjp->bcgjpn', bc, xd * decay_to_end[..., None])
    states = jnp.concatenate([h0[:, None], states], axis=1)
    chunk_decay = jnp.exp(_segsum(jnp.pad(acs[..., -1], ((0, 0), (0, 0), (0, 0), (1, 0)))))
    states = jnp.einsum('bgjzc,bcgjpn->bzgjpn', chunk_decay, states)
    final = states[:, -1]
    if not with_output:
        return None, final
    lm = jnp.exp(_segsum(da))
    cb = jnp.einsum('bclgn,bcsgn->bgcls', cc, bc)
    y_diag = jnp.einsum('bgjcls,bcsgjp->bclgjp', cb[:, :, None] * lm, xd)
    y_off = jnp.einsum('bclgn,bcgjpn->bclgjp', cc, states[:, :-1]) * \
        jnp.exp(acs).transpose(0, 3, 4, 1, 2)[..., None]
    y = (y_diag + y_off).reshape(b, L, G, J, P)
    return y.astype(xs.dtype), final


def _ssd_core(p_ssd, conv_w, conv_b, a_log, dt_bias, d_skip, h0f, h0b, with_output):
    b, L, _ = p_ssd.shape
    G, J = SSD_GROUPS, HEADS_PER_GROUP
    xbc = jax.nn.silu(_dwconv(p_ssd[..., :XBC], conv_w, conv_b))
    xs = xbc[..., :D_SSD].reshape(b, L, G, J, HEAD_DIM)
    bm = xbc[..., D_SSD:D_SSD + G * D_STATE].reshape(b, L, G, D_STATE)
    cm = xbc[..., D_SSD + G * D_STATE:].reshape(b, L, G, D_STATE)
    dt = jax.nn.softplus(p_ssd[..., XBC:].astype(jnp.float32).reshape(b, L, 2, G, J)
                         + dt_bias.astype(jnp.float32).reshape(2, G, J))
    a = -jnp.exp(a_log.astype(jnp.float32)).reshape(2, G, J)
    rev = lambda t: jnp.flip(t, axis=1)
    yf, hf = _ssd_scan(xs, dt[:, :, 0], a[0], bm, cm, h0f, with_output)
    yb, hb = _ssd_scan(rev(xs), rev(dt[:, :, 1]), a[1], rev(bm), rev(cm), h0b, with_output)
    if not with_output:
        return None, hf, hb
    y = yf + rev(yb) + d_skip.reshape(G, J)[..., None] * xs
    return y, hf, hb


def _mixer(p, rows, h0f, h0b, conv_w, conv_b, a_log, dt_bias, d_skip, ssd_norm_w,
           w_ssd_out, w_pool, pool_scale, w_fourier, w_out):
    b, L, _ = p.shape
    G = SSD_GROUPS
    gw = HEADS_PER_GROUP * HEAD_DIM
    y, hf, hb = _ssd_core(p[..., :SSD_IN], conv_w, conv_b, a_log, dt_bias, d_skip, h0f, h0b, True)
    z = p[..., Z_OFF:POOL_OFF].reshape(b, L, G, gw)
    y = _rmsnorm(y.reshape(b, L, G, gw) * jax.nn.silu(z), ssd_norm_w.reshape(G, gw))
    y_ssd = y.reshape(b, L, D_SSD) @ w_ssd_out
    y_pool = _pool_branch(p[..., POOL_OFF:FOUR_OFF], rows, w_pool, pool_scale)
    y_four = _fourier_branch(p[..., FOUR_OFF:GATE_OFF], w_fourier)
    g = jax.nn.sigmoid(p[..., GATE_OFF:].astype(jnp.float32)).astype(p.dtype).reshape(b, L, 3, D_MODEL)
    merged = g[:, :, 0] * y_pool + g[:, :, 1] * y_four + g[:, :, 2] * y_ssd
    return merged @ w_out, hf, hb


def setup_inputs(seed: int = 0) -> dict:
    key = jax.random.key(seed)
    ks = jax.random.split(key, 26)
    f32 = jnp.float32
    nrm = lambda k, shape, s: jax.random.normal(k, shape, f32) * s
    dt0 = jnp.exp(jax.random.uniform(ks[12], (DEPTH, 2, SSD_HEADS), f32, math.log(1e-3), math.log(1e-1)))
    return {
        "x": nrm(ks[0], (BATCH, SEQ, D_MODEL), 1.0),
        "c": nrm(ks[1], (BATCH, D_MODEL), 1.0),
        "ctx": nrm(ks[2], (BATCH, CTX_LEN, D_MODEL), 1.0),
        "c_ctx": nrm(ks[3], (D_MODEL,), 1.0),
        "w_ada": nrm(ks[4], (DEPTH, D_MODEL, 6 * D_MODEL), 0.5 * D_MODEL ** -0.5),
        "b_ada": nrm(ks[5], (DEPTH, 6 * D_MODEL), 0.01),
        "norm_mix_w": 1.0 + nrm(ks[6], (DEPTH, D_MODEL), 0.05),
        "norm_ffn_w": 1.0 + nrm(ks[7], (DEPTH, D_MODEL), 0.05),
        "w_in": nrm(ks[8], (DEPTH, D_MODEL, N_IN), D_MODEL ** -0.5),
        "conv_w": nrm(ks[9], (DEPTH, CONV_W, XBC), CONV_W ** -0.5),
        "conv_b": nrm(ks[10], (DEPTH, XBC), 0.01),
        "a_log": jnp.log(jax.random.uniform(ks[11], (DEPTH, 2, SSD_HEADS), f32, 1.0, 16.0)),
        "dt_bias": dt0 + jnp.log(-jnp.expm1(-dt0)),
        "d_skip": 1.0 + nrm(ks[13], (DEPTH, SSD_HEADS), 0.1),
        "ssd_norm_w": 1.0 + nrm(ks[14], (DEPTH, D_SSD), 0.05),
        "w_ssd_out": nrm(ks[15], (DEPTH, D_SSD, D_MODEL), D_SSD ** -0.5),
        "w_pool": nrm(ks[16], (DEPTH, POOL_GROUPS, POOL_GC, POOL_OUT), POOL_GC ** -0.5),
        "pool_scale": 1.0 + nrm(ks[17], (DEPTH, D_MODEL), 0.05),
        "w_fourier": nrm(ks[18], (DEPTH, FOUR_W, D_MODEL), FOUR_W ** -0.5),
        "w_out": nrm(ks[19], (DEPTH, D_MODEL, D_MODEL), D_MODEL ** -0.5),
        "w_ffn_gate": nrm(ks[20], (DEPTH, D_MODEL, D_FF), D_MODEL ** -0.5),
        "w_ffn_up": nrm(ks[21], (DEPTH, D_MODEL, D_FF), D_MODEL ** -0.5),
        "w_ffn_down": nrm(ks[22], (DEPTH, D_FF, D_MODEL), D_FF ** -0.5),
        "final_norm_w": 1.0 + nrm(ks[23], (D_MODEL,), 0.05),
    }


def reference(x, c, ctx, c_ctx, w_ada, b_ada, norm_mix_w, norm_ffn_w, w_in, conv_w, conv_b,
              a_log, dt_bias, d_skip, ssd_norm_w, w_ssd_out, w_pool, pool_scale, w_fourier,
              w_out, w_ffn_gate, w_ffn_up, w_ffn_down, final_norm_w):
    rows = x.shape[1] // GRID_W
    h0 = jnp.zeros((ctx.shape[0], SSD_GROUPS, HEADS_PER_GROUP, HEAD_DIM, D_STATE), jnp.float32)
    for l in range(DEPTH):
        last = l == DEPTH - 1
        sh_m, sc_m, g_m, sh_f, sc_f, g_f = [m[:, None, :] for m in
            jnp.split(jax.nn.silu(c) @ w_ada[l] + b_ada[l], 6, axis=-1)]
        csh_m, csc_m, cg_m, csh_f, csc_f, cg_f = jnp.split(
            jax.nn.silu(c_ctx) @ w_ada[l] + b_ada[l], 6, axis=-1)

        hc = _rmsnorm(ctx, norm_mix_w[l]) * (1 + csc_m) + csh_m
        if last:
            _, hf, hb = _ssd_core(hc @ w_in[l][:, :SSD_IN], conv_w[l], conv_b[l], a_log[l],
                                  dt_bias[l], d_skip[l], h0, h0, False)
        else:
            yc, hf, hb = _mixer(hc @ w_in[l], None, h0, h0, conv_w[l], conv_b[l], a_log[l],
                                dt_bias[l], d_skip[l], ssd_norm_w[l], w_ssd_out[l], w_pool[l],
                                pool_scale[l], w_fourier[l], w_out[l])
            ctx = ctx + cg_m * yc
            ctx = ctx + cg_f * _swiglu(_rmsnorm(ctx, norm_ffn_w[l]) * (1 + csc_f) + csh_f,
                                       w_ffn_gate[l], w_ffn_up[l], w_ffn_down[l])

        hl = _rmsnorm(x, norm_mix_w[l]) * (1 + sc_m) + sh_m
        yl, _, _ = _mixer(hl @ w_in[l], rows, hf, hb, conv_w[l], conv_b[l], a_log[l],
                          dt_bias[l], d_skip[l], ssd_norm_w[l], w_ssd_out[l], w_pool[l],
                          pool_scale[l], w_fourier[l], w_out[l])
        x = x + g_m * yl
        x = x + g_f * _swiglu(_rmsnorm(x, norm_ffn_w[l]) * (1 + sc_f) + sh_f,
                              w_ffn_gate[l], w_ffn_up[l], w_ffn_down[l])
    return _rmsnorm(x, final_norm_w)
```

```python
import functools
import math

import numpy as np
import jax
import jax.numpy as jnp
from jax import lax
from jax.experimental import pallas as pl
from jax.experimental.pallas import tpu as pltpu

F32 = jnp.float32
BF16 = jnp.bfloat16

D_MODEL = 2048
DEPTH = 2
GRID_W = 64
EPS = 1e-6

POOL_GROUPS = 4
POOL_WINDOWS = (2, 4, 8, 16)
POOL_W = D_MODEL // 2
POOL_GC = POOL_W // POOL_GROUPS
POOL_OUT = D_MODEL // POOL_GROUPS

FOUR_GROUPS = 4
FOUR_W = D_MODEL // 2
FOUR_GC = FOUR_W // FOUR_GROUPS

D_SSD = D_MODEL
HEAD_DIM = 64
SSD_HEADS = D_SSD // HEAD_DIM
SSD_GROUPS = 4
HEADS_PER_GROUP = SSD_HEADS // SSD_GROUPS
GROUP_W = HEADS_PER_GROUP * HEAD_DIM
D_STATE = 128
CONV_W = 5
CHUNK = 128

D_FF = ((8 * D_MODEL // 3 + 255) // 256) * 256

XBC = D_SSD + 2 * SSD_GROUPS * D_STATE
SSD_IN = XBC + 2 * SSD_HEADS
Z_OFF = SSD_IN
POOL_OFF = Z_OFF + D_SSD
FOUR_OFF = POOL_OFF + POOL_W
GATE_OFF = FOUR_OFF + FOUR_W
N_IN = GATE_OFF + 3 * D_MODEL

P_XBC = 0
P_POOL = XBC
P_Z = P_POOL + POOL_W
P_GATE = P_Z + D_SSD
P_FOUR = P_GATE + 3 * D_MODEL
P_TOT = P_FOUR + FOUR_W
DT_PAD = 128

V7X_VMEM_LIMIT = 56 * 1024 * 1024
HALO = 8


def _cparams(sem):
    return pltpu.CompilerParams(dimension_semantics=sem, vmem_limit_bytes=V7X_VMEM_LIMIT)


def _split2(v):
    hi = v.astype(BF16)
    lo = (v - hi.astype(F32)).astype(BF16)
    return hi, lo


def _dot(a, b):
    return jnp.dot(a, b, preferred_element_type=F32)


def _dot_cd(const_hi, const_lo, data):
    dh, dl = _split2(data)
    return _dot(const_hi, dh) + _dot(const_hi, dl) + _dot(const_lo, dh)


def _dot_dc(data, const_hi, const_lo):
    dh, dl = _split2(data)
    return _dot(dh, const_hi) + _dot(dl, const_hi) + _dot(dh, const_lo)


def _silu(v):
    return v * jax.nn.sigmoid(v)


def _np_split2(m):
    m = np.asarray(m, np.float32)
    hi = jnp.asarray(m).astype(BF16)
    lo = (jnp.asarray(m) - hi.astype(F32)).astype(BF16)
    return hi, lo


def _adaln_kernel(c_ref, w_ref, b_ref, o_ref):
    s = _silu(c_ref[...])
    w = w_ref[0]
    sh, sl = _split2(s)
    wh, wl = _split2(w)
    o_ref[0] = _dot(sh, wh) + _dot(sl, wh) + _dot(sh, wl) + b_ref[0]


def _adaln(cc, w_ada, b_ada, tn=1024):
    depth, d, n = w_ada.shape
    return pl.pallas_call(
        _adaln_kernel,
        grid=(depth, n // tn),
        in_specs=[pl.BlockSpec((8, d), lambda l, j: (0, 0)),
                  pl.BlockSpec((1, d, tn), lambda l, j: (l, 0, j)),
                  pl.BlockSpec((1, 1, tn), lambda l, j: (l, 0, j))],
        out_specs=pl.BlockSpec((1, 8, tn), lambda l, j: (l, 0, j)),
        out_shape=jax.ShapeDtypeStruct((depth, 8, n), F32),
        compiler_params=_cparams(("parallel", "parallel")),
        name="adaln",
    )(cc, w_ada, b_ada.reshape(depth, 1, n))


def _norm_mod(x, nw, sc, sh):
    ms = jnp.mean(x * x, axis=-1, keepdims=True)
    return (x * lax.rsqrt(ms + EPS)) * nw * (1.0 + sc) + sh


def _norm_mm_kernel(x_ref, nw_ref, sc_ref, sh_ref, w_ref, o_ref, h_ref):
    @pl.when(pl.program_id(2) == 0)
    def _():
        h_ref[...] = _norm_mod(x_ref[0], nw_ref[...], sc_ref[0], sh_ref[0]).astype(BF16)

    o_ref[0] = _dot(h_ref[...], w_ref[...]).astype(o_ref.dtype)


def _norm_mm(x, nw, sc, sh, w, n_out, tm, tn, out_dtype=F32):
    b, l, d = x.shape
    tm = min(tm, l)
    return pl.pallas_call(
        _norm_mm_kernel,
        grid=(b, l // tm, n_out // tn),
        in_specs=[pl.BlockSpec((1, tm, d), lambda bi, i, j: (bi, i, 0)),
                  pl.BlockSpec((1, d), lambda bi, i, j: (0, 0)),
                  pl.BlockSpec((1, 1, d), lambda bi, i, j: (bi, 0, 0)),
                  pl.BlockSpec((1, 1, d), lambda bi, i, j: (bi, 0, 0)),
                  pl.BlockSpec((d, tn), lambda bi, i, j: (0, j))],
        out_specs=pl.BlockSpec((1, tm, tn), lambda bi, i, j: (bi, i, j)),
        out_shape=jax.ShapeDtypeStruct((b, l, n_out), out_dtype),
        scratch_shapes=[pltpu.VMEM((tm, d), BF16)],
        compiler_params=_cparams(("parallel", "parallel", "arbitrary")),
        name="norm_mm",
    )(x, nw.reshape(1, d), sc, sh, w)


def _mm_kernel(a_ref, w_ref, o_ref):
    o_ref[0] = _dot(a_ref[0].astype(BF16), w_ref[...]).astype(o_ref.dtype)


def _mm(a, w, tm, tn, out_dtype=F32):
    b, l, k = a.shape
    n = w.shape[1]
    tm = min(tm, l)
    return pl.pallas_call(
        _mm_kernel,
        grid=(b, l // tm, n // tn),
        in_specs=[pl.BlockSpec((1, tm, k), lambda bi, i, j: (bi, i, 0)),
                  pl.BlockSpec((k, tn), lambda bi, i, j: (0, j))],
        out_specs=pl.BlockSpec((1, tm, tn), lambda bi, i, j: (bi, i, j)),
        out_shape=jax.ShapeDtypeStruct((b, l, n), out_dtype),
        compiler_params=_cparams(("parallel", "parallel", "parallel")),
        name="mm",
    )(a, w)


def _pool_mm_kernel(a_ref, w_ref, s_ref, o_ref):
    o_ref[0] = (_dot(a_ref[0], w_ref[0]) * s_ref[...]).astype(o_ref.dtype)


def _pool_mm(d, w_pool, pool_scale, tm):
    b, l, _ = d.shape
    tm = min(tm, l)
    return pl.pallas_call(
        _pool_mm_kernel,
        grid=(b, l // tm, POOL_GROUPS),
        in_specs=[pl.BlockSpec((1, tm, POOL_GC), lambda bi, i, g: (bi, i, g)),
                  pl.BlockSpec((1, POOL_GC, POOL_OUT), lambda bi, i, g: (g, 0, 0)),
                  pl.BlockSpec((1, POOL_OUT), lambda bi, i, g: (0, g))],
        out_specs=pl.BlockSpec((1, tm, POOL_OUT), lambda bi, i, g: (bi, i, g)),
        out_shape=jax.ShapeDtypeStruct((b, l, D_MODEL), F32),
        compiler_params=_cparams(("parallel", "parallel", "parallel")),
        name="pool_mm",
    )(d, w_pool, pool_scale.reshape(1, D_MODEL))


def _merge_kernel(yp_ref, yf_ref, ys_ref, gp_ref, gf_ref, gs_ref, w_ref, x_ref, gm_ref, o_ref, m_ref):
    @pl.when(pl.program_id(2) == 0)
    def _():
        m = (jax.nn.sigmoid(gp_ref[0].astype(F32)) * yp_ref[0]
             + jax.nn.sigmoid(gf_ref[0].astype(F32)) * yf_ref[0]
             + jax.nn.sigmoid(gs_ref[0].astype(F32)) * ys_ref[0])
        m_ref[...] = m.astype(BF16)

    o_ref[0] = x_ref[0] + gm_ref[0] * _dot(m_ref[...], w_ref[...])


def _merge_out(y_pool, y_four, y_ssd, p, w_out, x, g_m, tm, tn):
    b, l, d = x.shape
    tm = min(tm, l)
    gblk = P_GATE // d
    row = lambda bi, i, j: (bi, i, 0)
    return pl.pallas_call(
        _merge_kernel,
        grid=(b, l // tm, d // tn),
        in_specs=[pl.BlockSpec((1, tm, d), row),
                  pl.BlockSpec((1, tm, d), row),
                  pl.BlockSpec((1, tm, d), row),
                  pl.BlockSpec((1, tm, d), lambda bi, i, j: (bi, i, gblk)),
                  pl.BlockSpec((1, tm, d), lambda bi, i, j: (bi, i, gblk + 1)),
                  pl.BlockSpec((1, tm, d), lambda bi, i, j: (bi, i, gblk + 2)),
                  pl.BlockSpec((d, tn), lambda bi, i, j: (0, j)),
                  pl.BlockSpec((1, tm, tn), lambda bi, i, j: (bi, i, j)),
                  pl.BlockSpec((1, 1, tn), lambda bi, i, j: (bi, 0, j))],
        out_specs=pl.BlockSpec((1, tm, tn), lambda bi, i, j: (bi, i, j)),
        out_shape=jax.ShapeDtypeStruct((b, l, d), F32),
        scratch_shapes=[pltpu.VMEM((tm, d), BF16)],
        compiler_params=_cparams(("parallel", "parallel", "arbitrary")),
        name="merge_out",
    )(y_pool, y_four, y_ssd, p, p, p, w_out, x, g_m)


def _ffn_kernel(x_ref, nw_ref, sc_ref, sh_ref, wg_ref, wu_ref, wd_ref, gate_ref, fnw_ref,
                o_ref, h_ref, acc_ref, *, final_norm):
    j = pl.program_id(2)

    @pl.when(j == 0)
    def _():
        h_ref[...] = _norm_mod(x_ref[0], nw_ref[...], sc_ref[0], sh_ref[0]).astype(BF16)
        acc_ref[...] = jnp.zeros_like(acc_ref)

    h = h_ref[...]
    a = _silu(_dot(h, wg_ref[...])) * _dot(h, wu_ref[...])
    acc_ref[...] += _dot(a.astype(BF16), wd_ref[...])

    @pl.when(j == pl.num_programs(2) - 1)
    def _():
        o = x_ref[0] + gate_ref[0] * acc_ref[...]
        if final_norm:
            ms = jnp.mean(o * o, axis=-1, keepdims=True)
            o = (o * lax.rsqrt(ms + EPS)) * fnw_ref[...]
        o_ref[0] = o


def _ffn(x, nw, sc, sh, wg, wu, wd, gate, fnw, final_norm, tm, tf):
    b, l, d = x.shape
    ff = wg.shape[1]
    tm = min(tm, l)
    vec = lambda bi, i, j: (bi, 0, 0)
    return pl.pallas_call(
        functools.partial(_ffn_kernel, final_norm=final_norm),
        grid=(b, l // tm, ff // tf),
        in_specs=[pl.BlockSpec((1, tm, d), lambda bi, i, j: (bi, i, 0)),
                  pl.BlockSpec((1, d), lambda bi, i, j: (0, 0)),
                  pl.BlockSpec((1, 1, d), vec),
                  pl.BlockSpec((1, 1, d), vec),
                  pl.BlockSpec((d, tf), lambda bi, i, j: (0, j)),
                  pl.BlockSpec((d, tf), lambda bi, i, j: (0, j)),
                  pl.BlockSpec((tf, d), lambda bi, i, j: (j, 0)),
                  pl.BlockSpec((1, 1, d), vec),
                  pl.BlockSpec((1, d), lambda bi, i, j: (0, 0))],
        out_specs=pl.BlockSpec((1, tm, d), lambda bi, i, j: (bi, i, 0)),
        out_shape=jax.ShapeDtypeStruct((b, l, d), F32),
        scratch_shapes=[pltpu.VMEM((tm, d), BF16), pltpu.VMEM((tm, d), F32)],
        compiler_params=_cparams(("parallel", "parallel", "arbitrary")),
        name="ffn",
    )(x, nw.reshape(1, d), sc, sh, wg, wu, wd, gate, fnw.reshape(1, d))


def _box_matrix(n, w):
    idx = np.arange(n)
    lo = np.clip(idx - w // 2, 0, n)
    hi = np.clip(idx + (w - w // 2), 0, n)
    m = ((idx[None, :] >= lo[:, None]) & (idx[None, :] < hi[:, None])).astype(np.float64)
    return m / (hi - lo)[:, None]


POOL_TB = 256
POOL_PAD = 16


def _pool_kernel(u_ref, mh_ref, ml_ref, o_ref, *scratch, grid_rows):
    g = pl.program_id(1)
    l = u_ref.shape[1]
    mh = mh_ref[0]
    ml = ml_ref[0]
    if grid_rows is None:
        v = u_ref[0].astype(F32)
        o_ref[0] = (_dot_cd(mh, ml, v) - v).astype(o_ref.dtype)
        return

    cp_ref, = scratch
    pad = POOL_PAD * GRID_W
    cp_ref[0:pad, :] = jnp.zeros((pad, POOL_GC), F32)
    cp_ref[pad + l:pad + l + pad, :] = jnp.zeros((pad, POOL_GC), F32)
    for i in range(l // POOL_TB):
        v = u_ref[0, i * POOL_TB:(i + 1) * POOL_TB, :].astype(F32)
        cp_ref[pad + i * POOL_TB:pad + (i + 1) * POOL_TB, :] = _dot_cd(mh, ml, v)

    for gi, w in enumerate(POOL_WINDOWS):
        lo_off, hi_off = -(w // 2), w - w // 2

        @pl.when(g == gi)
        def _(lo_off=lo_off, hi_off=hi_off):
            def body(r, carry):
                s = jnp.zeros((GRID_W, POOL_GC), F32)
                for o in range(lo_off, hi_off):
                    start = pl.multiple_of(pad + (r + o) * GRID_W, GRID_W)
                    s = s + cp_ref[pl.ds(start, GRID_W), :]
                cnt = jnp.minimum(r + hi_off, grid_rows) - jnp.maximum(r + lo_off, 0)
                cntv = jnp.full((GRID_W, POOL_GC), cnt, jnp.int32).astype(F32)
                t0 = pl.multiple_of(r * GRID_W, GRID_W)
                v = u_ref[0, pl.ds(t0, GRID_W), :].astype(F32)
                o_ref[0, pl.ds(t0, GRID_W), :] = (s / cntv - v).astype(o_ref.dtype)
                return carry

            lax.fori_loop(0, grid_rows, body, 0)


def _pool_diff(p, grid_rows):
    b, l, _ = p.shape
    mats = []
    for w in POOL_WINDOWS:
        if grid_rows is None:
            mats.append(_box_matrix(l, w))
        else:
            mats.append(np.kron(np.eye(POOL_TB // GRID_W), _box_matrix(GRID_W, w)))
    mh, ml = _np_split2(np.stack(mats))
    tb = mats[0].shape[0]
    scratch = []
    if grid_rows is not None:
        scratch = [pltpu.VMEM((l + 2 * POOL_PAD * GRID_W, POOL_GC), F32)]
    cblk = P_POOL // POOL_GC
    return pl.pallas_call(
        functools.partial(_pool_kernel, grid_rows=grid_rows),
        grid=(b, POOL_GROUPS),
        in_specs=[pl.BlockSpec((1, l, POOL_GC), lambda bi, g: (bi, 0, cblk + g)),
                  pl.BlockSpec((1, tb, tb), lambda bi, g: (g, 0, 0)),
                  pl.BlockSpec((1, tb, tb), lambda bi, g: (g, 0, 0))],
        out_specs=pl.BlockSpec((1, l, POOL_GC), lambda bi, g: (bi, 0, g)),
        out_shape=jax.ShapeDtypeStruct((b, l, POOL_W), BF16),
        scratch_shapes=scratch,
        compiler_params=_cparams(("parallel", "parallel")),
        name="pool_diff",
    )(p, mh, ml)


def _dft_cs(n):
    k = np.arange(n)
    ang = -2.0 * np.pi * ((k[:, None] * k[None, :]) % n) / n
    return np.cos(ang), np.sin(ang)


def _chdft_kernel(u_ref, wh_ref, wl_ref, o_ref):
    o_ref[0, 0] = _dot_dc(u_ref[0].astype(F32), wh_ref[...], wl_ref[...])


def _chdft(p, tm):
    b, l, _ = p.shape
    tm = min(tm, l)
    cr, ci = _dft_cs(FOUR_GC)
    wh, wl = _np_split2(np.concatenate([cr, ci], axis=1))
    cblk = P_FOUR // FOUR_GC
    return pl.pallas_call(
        _chdft_kernel,
        grid=(b, FOUR_GROUPS, l // tm),
        in_specs=[pl.BlockSpec((1, tm, FOUR_GC), lambda bi, g, i: (bi, i, cblk + g)),
                  pl.BlockSpec((FOUR_GC, 2 * FOUR_GC), lambda bi, g, i: (0, 0)),
                  pl.BlockSpec((FOUR_GC, 2 * FOUR_GC), lambda bi, g, i: (0, 0))],
        out_specs=pl.BlockSpec((1, 1, tm, 2 * FOUR_GC), lambda bi, g, i: (bi, g, i, 0)),
        out_shape=jax.ShapeDtypeStruct((b, FOUR_GROUPS, l, 2 * FOUR_GC), F32),
        compiler_params=_cparams(("parallel", "parallel", "parallel")),
        name="chdft",
    )(p, wh, wl)


def _cplx_mm(wh, wl, blk, n):
    m = _dot_cd(wh, wl, blk)
    c = FOUR_GC
    re = m[:n, :c] - m[n:, c:]
    im = m[:n, c:] + m[n:, :c]
    return re, im


SEQ_SUB = 8


def _seqdft1_kernel(z_ref, wh_ref, wl_ref, tr_ref, ti_ref, o_ref):
    wh = wh_ref[...]
    wl = wl_ref[...]
    c = FOUR_GC
    for i in range(SEQ_SUB):
        blk = z_ref[0, 0, :, i * 2 * c:(i + 1) * 2 * c]
        ar, ai = _cplx_mm(wh, wl, blk, GRID_W)
        tr = jnp.concatenate([tr_ref[i], tr_ref[i]], axis=1)
        ti = jnp.concatenate([ti_ref[i], ti_ref[i]], axis=1)
        o_ref[0, 0, i, :, :c] = ar * tr - ai * ti
        o_ref[0, 0, i, :, c:] = ar * ti + ai * tr


def _seqdft2_kernel(a_ref, wh_ref, wl_ref, o_ref, *, n, scale):
    wh = wh_ref[...]
    wl = wl_ref[...]
    c = FOUR_GC
    nsub = a_ref.shape[3] // (2 * c)
    for g in range(FOUR_GROUPS):
        for i in range(nsub):
            blk = a_ref[0, g, :, i * 2 * c:(i + 1) * 2 * c]
            m = _dot_cd(wh, wl, blk)
            re = m[:n, :c] - m[n:, c:]
            o_ref[0, :, i * FOUR_W + g * c:i * FOUR_W + (g + 1) * c] = (re * scale).astype(o_ref.dtype)


def _fourier_latent(p):
    b, l, _ = p.shape
    n = GRID_W
    assert l == n * n
    c2 = 2 * FOUR_GC
    z = _chdft(p, 512)
    wr, wi = _dft_cs(n)
    wh, wl = _np_split2(np.concatenate([wr, wi], axis=0))
    k = np.arange(n)
    ang = -2.0 * np.pi * (k[:, None] * k[None, :]) / l
    tr = jnp.asarray(np.broadcast_to(np.cos(ang)[:, :, None], (n, n, 128)).astype(np.float32))
    ti = jnp.asarray(np.broadcast_to(np.sin(ang)[:, :, None], (n, n, 128)).astype(np.float32))
    zv = z.reshape(b, FOUR_GROUPS, n, n * c2)
    a = pl.pallas_call(
        _seqdft1_kernel,
        grid=(b, FOUR_GROUPS, n // SEQ_SUB),
        in_specs=[pl.BlockSpec((1, 1, n, SEQ_SUB * c2), lambda bi, g, j: (bi, g, 0, j)),
                  pl.BlockSpec((2 * n, n), lambda bi, g, j: (0, 0)),
                  pl.BlockSpec((2 * n, n), lambda bi, g, j: (0, 0)),
                  pl.BlockSpec((SEQ_SUB, n, 128), lambda bi, g, j: (j, 0, 0)),
                  pl.BlockSpec((SEQ_SUB, n, 128), lambda bi, g, j: (j, 0, 0))],
        out_specs=pl.BlockSpec((1, 1, SEQ_SUB, n, c2), lambda bi, g, j: (bi, g, j, 0, 0)),
        out_shape=jax.ShapeDtypeStruct((b, FOUR_GROUPS, n, n, c2), F32),
        compiler_params=_cparams(("parallel", "parallel", "parallel")),
        name="seqdft1",
    )(zv, wh, wl, tr, ti)
    av = a.reshape(b, FOUR_GROUPS, n, n * c2)
    scale = 1.0 / math.sqrt(l * FOUR_GC)
    f = pl.pallas_call(
        functools.partial(_seqdft2_kernel, n=n, scale=scale),
        grid=(b, n // SEQ_SUB),
        in_specs=[pl.BlockSpec((1, FOUR_GROUPS, n, SEQ_SUB * c2), lambda bi, j: (bi, 0, 0, j)),
                  pl.BlockSpec((2 * n, n), lambda bi, j: (0, 0)),
                  pl.BlockSpec((2 * n, n), lambda bi, j: (0, 0))],
        out_specs=pl.BlockSpec((1, n, SEQ_SUB * FOUR_W), lambda bi, j: (bi, 0, j)),
        out_shape=jax.ShapeDtypeStruct((b, n, n * FOUR_W), BF16),
        compiler_params=_cparams(("parallel", "parallel")),
        name="seqdft2",
    )(av, wh, wl)
    return f.reshape(b, l, FOUR_W)


def _fourier_ctx(p):
    b, l, _ = p.shape
    c2 = 2 * FOUR_GC
    z = _chdft(p, l)
    wr, wi = _dft_cs(l)
    wh, wl = _np_split2(np.concatenate([wr, wi], axis=0))
    scale = 1.0 / math.sqrt(l * FOUR_GC)
    return pl.pallas_call(
        functools.partial(_seqdft2_kernel, n=l, scale=scale),
        grid=(b,),
        in_specs=[pl.BlockSpec((1, FOUR_GROUPS, l, c2), lambda bi: (bi, 0, 0, 0)),
                  pl.BlockSpec((2 * l, l), lambda bi: (0, 0)),
                  pl.BlockSpec((2 * l, l), lambda bi: (0, 0))],
        out_specs=pl.BlockSpec((1, l, FOUR_W), lambda bi: (bi, 0, 0)),
        out_shape=jax.ShapeDtypeStruct((b, l, FOUR_W), BF16),
        compiler_params=_cparams(("parallel",)),
        name="seqdft_ctx",
    )(z, wh, wl)


CONV_CB = 512


def _ssd_kernel(*refs, fwd, nc, epilogue):
    if epilogue:
        (xm_ref, xp_ref, xn_ref, dt_ref, cw_ref, cb_ref, dtb_ref, alog_ref, e_ref, h0_ref,
         z_ref, yb_ref, dsk_ref, nw_ref, y_ref, hout_ref, st_ref, ext_ref, xc_ref) = refs
    else:
        (xm_ref, xp_ref, xn_ref, dt_ref, cw_ref, cb_ref, dtb_ref, alog_ref, e_ref, h0_ref,
         y_ref, hout_ref, st_ref, ext_ref, xc_ref) = refs
    c = pl.program_id(1)
    cc = c if fwd else nc - 1 - c
    T = CHUNK

    @pl.when(c == 0)
    def _():
        st_ref[...] = h0_ref[0]

    keep_p = jnp.where(cc > 0, 1.0, 0.0)
    keep_n = jnp.where(cc < nc - 1, 1.0, 0.0)
    ext_ref[0:HALO, :] = xp_ref[0].astype(F32) * keep_p
    ext_ref[HALO:HALO + T, :] = xm_ref[0].astype(F32)
    ext_ref[HALO + T:HALO + T + HALO, :] = xn_ref[0].astype(F32) * keep_n
    for j in range(XBC // CONV_CB):
        cs = slice(j * CONV_CB, (j + 1) * CONV_CB)
        acc = jnp.broadcast_to(cb_ref[:, cs], (T, CONV_CB))
        for k in range(CONV_W):
            r0 = HALO - CONV_W // 2 + k
            acc = acc + ext_ref[r0:r0 + T, cs] * cw_ref[k:k + 1, cs]
        xc_ref[:, cs] = _silu(acc)

    xdt = dt_ref[0] + dtb_ref[...]
    dt = jnp.maximum(xdt, 0.0) + jnp.log1p(jnp.exp(-jnp.abs(xdt)))
    a = -jnp.exp(alog_ref[...])
    da = dt * a
    row = lax.broadcasted_iota(jnp.int32, (T, T), 0)
    col = lax.broadcasted_iota(jnp.int32, (T, T), 1)
    causal = (col <= row) if fwd else (col >= row)
    tri = jnp.where(causal, 1.0, 0.0).astype(BF16)
    d1 = da.astype(BF16)
    r1 = da - d1.astype(F32)
    d2 = r1.astype(BF16)
    d3 = (r1 - d2.astype(F32)).astype(BF16)
    acs = _dot(tri, d1) + _dot(tri, d2) + _dot(tri, d3)
    tot = acs[T - 1:T, :] if fwd else acs[0:1, :]
    eacs = jnp.exp(acs)
    dte = jnp.exp(tot - acs)
    etot = jnp.broadcast_to(jnp.exp(tot), (8, T))

    e = e_ref[...]

    def expand(v):
        hi, lo = _split2(v)
        return _dot(hi, e) + _dot(lo, e)

    dtx = expand(dt)
    eacs_x = expand(eacs)
    w2 = expand(dt * dte)
    etot_x = expand(etot)[0:1, :]
    acs_t = acs.T
    lane = lax.broadcasted_iota(jnp.int32, (T, 2 * HEAD_DIM), 1)
    hbase = 0 if fwd else SSD_HEADS

    for g in range(SSD_GROUPS):
        gs = slice(g * GROUP_W, (g + 1) * GROUP_W)
        bg = xc_ref[:, D_SSD + g * D_STATE:D_SSD + (g + 1) * D_STATE]
        cg = xc_ref[:, D_SSD + (SSD_GROUPS + g) * D_STATE:D_SSD + (SSD_GROUPS + g + 1) * D_STATE]
        bb = bg.astype(BF16)
        cbf = cg.astype(BF16)
        cb = lax.dot_general(cbf, bb, (((1,), (1,)), ((), ())), preferred_element_type=F32)
        st = st_ref[g]
        yoff = _dot(cbf, st.astype(BF16)) * eacs_x[:, gs]
        xg = xc_ref[:, gs]
        xdb = (xg * dtx[:, gs]).astype(BF16)
        ys = []
        for jp in range(HEADS_PER_GROUP // 2):
            ms = []
            for jj in range(2):
                hc = hbase + g * HEADS_PER_GROUP + 2 * jp + jj
                diff = acs[:, hc:hc + 1] - acs_t[hc:hc + 1, :]
                lm = jnp.exp(jnp.where(causal, diff, -1e30))
                ms.append((cb * lm).astype(BF16))
            r = _dot(jnp.concatenate(ms, axis=0), xdb[:, jp * 2 * HEAD_DIM:(jp + 1) * 2 * HEAD_DIM])
            ys.append(jnp.where(lane < HEAD_DIM, r[:T], r[T:]))
        yg = jnp.concatenate(ys, axis=1) + yoff

        xds = (xg * w2[:, gs]).astype(BF16)
        st_ref[g] = st * etot_x[:, gs] + _dot(bg.T.astype(BF16), xds)

        if epilogue:
            yt = yg + yb_ref[0, :, gs] + dsk_ref[:, gs] * xg
            v = yt * _silu(z_ref[0, :, gs].astype(F32))
            ms_ = jnp.mean(v * v, axis=-1, keepdims=True)
            y_ref[0, :, gs] = ((v * lax.rsqrt(ms_ + EPS)) * nw_ref[:, gs]).astype(y_ref.dtype)
        else:
            y_ref[0, :, gs] = yg

    @pl.when(c == nc - 1)
    def _():
        hout_ref[0] = st_ref[...]


def _ssd_pass(fwd, p, pdt, cw, cb, dtb, alog, e, h0, extra):
    b, l, _ = p.shape
    nc = l // CHUNK
    nh = l // HALO
    per = CHUNK // HALO
    cidx = (lambda c: c) if fwd else (lambda c: nc - 1 - c)
    chunk = lambda bi, c: (bi, cidx(c), 0)
    const2 = lambda bi, c: (0, 0)
    in_specs = [
        pl.BlockSpec((1, CHUNK, XBC), chunk),
        pl.BlockSpec((1, HALO, XBC), lambda bi, c: (bi, jnp.maximum(cidx(c) * per - 1, 0), 0)),
        pl.BlockSpec((1, HALO, XBC), lambda bi, c: (bi, jnp.minimum((cidx(c) + 1) * per, nh - 1), 0)),
        pl.BlockSpec((1, CHUNK, DT_PAD), chunk),
        pl.BlockSpec((8, XBC), const2),
        pl.BlockSpec((1, XBC), const2),
        pl.BlockSpec((1, DT_PAD), const2),
        pl.BlockSpec((1, DT_PAD), const2),
        pl.BlockSpec((DT_PAD, D_SSD), const2),
        pl.BlockSpec((1, SSD_GROUPS, D_STATE, GROUP_W), lambda bi, c: (bi, 0, 0, 0)),
    ]
    args = [p, p, p, pdt, cw, cb, dtb, alog, e, h0]
    epilogue = extra is not None
    if epilogue:
        z_blk = P_Z // D_SSD
        yb, dsk, nw = extra
        in_specs += [pl.BlockSpec((1, CHUNK, D_SSD), lambda bi, c: (bi, cidx(c), z_blk)),
                     pl.BlockSpec((1, CHUNK, D_SSD), chunk),
                     pl.BlockSpec((1, D_SSD), const2),
                     pl.BlockSpec((1, D_SSD), const2)]
        args += [p, yb, dsk, nw]
        y_dtype = BF16
    else:
        y_dtype = F32
    y, hout = pl.pallas_call(
        functools.partial(_ssd_kernel, fwd=fwd, nc=nc, epilogue=epilogue),
        grid=(b, nc),
        in_specs=in_specs,
        out_specs=[pl.BlockSpec((1, CHUNK, D_SSD), chunk),
                   pl.BlockSpec((1, SSD_GROUPS, D_STATE, GROUP_W), lambda bi, c: (bi, 0, 0, 0))],
        out_shape=[jax.ShapeDtypeStruct((b, l, D_SSD), y_dtype),
                   jax.ShapeDtypeStruct((b, SSD_GROUPS, D_STATE, GROUP_W), F32)],
        scratch_shapes=[pltpu.VMEM((SSD_GROUPS, D_STATE, GROUP_W), F32),
                        pltpu.VMEM((CHUNK + 2 * HALO, XBC), F32),
                        pltpu.VMEM((CHUNK, XBC), F32)],
        compiler_params=_cparams(("parallel", "arbitrary")),
        name="ssd_fwd" if fwd else "ssd_bwd",
    )(*args)
    return y, hout


def _ssd(p, pdt, lw, h0f, h0b):
    yb, hb = _ssd_pass(False, p, pdt, lw["conv_w"], lw["conv_b"], lw["dt_bias"], lw["a_log"], lw["expand_b"],
                       h0b, None)
    y, hf = _ssd_pass(True, p, pdt, lw["conv_w"], lw["conv_b"], lw["dt_bias"], lw["a_log"], lw["expand_f"],
                      h0f, (yb, lw["d_skip"], lw["ssd_norm_w"]))
    return y, hf, hb


def _expand_matrix(offset):
    e = np.zeros((DT_PAD, D_SSD), np.float32)
    for h in range(SSD_HEADS):
        e[offset + h, h * HEAD_DIM:(h + 1) * HEAD_DIM] = 1.0
    return jnp.asarray(e).astype(BF16)


def _layer_weights(l, w_in, conv_w, conv_b, a_log, dt_bias, d_skip, ssd_norm_w, w_ssd_out, w_pool,
                   pool_scale, w_fourier, w_out, w_ffn_gate, w_ffn_up, w_ffn_down):
    wi = w_in[l]
    w_main = jnp.concatenate([wi[:, :XBC], wi[:, POOL_OFF:FOUR_OFF], wi[:, Z_OFF:POOL_OFF],
                              wi[:, GATE_OFF:], wi[:, FOUR_OFF:GATE_OFF]], axis=1).astype(BF16)
    w_dt = jnp.pad(wi[:, XBC:SSD_IN], ((0, 0), (0, DT_PAD - 2 * SSD_HEADS))).astype(BF16)
    pad_h = (0, DT_PAD - 2 * SSD_HEADS)
    return {
        "w_main": w_main,
        "w_dt": w_dt,
        "conv_w": jnp.pad(conv_w[l], ((0, 8 - CONV_W), (0, 0))),
        "conv_b": conv_b[l].reshape(1, XBC),
        "dt_bias": jnp.pad(dt_bias[l].reshape(-1), pad_h).reshape(1, DT_PAD),
        "a_log": jnp.pad(a_log[l].reshape(-1), pad_h).reshape(1, DT_PAD),
        "d_skip": jnp.repeat(d_skip[l], HEAD_DIM).reshape(1, D_SSD),
        "ssd_norm_w": ssd_norm_w[l].reshape(1, D_SSD),
        "expand_f": _expand_matrix(0),
        "expand_b": _expand_matrix(SSD_HEADS),
        "w_ssd_out": w_ssd_out[l].astype(BF16),
        "w_pool": w_pool[l].astype(BF16),
        "pool_scale": pool_scale[l],
        "w_fourier": w_fourier[l].astype(BF16),
        "w_out": w_out[l].astype(BF16),
        "w_ffn_gate": w_ffn_gate[l].astype(BF16),
        "w_ffn_up": w_ffn_up[l].astype(BF16),
        "w_ffn_down": w_ffn_down[l].astype(BF16),
    }


def _mixer(x, nw, sc, sh, g_m, lw, grid_rows, h0f, h0b):
    p = _norm_mm(x, nw, sc, sh, lw["w_main"], P_TOT, 1024, 1024)
    pdt = _norm_mm(x, nw, sc, sh, lw["w_dt"], DT_PAD, 1024, DT_PAD)
    y_n, hf, hb = _ssd(p, pdt, lw, h0f, h0b)
    y_ssd = _mm(y_n, lw["w_ssd_out"], 1024, 1024)
    d = _pool_diff(p, grid_rows)
    y_pool = _pool_mm(d, lw["w_pool"], lw["pool_scale"], 1024)
    f = _fourier_latent(p) if grid_rows is not None else _fourier_ctx(p)
    y_four = _mm(f, lw["w_fourier"], 1024, 1024)
    x = _merge_out(y_pool, y_four, y_ssd, p, lw["w_out"], x, g_m, 256, 1024)
    return x, hf, hb


def kernel(x, c, ctx, c_ctx, w_ada, b_ada, norm_mix_w, norm_ffn_w, w_in, conv_w, conv_b, a_log, dt_bias,
           d_skip, ssd_norm_w, w_ssd_out, w_pool, pool_scale, w_fourier, w_out, w_ffn_gate, w_ffn_up,
           w_ffn_down, final_norm_w):
    b, seq, d = x.shape
    rows = seq // GRID_W
    cc = jnp.concatenate([c, c_ctx[None, :], jnp.zeros((8 - b - 1, d), F32)], axis=0)
    mod = _adaln(cc, w_ada, b_ada)
    h0 = jnp.zeros((b, SSD_GROUPS, D_STATE, GROUP_W), F32)
    for l in range(DEPTH):
        last = l == DEPTH - 1
        lw = _layer_weights(l, w_in, conv_w, conv_b, a_log, dt_bias, d_skip, ssd_norm_w, w_ssd_out, w_pool,
                            pool_scale, w_fourier, w_out, w_ffn_gate, w_ffn_up, w_ffn_down)
        sh_m, sc_m, g_m, sh_f, sc_f, g_f = [mod[l, :b, i * d:(i + 1) * d].reshape(b, 1, d) for i in range(6)]
        csh_m, csc_m, cg_m, csh_f, csc_f, cg_f = [
            jnp.broadcast_to(mod[l, b, i * d:(i + 1) * d].reshape(1, 1, d), (b, 1, d)) for i in range(6)]

        if last:
            pc = _norm_mm(ctx, norm_mix_w[l], csc_m, csh_m, lw["w_main"], XBC, 256, 1024)
            pdtc = _norm_mm(ctx, norm_mix_w[l], csc_m, csh_m, lw["w_dt"], DT_PAD, 256, DT_PAD)
            _, hb = _ssd_pass(False, pc, pdtc, lw["conv_w"], lw["conv_b"], lw["dt_bias"], lw["a_log"],
                              lw["expand_b"], h0, None)
            _, hf = _ssd_pass(True, pc, pdtc, lw["conv_w"], lw["conv_b"], lw["dt_bias"], lw["a_log"],
                              lw["expand_f"], h0, None)
        else:
            ctx, hf, hb = _mixer(ctx, norm_mix_w[l], csc_m, csh_m, cg_m, lw, None, h0, h0)
            ctx = _ffn(ctx, norm_ffn_w[l], csc_f, csh_f, lw["w_ffn_gate"], lw["w_ffn_up"], lw["w_ffn_down"],
                       cg_f, final_norm_w, False, 256, 512)

        x, _, _ = _mixer(x, norm_mix_w[l], sc_m, sh_m, g_m, lw, rows, hf, hb)
        x = _ffn(x, norm_ffn_w[l], sc_f, sh_f, lw["w_ffn_gate"], lw["w_ffn_up"], lw["w_ffn_down"],
                 g_f, final_norm_w, last, 512, 512)
    return x
```

```python
import functools
import math

import numpy as np
import jax
import jax.numpy as jnp
from jax import lax
from jax.experimental import pallas as pl
from jax.experimental.pallas import tpu as pltpu

F32 = jnp.float32
BF16 = jnp.bfloat16

D_MODEL = 2048
DEPTH = 2
GRID_W = 64
EPS = 1e-6

POOL_GROUPS = 4
POOL_WINDOWS = (2, 4, 8, 16)
POOL_W = D_MODEL // 2
POOL_GC = POOL_W // POOL_GROUPS
POOL_OUT = D_MODEL // POOL_GROUPS

FOUR_GROUPS = 4
FOUR_W = D_MODEL // 2
FOUR_GC = FOUR_W // FOUR_GROUPS

D_SSD = D_MODEL
HEAD_DIM = 64
SSD_HEADS = D_SSD // HEAD_DIM
SSD_GROUPS = 4
HEADS_PER_GROUP = SSD_HEADS // SSD_GROUPS
GROUP_W = HEADS_PER_GROUP * HEAD_DIM
D_STATE = 128
CONV_W = 5
CHUNK = 128

D_FF = ((8 * D_MODEL // 3 + 255) // 256) * 256

XBC = D_SSD + 2 * SSD_GROUPS * D_STATE
SSD_IN = XBC + 2 * SSD_HEADS
Z_OFF = SSD_IN
POOL_OFF = Z_OFF + D_SSD
FOUR_OFF = POOL_OFF + POOL_W
GATE_OFF = FOUR_OFF + FOUR_W
N_IN = GATE_OFF + 3 * D_MODEL

P_XBC = 0
P_POOL = XBC
P_Z = P_POOL + POOL_W
P_GATE = P_Z + D_SSD
P_FOUR = P_GATE + 3 * D_MODEL
P_TOT = P_FOUR + FOUR_W
DT_PAD = 128

V7X_VMEM_LIMIT = 56 * 1024 * 1024
HALO = 16


def _cparams(sem):
    return pltpu.CompilerParams(dimension_semantics=sem, vmem_limit_bytes=V7X_VMEM_LIMIT)


def _resident(shape, index_map):
    return pl.BlockSpec(shape, index_map, pipeline_mode=pl.Buffered(1))


def _split2(v):
    hi = v.astype(BF16)
    lo = (v - hi.astype(F32)).astype(BF16)
    return hi, lo


def _dot(a, b):
    return jnp.dot(a, b, preferred_element_type=F32)


def _sigmoid(v):
    return 0.5 * jnp.tanh(0.5 * v) + 0.5


def _silu(v):
    return v * _sigmoid(v)


def _np_split2(m):
    m = jnp.asarray(np.asarray(m, np.float32))
    hi = m.astype(BF16)
    lo = (m - hi.astype(F32)).astype(BF16)
    return hi, lo


def _adaln_kernel(c_ref, w_ref, b_ref, o_ref):
    s = _silu(c_ref[...])
    w = w_ref[0]
    sh, sl = _split2(s)
    wh, wl = _split2(w)
    o_ref[0] = _dot(sh, wh) + _dot(sl, wh) + _dot(sh, wl) + b_ref[0]


def _adaln(cc, w_ada, b_ada, tn=1024):
    depth, d, n = w_ada.shape
    return pl.pallas_call(
        _adaln_kernel,
        grid=(depth, n // tn),
        in_specs=[pl.BlockSpec((8, d), lambda l, j: (0, 0)),
                  pl.BlockSpec((1, d, tn), lambda l, j: (l, 0, j)),
                  pl.BlockSpec((1, 1, tn), lambda l, j: (l, 0, j))],
        out_specs=pl.BlockSpec((1, 8, tn), lambda l, j: (l, 0, j)),
        out_shape=jax.ShapeDtypeStruct((depth, 8, n), F32),
        compiler_params=_cparams(("parallel", "parallel")),
        name="adaln",
    )(cc, w_ada, b_ada.reshape(depth, 1, n))


def _norm_mod(x, nw, sc, sh):
    ms = jnp.mean(x * x, axis=-1, keepdims=True)
    return (x * lax.rsqrt(ms + EPS)) * nw * (1.0 + sc) + sh


def _inproj_kernel(x_ref, nw_ref, sc_ref, sh_ref, w_ref, wdt_ref, o_ref, dt_ref, h_ref):
    @pl.when(pl.program_id(2) == 0)
    def _():
        h = _norm_mod(x_ref[0], nw_ref[...], sc_ref[0], sh_ref[0]).astype(BF16)
        h_ref[...] = h
        dt_ref[0] = _dot(h, wdt_ref[...])

    o_ref[0] = _dot(h_ref[...], w_ref[...]).astype(o_ref.dtype)


def _inproj(x, nw, sc, sh, w, wdt, n_out, tm, tn):
    b, l, d = x.shape
    tm = min(tm, l)
    return pl.pallas_call(
        _inproj_kernel,
        grid=(b, l // tm, n_out // tn),
        in_specs=[pl.BlockSpec((1, tm, d), lambda bi, i, j: (bi, i, 0)),
                  pl.BlockSpec((1, d), lambda bi, i, j: (0, 0)),
                  pl.BlockSpec((1, 1, d), lambda bi, i, j: (bi, 0, 0)),
                  pl.BlockSpec((1, 1, d), lambda bi, i, j: (bi, 0, 0)),
                  pl.BlockSpec((d, tn), lambda bi, i, j: (0, j)),
                  _resident((d, DT_PAD), lambda bi, i, j: (0, 0))],
        out_specs=[pl.BlockSpec((1, tm, tn), lambda bi, i, j: (bi, i, j)),
                   pl.BlockSpec((1, tm, DT_PAD), lambda bi, i, j: (bi, i, 0))],
        out_shape=[jax.ShapeDtypeStruct((b, l, n_out), BF16),
                   jax.ShapeDtypeStruct((b, l, DT_PAD), F32)],
        scratch_shapes=[pltpu.VMEM((tm, d), BF16)],
        compiler_params=_cparams(("parallel", "parallel", "arbitrary")),
        name="inproj",
    )(x, nw.reshape(1, d), sc, sh, w, wdt)


def _merge_kernel(d_ref, f_ref, yn_ref, gp_ref, gf_ref, gs_ref, x_ref, gm_ref,
                  wp_ref, ps_ref, wf_ref, ws_ref, wo_ref, o_ref):
    d = d_ref[0]
    y_pool = jnp.concatenate(
        [_dot(d[:, g * POOL_GC:(g + 1) * POOL_GC], wp_ref[g]) for g in range(POOL_GROUPS)], axis=1)
    m = _sigmoid(gp_ref[0].astype(F32)) * (y_pool * ps_ref[...])
    m = m + _sigmoid(gf_ref[0].astype(F32)) * _dot(f_ref[0], wf_ref[...])
    m = m + _sigmoid(gs_ref[0].astype(F32)) * _dot(yn_ref[0], ws_ref[...])
    o_ref[0] = x_ref[0] + gm_ref[0] * _dot(m.astype(BF16), wo_ref[...])


def _merge_out(d, f, yn, p, x, g_m, lw, tm):
    b, l, dm = x.shape
    tm = min(tm, l)
    gblk = P_GATE // dm
    row = lambda bi, i: (bi, i, 0)
    c2 = lambda bi, i: (0, 0)
    return pl.pallas_call(
        _merge_kernel,
        grid=(b, l // tm),
        in_specs=[pl.BlockSpec((1, tm, POOL_W), row),
                  pl.BlockSpec((1, tm, FOUR_W), row),
                  pl.BlockSpec((1, tm, D_SSD), row),
                  pl.BlockSpec((1, tm, dm), lambda bi, i: (bi, i, gblk)),
                  pl.BlockSpec((1, tm, dm), lambda bi, i: (bi, i, gblk + 1)),
                  pl.BlockSpec((1, tm, dm), lambda bi, i: (bi, i, gblk + 2)),
                  pl.BlockSpec((1, tm, dm), row),
                  pl.BlockSpec((1, 1, dm), lambda bi, i: (bi, 0, 0)),
                  _resident((POOL_GROUPS, POOL_GC, POOL_OUT), lambda bi, i: (0, 0, 0)),
                  _resident((1, dm), c2),
                  _resident((FOUR_W, dm), c2),
                  _resident((D_SSD, dm), c2),
                  _resident((dm, dm), c2)],
        out_specs=pl.BlockSpec((1, tm, dm), row),
        out_shape=jax.ShapeDtypeStruct((b, l, dm), F32),
        compiler_params=_cparams(("parallel", "parallel")),
        name="merge_out",
    )(d, f, yn, p, p, p, x, g_m, lw["w_pool"], lw["pool_scale"], lw["w_fourier"], lw["w_ssd_out"], lw["w_out"])


def _ffn_kernel(x_ref, nw_ref, sc_ref, sh_ref, wg_ref, wu_ref, wd_ref, gate_ref, fnw_ref,
                o_ref, h_ref, acc_ref, *, final_norm):
    j = pl.program_id(2)

    @pl.when(j == 0)
    def _():
        h_ref[...] = _norm_mod(x_ref[0], nw_ref[...], sc_ref[0], sh_ref[0]).astype(BF16)
        acc_ref[...] = jnp.zeros_like(acc_ref)

    h = h_ref[...]
    a = _silu(_dot(h, wg_ref[...])) * _dot(h, wu_ref[...])
    acc_ref[...] += _dot(a.astype(BF16), wd_ref[...])

    @pl.when(j == pl.num_programs(2) - 1)
    def _():
        o = x_ref[0] + gate_ref[0] * acc_ref[...]
        if final_norm:
            ms = jnp.mean(o * o, axis=-1, keepdims=True)
            o = (o * lax.rsqrt(ms + EPS)) * fnw_ref[...]
        o_ref[0] = o


def _ffn(x, nw, sc, sh, wg, wu, wd, gate, fnw, final_norm, tm, tf):
    b, l, d = x.shape
    ff = wg.shape[1]
    tm = min(tm, l)
    vec = lambda bi, i, j: (bi, 0, 0)
    return pl.pallas_call(
        functools.partial(_ffn_kernel, final_norm=final_norm),
        grid=(b, l // tm, ff // tf),
        in_specs=[pl.BlockSpec((1, tm, d), lambda bi, i, j: (bi, i, 0)),
                  pl.BlockSpec((1, d), lambda bi, i, j: (0, 0)),
                  pl.BlockSpec((1, 1, d), vec),
                  pl.BlockSpec((1, 1, d), vec),
                  pl.BlockSpec((d, tf), lambda bi, i, j: (0, j)),
                  pl.BlockSpec((d, tf), lambda bi, i, j: (0, j)),
                  pl.BlockSpec((tf, d), lambda bi, i, j: (j, 0)),
                  pl.BlockSpec((1, 1, d), vec),
                  pl.BlockSpec((1, d), lambda bi, i, j: (0, 0))],
        out_specs=pl.BlockSpec((1, tm, d), lambda bi, i, j: (bi, i, 0)),
        out_shape=jax.ShapeDtypeStruct((b, l, d), F32),
        scratch_shapes=[pltpu.VMEM((tm, d), BF16), pltpu.VMEM((tm, d), F32)],
        compiler_params=_cparams(("parallel", "parallel", "arbitrary")),
        name="ffn",
    )(x, nw.reshape(1, d), sc, sh, wg, wu, wd, gate, fnw.reshape(1, d))


def _box_matrix(n, w):
    idx = np.arange(n)
    lo = np.clip(idx - w // 2, 0, n)
    hi = np.clip(idx + (w - w // 2), 0, n)
    m = ((idx[None, :] >= lo[:, None]) & (idx[None, :] < hi[:, None])).astype(np.float64)
    return m / (hi - lo)[:, None]


POOL_TB = 256
POOL_PAD = 16


def _pool_kernel(u_ref, mh_ref, ml_ref, o_ref, *scratch, grid_rows):
    g = pl.program_id(1)
    l = u_ref.shape[1]
    mh = mh_ref[0]
    ml = ml_ref[0]
    if grid_rows is None:
        v = u_ref[0]
        o_ref[0] = (_dot(mh, v) + _dot(ml, v) - v.astype(F32)).astype(o_ref.dtype)
        return

    cp_ref, = scratch
    pad = POOL_PAD * GRID_W
    cp_ref[0:pad, :] = jnp.zeros((pad, POOL_GC), F32)
    cp_ref[pad + l:pad + l + pad, :] = jnp.zeros((pad, POOL_GC), F32)
    for i in range(l // POOL_TB):
        v = u_ref[0, i * POOL_TB:(i + 1) * POOL_TB, :]
        cp_ref[pad + i * POOL_TB:pad + (i + 1) * POOL_TB, :] = _dot(mh, v) + _dot(ml, v)

    for gi, w in enumerate(POOL_WINDOWS):
        lo_off, hi_off = -(w // 2), w - w // 2

        @pl.when(g == gi)
        def _(lo_off=lo_off, hi_off=hi_off):
            def body(r, carry):
                s = jnp.zeros((GRID_W, POOL_GC), F32)
                for o in range(lo_off, hi_off):
                    start = pl.multiple_of(pad + (r + o) * GRID_W, GRID_W)
                    s = s + cp_ref[pl.ds(start, GRID_W), :]
                cnt = jnp.minimum(r + hi_off, grid_rows) - jnp.maximum(r + lo_off, 0)
                cntv = jnp.full((GRID_W, POOL_GC), cnt, jnp.int32).astype(F32)
                t0 = pl.multiple_of(r * GRID_W, GRID_W)
                v = u_ref[0, pl.ds(t0, GRID_W), :].astype(F32)
                o_ref[0, pl.ds(t0, GRID_W), :] = (s / cntv - v).astype(o_ref.dtype)
                return carry

            lax.fori_loop(0, grid_rows, body, 0)


def _pool_diff(p, grid_rows):
    b, l, _ = p.shape
    mats = []
    for w in POOL_WINDOWS:
        if grid_rows is None:
            mats.append(_box_matrix(l, w))
        else:
            mats.append(np.kron(np.eye(POOL_TB // GRID_W), _box_matrix(GRID_W, w)))
    mh, ml = _np_split2(np.stack(mats))
    tb = mats[0].shape[0]
    scratch = []
    if grid_rows is not None:
        scratch = [pltpu.VMEM((l + 2 * POOL_PAD * GRID_W, POOL_GC), F32)]
    cblk = P_POOL // POOL_GC
    return pl.pallas_call(
        functools.partial(_pool_kernel, grid_rows=grid_rows),
        grid=(b, POOL_GROUPS),
        in_specs=[pl.BlockSpec((1, l, POOL_GC), lambda bi, g: (bi, 0, cblk + g)),
                  pl.BlockSpec((1, tb, tb), lambda bi, g: (g, 0, 0)),
                  pl.BlockSpec((1, tb, tb), lambda bi, g: (g, 0, 0))],
        out_specs=pl.BlockSpec((1, l, POOL_GC), lambda bi, g: (bi, 0, g)),
        out_shape=jax.ShapeDtypeStruct((b, l, POOL_W), BF16),
        scratch_shapes=scratch,
        compiler_params=_cparams(("parallel", "parallel")),
        name="pool_diff",
    )(p, mh, ml)


def _chdft_kernel(u_ref, w_ref, o_ref):
    o_ref[0, 0] = _dot(u_ref[0], w_ref[...]).astype(o_ref.dtype)


def _chdft(p, tm):
    b, l, _ = p.shape
    tm = min(tm, l)
    k = np.arange(FOUR_GC)
    ang = -2.0 * np.pi * ((k[:, None] * k[None, :]) % FOUR_GC) / FOUR_GC
    w = jnp.asarray(np.concatenate([np.cos(ang), np.sin(ang)], axis=1).astype(np.float32)).astype(BF16)
    cblk = P_FOUR // FOUR_GC
    return pl.pallas_call(
        _chdft_kernel,
        grid=(b, FOUR_GROUPS, l // tm),
        in_specs=[pl.BlockSpec((1, tm, FOUR_GC), lambda bi, g, i: (bi, i, cblk + g)),
                  _resident((FOUR_GC, 2 * FOUR_GC), lambda bi, g, i: (0, 0))],
        out_specs=pl.BlockSpec((1, 1, tm, 2 * FOUR_GC), lambda bi, g, i: (bi, g, i, 0)),
        out_shape=jax.ShapeDtypeStruct((b, FOUR_GROUPS, l, 2 * FOUR_GC), BF16),
        compiler_params=_cparams(("parallel", "parallel", "parallel")),
        name="chdft",
    )(p, w)


def _seq_dft_tables(l):
    r = int(round(math.sqrt(l)))
    if r * r != l:
        k = np.arange(l)
        ang = 2.0 * np.pi * ((k[:, None] * k[None, :]) % l) / l
        return (jnp.asarray(np.cos(ang).astype(np.float32)).astype(BF16),
                jnp.asarray(np.sin(ang).astype(np.float32)).astype(BF16))
    col = np.arange(l)
    hi = 2.0 * np.pi * ((np.arange(r)[:, None] * r * col[None, :]) % l) / l
    lo = 2.0 * np.pi * ((np.arange(r)[:, None] * col[None, :]) % l) / l
    ch, sh = [jnp.asarray(f(hi).astype(np.float32))[:, None, :] for f in (np.cos, np.sin)]
    cl, sl = [jnp.asarray(f(lo).astype(np.float32))[None, :, :] for f in (np.cos, np.sin)]
    cos = (ch * cl - sh * sl).reshape(l, l).astype(BF16)
    sin = (sh * cl + ch * sl).reshape(l, l).astype(BF16)
    return cos, sin


def _seqdft_kernel(c_ref, s_ref, z_ref, o_ref, *, scale):
    c = FOUR_GC
    z = z_ref[0, 0]
    y = _dot(c_ref[...], z[:, :c]) + _dot(s_ref[...], z[:, c:])
    o_ref[0] = (y * scale).astype(o_ref.dtype)


def _fourier(p, tm):
    b, l, _ = p.shape
    tm = min(tm, l)
    z = _chdft(p, 1024)
    cos, sin = _seq_dft_tables(l)
    scale = 1.0 / math.sqrt(l * FOUR_GC)
    return pl.pallas_call(
        functools.partial(_seqdft_kernel, scale=scale),
        grid=(l // tm, b, FOUR_GROUPS),
        in_specs=[pl.BlockSpec((tm, l), lambda i, bi, g: (i, 0)),
                  pl.BlockSpec((tm, l), lambda i, bi, g: (i, 0)),
                  pl.BlockSpec((1, 1, l, 2 * FOUR_GC), lambda i, bi, g: (bi, g, 0, 0))],
        out_specs=pl.BlockSpec((1, tm, FOUR_GC), lambda i, bi, g: (bi, i, g)),
        out_shape=jax.ShapeDtypeStruct((b, l, FOUR_W), BF16),
        compiler_params=_cparams(("parallel", "parallel", "parallel")),
        name="seqdft",
    )(cos, sin, z)


CONV_CB = 512


def _ssd_kernel(*refs, fwd, nc, epilogue):
    if fwd:
        xc_ref, dt_ref, dtb_ref, alog_ref, e_ref, h0_ref = refs[:6]
        rest = refs[6:]
        if epilogue:
            z_ref, yb_ref, dsk_ref, nw_ref, y_ref, hout_ref, st_ref = rest
        else:
            y_ref, hout_ref, st_ref = rest
    else:
        (xm_ref, xp_ref, xn_ref, dt_ref, cw_ref, cb_ref, dtb_ref, alog_ref, e_ref, h0_ref,
         y_ref, xc_ref, hout_ref, st_ref, ext_ref) = refs
    c = pl.program_id(1)
    cc = c if fwd else nc - 1 - c
    T = CHUNK

    @pl.when(c == 0)
    def _():
        st_ref[...] = h0_ref[0]

    if fwd:
        xc = lambda cs: xc_ref[0, :, cs].astype(F32)
    else:
        keep_p = jnp.where(cc > 0, 1.0, 0.0)
        keep_n = jnp.where(cc < nc - 1, 1.0, 0.0)
        ext_ref[0:HALO, :] = xp_ref[0].astype(F32) * keep_p
        ext_ref[HALO:HALO + T, :] = xm_ref[0].astype(F32)
        ext_ref[HALO + T:HALO + T + HALO, :] = xn_ref[0].astype(F32) * keep_n
        for j in range(XBC // CONV_CB):
            cs = slice(j * CONV_CB, (j + 1) * CONV_CB)
            acc = jnp.broadcast_to(cb_ref[:, cs], (T, CONV_CB))
            for k in range(CONV_W):
                r0 = HALO - CONV_W // 2 + k
                acc = acc + ext_ref[r0:r0 + T, cs] * cw_ref[k:k + 1, cs]
            xc_ref[0, :, cs] = _silu(acc).astype(xc_ref.dtype)
        xc = lambda cs: xc_ref[0, :, cs].astype(F32)

    xdt = dt_ref[0] + dtb_ref[...]
    dt = jnp.maximum(xdt, 0.0) + jnp.log1p(jnp.exp(-jnp.abs(xdt)))
    a = -jnp.exp(alog_ref[...])
    da = dt * a
    row = lax.broadcasted_iota(jnp.int32, (T, T), 0)
    col = lax.broadcasted_iota(jnp.int32, (T, T), 1)
    causal = (col <= row) if fwd else (col >= row)
    tri = jnp.where(causal, 1.0, 0.0).astype(BF16)
    d1 = da.astype(BF16)
    r1 = da - d1.astype(F32)
    d2 = r1.astype(BF16)
    d3 = (r1 - d2.astype(F32)).astype(BF16)
    acs = _dot(tri, d1) + _dot(tri, d2) + _dot(tri, d3)
    tot = acs[T - 1:T, :] if fwd else acs[0:1, :]
    eacs = jnp.exp(acs)
    dte = jnp.exp(tot - acs)
    etot = jnp.broadcast_to(jnp.exp(tot), (8, T))

    e = e_ref[...]

    def expand(v):
        hi, lo = _split2(v)
        return _dot(hi, e) + _dot(lo, e)

    dtx = expand(dt)
    eacs_x = expand(eacs)
    w2 = expand(dt * dte)
    etot_x = expand(etot)[0:1, :]
    acs_t = acs.T
    lane = lax.broadcasted_iota(jnp.int32, (T, 2 * HEAD_DIM), 1)
    hbase = 0 if fwd else SSD_HEADS

    for g in range(SSD_GROUPS):
        gs = slice(g * GROUP_W, (g + 1) * GROUP_W)
        bg = xc(slice(D_SSD + g * D_STATE, D_SSD + (g + 1) * D_STATE))
        cg = xc(slice(D_SSD + (SSD_GROUPS + g) * D_STATE, D_SSD + (SSD_GROUPS + g + 1) * D_STATE))
        bb = bg.astype(BF16)
        cbf = cg.astype(BF16)
        cb = lax.dot_general(cbf, bb, (((1,), (1,)), ((), ())), preferred_element_type=F32)
        st = st_ref[g]
        yoff = _dot(cbf, st.astype(BF16)) * eacs_x[:, gs]
        xg = xc(gs)
        xdb = (xg * dtx[:, gs]).astype(BF16)
        ys = []
        for jp in range(HEADS_PER_GROUP // 2):
            ms = []
            for jj in range(2):
                hc = hbase + g * HEADS_PER_GROUP + 2 * jp + jj
                diff = acs[:, hc:hc + 1] - acs_t[hc:hc + 1, :]
                lm = jnp.exp(jnp.where(causal, diff, -1e30))
                ms.append((cb * lm).astype(BF16))
            r = _dot(jnp.concatenate(ms, axis=0), xdb[:, jp * 2 * HEAD_DIM:(jp + 1) * 2 * HEAD_DIM])
            ys.append(jnp.where(lane < HEAD_DIM, r[:T], r[T:]))
        yg = jnp.concatenate(ys, axis=1) + yoff

        xds = (xg * w2[:, gs]).astype(BF16)
        st_ref[g] = st * etot_x[:, gs] + _dot(bg.T.astype(BF16), xds)

        if epilogue:
            yt = yg + yb_ref[0, :, gs] + dsk_ref[:, gs] * xg
            v = yt * _silu(z_ref[0, :, gs].astype(F32))
            ms_ = jnp.mean(v * v, axis=-1, keepdims=True)
            y_ref[0, :, gs] = ((v * lax.rsqrt(ms_ + EPS)) * nw_ref[:, gs]).astype(y_ref.dtype)
        else:
            y_ref[0, :, gs] = yg

    @pl.when(c == nc - 1)
    def _():
        hout_ref[0] = st_ref[...]


_STATE_BLOCK = (1, SSD_GROUPS, D_STATE, GROUP_W)


def _ssd_bwd(p, pdt, lw, h0):
    b, l, _ = p.shape
    nc = l // CHUNK
    nh = l // HALO
    per = CHUNK // HALO
    cidx = lambda c: nc - 1 - c
    chunk = lambda bi, c: (bi, cidx(c), 0)
    const2 = lambda bi, c: (0, 0)
    state = lambda bi, c: (bi, 0, 0, 0)
    return pl.pallas_call(
        functools.partial(_ssd_kernel, fwd=False, nc=nc, epilogue=False),
        grid=(b, nc),
        in_specs=[pl.BlockSpec((1, CHUNK, XBC), chunk),
                  pl.BlockSpec((1, HALO, XBC), lambda bi, c: (bi, jnp.maximum(cidx(c) * per - 1, 0), 0)),
                  pl.BlockSpec((1, HALO, XBC), lambda bi, c: (bi, jnp.minimum((cidx(c) + 1) * per, nh - 1), 0)),
                  pl.BlockSpec((1, CHUNK, DT_PAD), chunk),
                  _resident((8, XBC), const2),
                  _resident((1, XBC), const2),
                  _resident((1, DT_PAD), const2),
                  _resident((1, DT_PAD), const2),
                  _resident((DT_PAD, D_SSD), const2),
                  pl.BlockSpec(_STATE_BLOCK, state)],
        out_specs=[pl.BlockSpec((1, CHUNK, D_SSD), chunk),
                   pl.BlockSpec((1, CHUNK, XBC), chunk),
                   pl.BlockSpec(_STATE_BLOCK, state)],
        out_shape=[jax.ShapeDtypeStruct((b, l, D_SSD), F32),
                   jax.ShapeDtypeStruct((b, l, XBC), BF16),
                   jax.ShapeDtypeStruct((b,) + _STATE_BLOCK[1:], F32)],
        scratch_shapes=[pltpu.VMEM(_STATE_BLOCK[1:], F32),
                        pltpu.VMEM((CHUNK + 2 * HALO, XBC), F32)],
        compiler_params=_cparams(("parallel", "arbitrary")),
        name="ssd_bwd",
    )(p, p, p, pdt, lw["conv_w"], lw["conv_b"], lw["dt_bias"], lw["a_log"], lw["expand_b"], h0)


def _ssd_fwd(xc, pdt, lw, h0, p=None, yb=None):
    b, l, _ = xc.shape
    nc = l // CHUNK
    chunk = lambda bi, c: (bi, c, 0)
    const2 = lambda bi, c: (0, 0)
    state = lambda bi, c: (bi, 0, 0, 0)
    epilogue = p is not None
    in_specs = [pl.BlockSpec((1, CHUNK, XBC), chunk),
                pl.BlockSpec((1, CHUNK, DT_PAD), chunk),
                _resident((1, DT_PAD), const2),
                _resident((1, DT_PAD), const2),
                _resident((DT_PAD, D_SSD), const2),
                pl.BlockSpec(_STATE_BLOCK, state)]
    args = [xc, pdt, lw["dt_bias"], lw["a_log"], lw["expand_f"], h0]
    if epilogue:
        z_blk = P_Z // D_SSD
        in_specs += [pl.BlockSpec((1, CHUNK, D_SSD), lambda bi, c: (bi, c, z_blk)),
                     pl.BlockSpec((1, CHUNK, D_SSD), chunk),
                     _resident((1, D_SSD), const2),
                     _resident((1, D_SSD), const2)]
        args += [p, yb, lw["d_skip"], lw["ssd_norm_w"]]
    return pl.pallas_call(
        functools.partial(_ssd_kernel, fwd=True, nc=nc, epilogue=epilogue),
        grid=(b, nc),
        in_specs=in_specs,
        out_specs=[pl.BlockSpec((1, CHUNK, D_SSD), chunk),
                   pl.BlockSpec(_STATE_BLOCK, state)],
        out_shape=[jax.ShapeDtypeStruct((b, l, D_SSD), BF16 if epilogue else F32),
                   jax.ShapeDtypeStruct((b,) + _STATE_BLOCK[1:], F32)],
        scratch_shapes=[pltpu.VMEM(_STATE_BLOCK[1:], F32)],
        compiler_params=_cparams(("parallel", "arbitrary")),
        name="ssd_fwd",
    )(*args)


def _ssd(p, pdt, lw, h0f, h0b):
    yb, xc, hb = _ssd_bwd(p, pdt, lw, h0b)
    y, hf = _ssd_fwd(xc, pdt, lw, h0f, p, yb)
    return y, hf, hb


def _expand_matrix(offset):
    e = np.zeros((DT_PAD, D_SSD), np.float32)
    for h in range(SSD_HEADS):
        e[offset + h, h * HEAD_DIM:(h + 1) * HEAD_DIM] = 1.0
    return jnp.asarray(e).astype(BF16)


def _layer_weights(l, w_in, conv_w, conv_b, a_log, dt_bias, d_skip, ssd_norm_w, w_ssd_out, w_pool,
                   pool_scale, w_fourier, w_out, w_ffn_gate, w_ffn_up, w_ffn_down):
    wi = w_in[l]
    w_main = jnp.concatenate([wi[:, :XBC], wi[:, POOL_OFF:FOUR_OFF], wi[:, Z_OFF:POOL_OFF],
                              wi[:, GATE_OFF:], wi[:, FOUR_OFF:GATE_OFF]], axis=1).astype(BF16)
    w_dt = jnp.pad(wi[:, XBC:SSD_IN], ((0, 0), (0, DT_PAD - 2 * SSD_HEADS))).astype(BF16)
    pad_h = (0, DT_PAD - 2 * SSD_HEADS)
    return {
        "w_main": w_main,
        "w_dt": w_dt,
        "conv_w": jnp.pad(conv_w[l], ((0, 8 - CONV_W), (0, 0))),
        "conv_b": conv_b[l].reshape(1, XBC),
        "dt_bias": jnp.pad(dt_bias[l].reshape(-1), pad_h).reshape(1, DT_PAD),
        "a_log": jnp.pad(a_log[l].reshape(-1), pad_h).reshape(1, DT_PAD),
        "d_skip": jnp.repeat(d_skip[l], HEAD_DIM).reshape(1, D_SSD),
        "ssd_norm_w": ssd_norm_w[l].reshape(1, D_SSD),
        "expand_f": _expand_matrix(0),
        "expand_b": _expand_matrix(SSD_HEADS),
        "w_ssd_out": w_ssd_out[l].astype(BF16),
        "w_pool": w_pool[l].astype(BF16),
        "pool_scale": pool_scale[l].reshape(1, D_MODEL),
        "w_fourier": w_fourier[l].astype(BF16),
        "w_out": w_out[l].astype(BF16),
        "w_ffn_gate": w_ffn_gate[l].astype(BF16),
        "w_ffn_up": w_ffn_up[l].astype(BF16),
        "w_ffn_down": w_ffn_down[l].astype(BF16),
    }


def _mixer(x, nw, sc, sh, g_m, lw, grid_rows, h0f, h0b):
    p, pdt = _inproj(x, nw, sc, sh, lw["w_main"], lw["w_dt"], P_TOT, 1024, 1024)
    y_n, hf, hb = _ssd(p, pdt, lw, h0f, h0b)
    d = _pool_diff(p, grid_rows)
    f = _fourier(p, 1024)
    x = _merge_out(d, f, y_n, p, x, g_m, lw, 256)
    return x, hf, hb


def kernel(x, c, ctx, c_ctx, w_ada, b_ada, norm_mix_w, norm_ffn_w, w_in, conv_w, conv_b, a_log, dt_bias,
           d_skip, ssd_norm_w, w_ssd_out, w_pool, pool_scale, w_fourier, w_out, w_ffn_gate, w_ffn_up,
           w_ffn_down, final_norm_w):
    b, seq, d = x.shape
    rows = seq // GRID_W
    cc = jnp.concatenate([c, c_ctx[None, :], jnp.zeros((8 - b - 1, d), F32)], axis=0)
    mod = _adaln(cc, w_ada, b_ada)
    h0 = jnp.zeros((b,) + _STATE_BLOCK[1:], F32)
    for l in range(DEPTH):
        last = l == DEPTH - 1
        lw = _layer_weights(l, w_in, conv_w, conv_b, a_log, dt_bias, d_skip, ssd_norm_w, w_ssd_out, w_pool,
                            pool_scale, w_fourier, w_out, w_ffn_gate, w_ffn_up, w_ffn_down)
        sh_m, sc_m, g_m, sh_f, sc_f, g_f = [mod[l, :b, i * d:(i + 1) * d].reshape(b, 1, d) for i in range(6)]
        csh_m, csc_m, cg_m, csh_f, csc_f, cg_f = [
            jnp.broadcast_to(mod[l, b, i * d:(i + 1) * d].reshape(1, 1, d), (b, 1, d)) for i in range(6)]

        if last:
            pc, pdtc = _inproj(ctx, norm_mix_w[l], csc_m, csh_m, lw["w_main"], lw["w_dt"], XBC, 256, 1024)
            _, xcc, hb = _ssd_bwd(pc, pdtc, lw, h0)
            _, hf = _ssd_fwd(xcc, pdtc, lw, h0)
        else:
            ctx, hf, hb = _mixer(ctx, norm_mix_w[l], csc_m, csh_m, cg_m, lw, None, h0, h0)
            ctx = _ffn(ctx, norm_ffn_w[l], csc_f, csh_f, lw["w_ffn_gate"], lw["w_ffn_up"], lw["w_ffn_down"],
                       cg_f, final_norm_w, False, 256, 512)

        x, _, _ = _mixer(x, norm_mix_w[l], sc_m, sh_m, g_m, lw, rows, hf, hb)
        x = _ffn(x, norm_ffn_w[l], sc_f, sh_f, lw["w_ffn_gate"], lw["w_ffn_up"], lw["w_ffn_down"],
                 g_f, final_norm_w, last, 512, 512)
    return x
```

```python
import functools
import math

import numpy as np
import jax
import jax.numpy as jnp
from jax import lax
from jax.experimental import pallas as pl
from jax.experimental.pallas import tpu as pltpu

F32 = jnp.float32
BF16 = jnp.bfloat16

D_MODEL = 2048
DEPTH = 2
GRID_W = 64
EPS = 1e-6

POOL_GROUPS = 4
POOL_WINDOWS = (2, 4, 8, 16)
POOL_W = D_MODEL // 2
POOL_GC = POOL_W // POOL_GROUPS
POOL_OUT = D_MODEL // POOL_GROUPS

FOUR_GROUPS = 4
FOUR_W = D_MODEL // 2
FOUR_GC = FOUR_W // FOUR_GROUPS

D_SSD = D_MODEL
HEAD_DIM = 64
SSD_HEADS = D_SSD // HEAD_DIM
SSD_GROUPS = 4
HEADS_PER_GROUP = SSD_HEADS // SSD_GROUPS
GROUP_W = HEADS_PER_GROUP * HEAD_DIM
D_STATE = 128
CONV_W = 5
CHUNK = 128

D_FF = ((8 * D_MODEL // 3 + 255) // 256) * 256

XBC = D_SSD + 2 * SSD_GROUPS * D_STATE
SSD_IN = XBC + 2 * SSD_HEADS
Z_OFF = SSD_IN
POOL_OFF = Z_OFF + D_SSD
FOUR_OFF = POOL_OFF + POOL_W
GATE_OFF = FOUR_OFF + FOUR_W
N_IN = GATE_OFF + 3 * D_MODEL

P_XBC = 0
P_POOL = XBC
P_Z = P_POOL + POOL_W
P_GATE = P_Z + D_SSD
P_FOUR = P_GATE + 3 * D_MODEL
P_TOT = P_FOUR + FOUR_W
DT_PAD = 128

V7X_VMEM_LIMIT = 56 * 1024 * 1024
HALO = 16


def _cparams(sem):
    return pltpu.CompilerParams(dimension_semantics=sem, vmem_limit_bytes=V7X_VMEM_LIMIT)


def _resident(shape, index_map):
    return pl.BlockSpec(shape, index_map, pipeline_mode=pl.Buffered(1))


def _split2(v):
    hi = v.astype(BF16)
    lo = (v - hi.astype(F32)).astype(BF16)
    return hi, lo


def _dot(a, b):
    return jnp.dot(a, b, preferred_element_type=F32)


def _sigmoid(v):
    return 0.5 * jnp.tanh(0.5 * v) + 0.5


def _silu(v):
    return v * _sigmoid(v)


def _np_split2(m):
    m = jnp.asarray(np.asarray(m, np.float32))
    hi = m.astype(BF16)
    lo = (m - hi.astype(F32)).astype(BF16)
    return hi, lo


CAST_ROWS = 256
WIN_ROWS = 128
WIN_CHUNK = 1024
LANES = 128


def _cast_kernel(w_ref, o_ref):
    o_ref[...] = w_ref[...].astype(BF16)


def _to_bf16(w):
    depth, k, n = w.shape
    return pl.pallas_call(
        _cast_kernel,
        grid=(depth, k // CAST_ROWS),
        in_specs=[pl.BlockSpec((1, CAST_ROWS, n), lambda l, i: (l, i, 0))],
        out_specs=pl.BlockSpec((1, CAST_ROWS, n), lambda l, i: (l, i, 0)),
        out_shape=jax.ShapeDtypeStruct((depth, k, n), BF16),
        compiler_params=_cparams(("parallel", "parallel")),
        name="cast_bf16",
    )(w)


_WIN_SEGMENTS = ((P_XBC, 0, XBC), (P_POOL, POOL_OFF, POOL_W), (P_Z, Z_OFF, D_SSD),
                 (P_GATE, GATE_OFF, 3 * D_MODEL), (P_FOUR, FOUR_OFF, FOUR_W))


def _winprep_kernel(w_ref, o_ref, dt_ref):
    def src_cols(s0, n):
        base = (s0 // LANES) * LANES
        if base == s0:
            return w_ref[0, :, s0:s0 + n]
        end = min(base + n + LANES, N_IN)
        return w_ref[0, :, base:end][:, s0 - base:s0 - base + n]

    for dst, src, width in _WIN_SEGMENTS:
        for c0 in range(0, width, WIN_CHUNK):
            o_ref[0, :, dst + c0:dst + c0 + WIN_CHUNK] = src_cols(src + c0, WIN_CHUNK).astype(BF16)
    dtw = src_cols(XBC, 2 * SSD_HEADS)
    dt_ref[0] = jnp.concatenate([dtw, jnp.zeros((WIN_ROWS, DT_PAD - 2 * SSD_HEADS), F32)], axis=1).astype(BF16)


def _winprep(w_in):
    depth, d, n = w_in.shape
    return pl.pallas_call(
        _winprep_kernel,
        grid=(depth, d // WIN_ROWS),
        in_specs=[pl.BlockSpec((1, WIN_ROWS, n), lambda l, i: (l, i, 0))],
        out_specs=[pl.BlockSpec((1, WIN_ROWS, P_TOT), lambda l, i: (l, i, 0)),
                   pl.BlockSpec((1, WIN_ROWS, DT_PAD), lambda l, i: (l, i, 0))],
        out_shape=[jax.ShapeDtypeStruct((depth, d, P_TOT), BF16),
                   jax.ShapeDtypeStruct((depth, d, DT_PAD), BF16)],
        compiler_params=_cparams(("parallel", "parallel")),
        name="winprep",
    )(w_in)


def _adaln_kernel(c_ref, w_ref, b_ref, o_ref):
    s = _silu(c_ref[...])
    w = w_ref[0]
    sh, sl = _split2(s)
    wh, wl = _split2(w)
    o_ref[0] = _dot(sh, wh) + _dot(sl, wh) + _dot(sh, wl) + b_ref[0]


def _adaln(cc, w_ada, b_ada, tn=1024):
    depth, d, n = w_ada.shape
    return pl.pallas_call(
        _adaln_kernel,
        grid=(depth, n // tn),
        in_specs=[pl.BlockSpec((8, d), lambda l, j: (0, 0)),
                  pl.BlockSpec((1, d, tn), lambda l, j: (l, 0, j)),
                  pl.BlockSpec((1, 1, tn), lambda l, j: (l, 0, j))],
        out_specs=pl.BlockSpec((1, 8, tn), lambda l, j: (l, 0, j)),
        out_shape=jax.ShapeDtypeStruct((depth, 8, n), F32),
        compiler_params=_cparams(("parallel", "parallel")),
        name="adaln",
    )(cc, w_ada, b_ada.reshape(depth, 1, n))


def _norm_mod(x, nw, sc, sh):
    ms = jnp.mean(x * x, axis=-1, keepdims=True)
    return (x * lax.rsqrt(ms + EPS)) * nw * (1.0 + sc) + sh


def _inproj_kernel(x_ref, nw_ref, sc_ref, sh_ref, w_ref, wdt_ref, o_ref, dt_ref, h_ref):
    @pl.when(pl.program_id(2) == 0)
    def _():
        h = _norm_mod(x_ref[0], nw_ref[...], sc_ref[0], sh_ref[0]).astype(BF16)
        h_ref[...] = h
        dt_ref[0] = _dot(h, wdt_ref[0])

    o_ref[0] = _dot(h_ref[...], w_ref[0]).astype(o_ref.dtype)


def _inproj(x, nw, sc, sh, w, wdt, layer, n_out, tm, tn):
    b, l, d = x.shape
    tm = min(tm, l)
    return pl.pallas_call(
        _inproj_kernel,
        grid=(b, l // tm, n_out // tn),
        in_specs=[pl.BlockSpec((1, tm, d), lambda bi, i, j: (bi, i, 0)),
                  pl.BlockSpec((1, d), lambda bi, i, j: (0, 0)),
                  pl.BlockSpec((1, 1, d), lambda bi, i, j: (bi, 0, 0)),
                  pl.BlockSpec((1, 1, d), lambda bi, i, j: (bi, 0, 0)),
                  pl.BlockSpec((1, d, tn), lambda bi, i, j: (layer, 0, j)),
                  _resident((1, d, DT_PAD), lambda bi, i, j: (layer, 0, 0))],
        out_specs=[pl.BlockSpec((1, tm, tn), lambda bi, i, j: (bi, i, j)),
                   pl.BlockSpec((1, tm, DT_PAD), lambda bi, i, j: (bi, i, 0))],
        out_shape=[jax.ShapeDtypeStruct((b, l, n_out), BF16),
                   jax.ShapeDtypeStruct((b, l, DT_PAD), F32)],
        scratch_shapes=[pltpu.VMEM((tm, d), BF16)],
        compiler_params=_cparams(("parallel", "parallel", "arbitrary")),
        name="inproj",
    )(x, nw.reshape(1, d), sc, sh, w, wdt)


def _merge_kernel(d_ref, f_ref, yn_ref, gp_ref, gf_ref, gs_ref, x_ref, gm_ref,
                  wp_ref, ps_ref, wf_ref, ws_ref, wo_ref, o_ref):
    d = d_ref[0]
    y_pool = jnp.concatenate(
        [_dot(d[:, g * POOL_GC:(g + 1) * POOL_GC], wp_ref[0, g * POOL_GC:(g + 1) * POOL_GC, :])
         for g in range(POOL_GROUPS)], axis=1)
    m = _sigmoid(gp_ref[0].astype(F32)) * (y_pool * ps_ref[0])
    m = m + _sigmoid(gf_ref[0].astype(F32)) * _dot(f_ref[0], wf_ref[0])
    m = m + _sigmoid(gs_ref[0].astype(F32)) * _dot(yn_ref[0], ws_ref[0])
    o_ref[0] = x_ref[0] + gm_ref[0] * _dot(m.astype(BF16), wo_ref[0])


def _merge_out(d, f, yn, p, x, g_m, wts, layer, tm):
    b, l, dm = x.shape
    tm = min(tm, l)
    gblk = P_GATE // dm
    row = lambda bi, i: (bi, i, 0)
    lyr = lambda bi, i: (layer, 0, 0)
    return pl.pallas_call(
        _merge_kernel,
        grid=(b, l // tm),
        in_specs=[pl.BlockSpec((1, tm, POOL_W), row),
                  pl.BlockSpec((1, tm, FOUR_W), row),
                  pl.BlockSpec((1, tm, D_SSD), row),
                  pl.BlockSpec((1, tm, dm), lambda bi, i: (bi, i, gblk)),
                  pl.BlockSpec((1, tm, dm), lambda bi, i: (bi, i, gblk + 1)),
                  pl.BlockSpec((1, tm, dm), lambda bi, i: (bi, i, gblk + 2)),
                  pl.BlockSpec((1, tm, dm), row),
                  pl.BlockSpec((1, 1, dm), lambda bi, i: (bi, 0, 0)),
                  _resident((1, POOL_W, POOL_OUT), lyr),
                  _resident((1, 1, dm), lyr),
                  _resident((1, FOUR_W, dm), lyr),
                  _resident((1, D_SSD, dm), lyr),
                  _resident((1, dm, dm), lyr)],
        out_specs=pl.BlockSpec((1, tm, dm), row),
        out_shape=jax.ShapeDtypeStruct((b, l, dm), F32),
        compiler_params=_cparams(("parallel", "parallel")),
        name="merge_out",
    )(d, f, yn, p, p, p, x, g_m, wts["w_pool"], wts["pool_scale"], wts["w_fourier"], wts["w_ssd_out"],
      wts["w_out"])


def _ffn_kernel(x_ref, nw_ref, sc_ref, sh_ref, wg_ref, wu_ref, wd_ref, gate_ref, fnw_ref,
                o_ref, h_ref, acc_ref, *, final_norm):
    j = pl.program_id(2)

    @pl.when(j == 0)
    def _():
        h_ref[...] = _norm_mod(x_ref[0], nw_ref[...], sc_ref[0], sh_ref[0]).astype(BF16)
        acc_ref[...] = jnp.zeros_like(acc_ref)

    h = h_ref[...]
    a = _silu(_dot(h, wg_ref[0])) * _dot(h, wu_ref[0])
    acc_ref[...] += _dot(a.astype(BF16), wd_ref[0])

    @pl.when(j == pl.num_programs(2) - 1)
    def _():
        o = x_ref[0] + gate_ref[0] * acc_ref[...]
        if final_norm:
            ms = jnp.mean(o * o, axis=-1, keepdims=True)
            o = (o * lax.rsqrt(ms + EPS)) * fnw_ref[...]
        o_ref[0] = o


def _ffn(x, nw, sc, sh, wg, wu, wd, layer, gate, fnw, final_norm, tm, tf):
    b, l, d = x.shape
    ff = wg.shape[2]
    tm = min(tm, l)
    vec = lambda bi, i, j: (bi, 0, 0)
    return pl.pallas_call(
        functools.partial(_ffn_kernel, final_norm=final_norm),
        grid=(b, l // tm, ff // tf),
        in_specs=[pl.BlockSpec((1, tm, d), lambda bi, i, j: (bi, i, 0)),
                  pl.BlockSpec((1, d), lambda bi, i, j: (0, 0)),
                  pl.BlockSpec((1, 1, d), vec),
                  pl.BlockSpec((1, 1, d), vec),
                  pl.BlockSpec((1, d, tf), lambda bi, i, j: (layer, 0, j)),
                  pl.BlockSpec((1, d, tf), lambda bi, i, j: (layer, 0, j)),
                  pl.BlockSpec((1, tf, d), lambda bi, i, j: (layer, j, 0)),
                  pl.BlockSpec((1, 1, d), vec),
                  pl.BlockSpec((1, d), lambda bi, i, j: (0, 0))],
        out_specs=pl.BlockSpec((1, tm, d), lambda bi, i, j: (bi, i, 0)),
        out_shape=jax.ShapeDtypeStruct((b, l, d), F32),
        scratch_shapes=[pltpu.VMEM((tm, d), BF16), pltpu.VMEM((tm, d), F32)],
        compiler_params=_cparams(("parallel", "parallel", "arbitrary")),
        name="ffn",
    )(x, nw.reshape(1, d), sc, sh, wg, wu, wd, gate, fnw.reshape(1, d))


def _box_matrix(n, w):
    idx = np.arange(n)
    lo = np.clip(idx - w // 2, 0, n)
    hi = np.clip(idx + (w - w // 2), 0, n)
    m = ((idx[None, :] >= lo[:, None]) & (idx[None, :] < hi[:, None])).astype(np.float64)
    return m / (hi - lo)[:, None]


POOL_TB = 256
POOL_PAD = 16


def _pool_kernel(u_ref, mh_ref, ml_ref, o_ref, *scratch, grid_rows):
    g = pl.program_id(1)
    l = u_ref.shape[1]
    mh = mh_ref[0]
    ml = ml_ref[0]
    if grid_rows is None:
        v = u_ref[0]
        o_ref[0] = (_dot(mh, v) + _dot(ml, v) - v.astype(F32)).astype(o_ref.dtype)
        return

    cp_ref, = scratch
    pad = POOL_PAD * GRID_W
    cp_ref[0:pad, :] = jnp.zeros((pad, POOL_GC), F32)
    cp_ref[pad + l:pad + l + pad, :] = jnp.zeros((pad, POOL_GC), F32)
    for i in range(l // POOL_TB):
        v = u_ref[0, i * POOL_TB:(i + 1) * POOL_TB, :]
        cp_ref[pad + i * POOL_TB:pad + (i + 1) * POOL_TB, :] = _dot(mh, v) + _dot(ml, v)

    for gi, w in enumerate(POOL_WINDOWS):
        lo_off, hi_off = -(w // 2), w - w // 2

        @pl.when(g == gi)
        def _(lo_off=lo_off, hi_off=hi_off):
            def body(r, carry):
                s = jnp.zeros((GRID_W, POOL_GC), F32)
                for o in range(lo_off, hi_off):
                    start = pl.multiple_of(pad + (r + o) * GRID_W, GRID_W)
                    s = s + cp_ref[pl.ds(start, GRID_W), :]
                cnt = jnp.minimum(r + hi_off, grid_rows) - jnp.maximum(r + lo_off, 0)
                cntv = jnp.full((GRID_W, POOL_GC), cnt, jnp.int32).astype(F32)
                t0 = pl.multiple_of(r * GRID_W, GRID_W)
                v = u_ref[0, pl.ds(t0, GRID_W), :].astype(F32)
                o_ref[0, pl.ds(t0, GRID_W), :] = (s / cntv - v).astype(o_ref.dtype)
                return carry

            lax.fori_loop(0, grid_rows, body, 0)


def _pool_diff(p, grid_rows):
    b, l, _ = p.shape
    mats = []
    for w in POOL_WINDOWS:
        if grid_rows is None:
            mats.append(_box_matrix(l, w))
        else:
            mats.append(np.kron(np.eye(POOL_TB // GRID_W), _box_matrix(GRID_W, w)))
    mh, ml = _np_split2(np.stack(mats))
    tb = mats[0].shape[0]
    scratch = []
    if grid_rows is not None:
        scratch = [pltpu.VMEM((l + 2 * POOL_PAD * GRID_W, POOL_GC), F32)]
    cblk = P_POOL // POOL_GC
    return pl.pallas_call(
        functools.partial(_pool_kernel, grid_rows=grid_rows),
        grid=(b, POOL_GROUPS),
        in_specs=[pl.BlockSpec((1, l, POOL_GC), lambda bi, g: (bi, 0, cblk + g)),
                  pl.BlockSpec((1, tb, tb), lambda bi, g: (g, 0, 0)),
                  pl.BlockSpec((1, tb, tb), lambda bi, g: (g, 0, 0))],
        out_specs=pl.BlockSpec((1, l, POOL_GC), lambda bi, g: (bi, 0, g)),
        out_shape=jax.ShapeDtypeStruct((b, l, POOL_W), BF16),
        scratch_shapes=scratch,
        compiler_params=_cparams(("parallel", "parallel")),
        name="pool_diff",
    )(p, mh, ml)


def _chdft_kernel(u_ref, w_ref, o_ref):
    o_ref[0, 0] = _dot(u_ref[0], w_ref[...]).astype(o_ref.dtype)


def _chdft(p, tm):
    b, l, _ = p.shape
    tm = min(tm, l)
    k = np.arange(FOUR_GC)
    ang = -2.0 * np.pi * ((k[:, None] * k[None, :]) % FOUR_GC) / FOUR_GC
    w = jnp.asarray(np.concatenate([np.cos(ang), np.sin(ang)], axis=1).astype(np.float32)).astype(BF16)
    cblk = P_FOUR // FOUR_GC
    return pl.pallas_call(
        _chdft_kernel,
        grid=(b, FOUR_GROUPS, l // tm),
        in_specs=[pl.BlockSpec((1, tm, FOUR_GC), lambda bi, g, i: (bi, i, cblk + g)),
                  _resident((FOUR_GC, 2 * FOUR_GC), lambda bi, g, i: (0, 0))],
        out_specs=pl.BlockSpec((1, 1, tm, 2 * FOUR_GC), lambda bi, g, i: (bi, g, i, 0)),
        out_shape=jax.ShapeDtypeStruct((b, FOUR_GROUPS, l, 2 * FOUR_GC), BF16),
        compiler_params=_cparams(("parallel", "parallel", "parallel")),
        name="chdft",
    )(p, w)


def _seq_dft_tables(l):
    r = int(round(math.sqrt(l)))
    if r * r != l:
        k = np.arange(l)
        ang = 2.0 * np.pi * ((k[:, None] * k[None, :]) % l) / l
        return (jnp.asarray(np.cos(ang).astype(np.float32)).astype(BF16),
                jnp.asarray(np.sin(ang).astype(np.float32)).astype(BF16))
    col = np.arange(l)
    hi = 2.0 * np.pi * ((np.arange(r)[:, None] * r * col[None, :]) % l) / l
    lo = 2.0 * np.pi * ((np.arange(r)[:, None] * col[None, :]) % l) / l
    ch, sh = [jnp.asarray(f(hi).astype(np.float32))[:, None, :] for f in (np.cos, np.sin)]
    cl, sl = [jnp.asarray(f(lo).astype(np.float32))[None, :, :] for f in (np.cos, np.sin)]
    cos = (ch * cl - sh * sl).reshape(l, l).astype(BF16)
    sin = (sh * cl + ch * sl).reshape(l, l).astype(BF16)
    return cos, sin


def _seqdft_kernel(c_ref, s_ref, z_ref, o_ref, *, scale):
    c = FOUR_GC
    z = z_ref[0, 0]
    y = _dot(c_ref[...], z[:, :c]) + _dot(s_ref[...], z[:, c:])
    o_ref[0] = (y * scale).astype(o_ref.dtype)


def _fourier(p, tm):
    b, l, _ = p.shape
    tm = min(tm, l)
    z = _chdft(p, 1024)
    cos, sin = _seq_dft_tables(l)
    scale = 1.0 / math.sqrt(l * FOUR_GC)
    return pl.pallas_call(
        functools.partial(_seqdft_kernel, scale=scale),
        grid=(l // tm, b, FOUR_GROUPS),
        in_specs=[pl.BlockSpec((tm, l), lambda i, bi, g: (i, 0)),
                  pl.BlockSpec((tm, l), lambda i, bi, g: (i, 0)),
                  pl.BlockSpec((1, 1, l, 2 * FOUR_GC), lambda i, bi, g: (bi, g, 0, 0))],
        out_specs=pl.BlockSpec((1, tm, FOUR_GC), lambda i, bi, g: (bi, i, g)),
        out_shape=jax.ShapeDtypeStruct((b, l, FOUR_W), BF16),
        compiler_params=_cparams(("parallel", "parallel", "parallel")),
        name="seqdft",
    )(cos, sin, z)


CONV_CB = 512


def _ssd_kernel(*refs, fwd, nc, epilogue):
    if fwd:
        xc_ref, dt_ref, dtb_ref, alog_ref, e_ref, h0_ref = refs[:6]
        rest = refs[6:]
        if epilogue:
            z_ref, yb_ref, dsk_ref, nw_ref, y_ref, hout_ref, st_ref = rest
        else:
            y_ref, hout_ref, st_ref = rest
    else:
        (xm_ref, xp_ref, xn_ref, dt_ref, cw_ref, cb_ref, dtb_ref, alog_ref, e_ref, h0_ref,
         y_ref, xc_ref, hout_ref, st_ref, ext_ref) = refs
    c = pl.program_id(1)
    cc = c if fwd else nc - 1 - c
    T = CHUNK

    @pl.when(c == 0)
    def _():
        st_ref[...] = h0_ref[0]

    if fwd:
        xc = lambda cs: xc_ref[0, :, cs].astype(F32)
    else:
        keep_p = jnp.where(cc > 0, 1.0, 0.0)
        keep_n = jnp.where(cc < nc - 1, 1.0, 0.0)
        ext_ref[0:HALO, :] = xp_ref[0].astype(F32) * keep_p
        ext_ref[HALO:HALO + T, :] = xm_ref[0].astype(F32)
        ext_ref[HALO + T:HALO + T + HALO, :] = xn_ref[0].astype(F32) * keep_n
        for j in range(XBC // CONV_CB):
            cs = slice(j * CONV_CB, (j + 1) * CONV_CB)
            acc = jnp.broadcast_to(cb_ref[:, cs], (T, CONV_CB))
            for k in range(CONV_W):
                r0 = HALO - CONV_W // 2 + k
                acc = acc + ext_ref[r0:r0 + T, cs] * cw_ref[k:k + 1, cs]
            xc_ref[0, :, cs] = _silu(acc).astype(xc_ref.dtype)
        xc = lambda cs: xc_ref[0, :, cs].astype(F32)

    xdt = dt_ref[0] + dtb_ref[...]
    dt = jnp.maximum(xdt, 0.0) + jnp.log1p(jnp.exp(-jnp.abs(xdt)))
    a = -jnp.exp(alog_ref[...])
    da = dt * a
    row = lax.broadcasted_iota(jnp.int32, (T, T), 0)
    col = lax.broadcasted_iota(jnp.int32, (T, T), 1)
    causal = (col <= row) if fwd else (col >= row)
    tri = jnp.where(causal, 1.0, 0.0).astype(BF16)
    d1 = da.astype(BF16)
    r1 = da - d1.astype(F32)
    d2 = r1.astype(BF16)
    d3 = (r1 - d2.astype(F32)).astype(BF16)
    acs = _dot(tri, d1) + _dot(tri, d2) + _dot(tri, d3)
    tot = acs[T - 1:T, :] if fwd else acs[0:1, :]
    eacs = jnp.exp(acs)
    dte = jnp.exp(tot - acs)

    e = e_ref[...]

    eh, el = _split2(jnp.broadcast_to(jnp.exp(tot), (16, T)))
    ex = _dot(jnp.concatenate([dt.astype(BF16), eacs.astype(BF16), (dt * dte).astype(BF16), eh, el], axis=0), e)
    dtx = ex[0:T]
    eacs_x = ex[T:2 * T]
    w2 = ex[2 * T:3 * T]
    etot_x = ex[3 * T:3 * T + 1] + ex[3 * T + 16:3 * T + 17]
    acs_t = acs.T
    lane = lax.broadcasted_iota(jnp.int32, (T, 2 * HEAD_DIM), 1)
    hbase = 0 if fwd else SSD_HEADS

    for g in range(SSD_GROUPS):
        gs = slice(g * GROUP_W, (g + 1) * GROUP_W)
        bg = xc(slice(D_SSD + g * D_STATE, D_SSD + (g + 1) * D_STATE))
        cg = xc(slice(D_SSD + (SSD_GROUPS + g) * D_STATE, D_SSD + (SSD_GROUPS + g + 1) * D_STATE))
        bb = bg.astype(BF16)
        cbf = cg.astype(BF16)
        cb = lax.dot_general(cbf, bb, (((1,), (1,)), ((), ())), preferred_element_type=F32)
        st = st_ref[g]
        yoff = _dot(cbf, st.astype(BF16)) * eacs_x[:, gs]
        xg = xc(gs)
        xdb = (xg * dtx[:, gs]).astype(BF16)
        ys = []
        for jp in range(HEADS_PER_GROUP // 2):
            ms = []
            for jj in range(2):
                hc = hbase + g * HEADS_PER_GROUP + 2 * jp + jj
                diff = acs[:, hc:hc + 1] - acs_t[hc:hc + 1, :]
                lm = jnp.exp(jnp.where(causal, diff, -1e30))
                ms.append((cb * lm).astype(BF16))
            r = _dot(jnp.concatenate(ms, axis=0), xdb[:, jp * 2 * HEAD_DIM:(jp + 1) * 2 * HEAD_DIM])
            ys.append(jnp.where(lane < HEAD_DIM, r[:T], r[T:]))
        yg = jnp.concatenate(ys, axis=1) + yoff

        xds = (xg * w2[:, gs]).astype(BF16)
        st_ref[g] = st * etot_x[:, gs] + _dot(bg.T.astype(BF16), xds)

        if epilogue:
            yt = yg + yb_ref[0, :, gs] + dsk_ref[:, gs] * xg
            v = yt * _silu(z_ref[0, :, gs].astype(F32))
            ms_ = jnp.mean(v * v, axis=-1, keepdims=True)
            y_ref[0, :, gs] = ((v * lax.rsqrt(ms_ + EPS)) * nw_ref[:, gs]).astype(y_ref.dtype)
        else:
            y_ref[0, :, gs] = yg

    @pl.when(c == nc - 1)
    def _():
        hout_ref[0] = st_ref[...]


_STATE_BLOCK = (1, SSD_GROUPS, D_STATE, GROUP_W)


def _ssd_bwd(p, pdt, lw, h0):
    b, l, _ = p.shape
    nc = l // CHUNK
    nh = l // HALO
    per = CHUNK // HALO
    cidx = lambda c: nc - 1 - c
    chunk = lambda bi, c: (bi, cidx(c), 0)
    const2 = lambda bi, c: (0, 0)
    state = lambda bi, c: (bi, 0, 0, 0)
    return pl.pallas_call(
        functools.partial(_ssd_kernel, fwd=False, nc=nc, epilogue=False),
        grid=(b, nc),
        in_specs=[pl.BlockSpec((1, CHUNK, XBC), chunk),
                  pl.BlockSpec((1, HALO, XBC), lambda bi, c: (bi, jnp.maximum(cidx(c) * per - 1, 0), 0)),
                  pl.BlockSpec((1, HALO, XBC), lambda bi, c: (bi, jnp.minimum((cidx(c) + 1) * per, nh - 1), 0)),
                  pl.BlockSpec((1, CHUNK, DT_PAD), chunk),
                  _resident((8, XBC), const2),
                  _resident((1, XBC), const2),
                  _resident((1, DT_PAD), const2),
                  _resident((1, DT_PAD), const2),
                  _resident((DT_PAD, D_SSD), const2),
                  pl.BlockSpec(_STATE_BLOCK, state)],
        out_specs=[pl.BlockSpec((1, CHUNK, D_SSD), chunk),
                   pl.BlockSpec((1, CHUNK, XBC), chunk),
                   pl.BlockSpec(_STATE_BLOCK, state)],
        out_shape=[jax.ShapeDtypeStruct((b, l, D_SSD), F32),
                   jax.ShapeDtypeStruct((b, l, XBC), BF16),
                   jax.ShapeDtypeStruct((b,) + _STATE_BLOCK[1:], F32)],
        scratch_shapes=[pltpu.VMEM(_STATE_BLOCK[1:], F32),
                        pltpu.VMEM((CHUNK + 2 * HALO, XBC), F32)],
        compiler_params=_cparams(("parallel", "arbitrary")),
        name="ssd_bwd",
    )(p, p, p, pdt, lw["conv_w"], lw["conv_b"], lw["dt_bias"], lw["a_log"], lw["expand_b"], h0)


def _ssd_fwd(xc, pdt, lw, h0, p=None, yb=None):
    b, l, _ = xc.shape
    nc = l // CHUNK
    chunk = lambda bi, c: (bi, c, 0)
    const2 = lambda bi, c: (0, 0)
    state = lambda bi, c: (bi, 0, 0, 0)
    epilogue = p is not None
    in_specs = [pl.BlockSpec((1, CHUNK, XBC), chunk),
                pl.BlockSpec((1, CHUNK, DT_PAD), chunk),
                _resident((1, DT_PAD), const2),
                _resident((1, DT_PAD), const2),
                _resident((DT_PAD, D_SSD), const2),
                pl.BlockSpec(_STATE_BLOCK, state)]
    args = [xc, pdt, lw["dt_bias"], lw["a_log"], lw["expand_f"], h0]
    if epilogue:
        z_blk = P_Z // D_SSD
        in_specs += [pl.BlockSpec((1, CHUNK, D_SSD), lambda bi, c: (bi, c, z_blk)),
                     pl.BlockSpec((1, CHUNK, D_SSD), chunk),
                     _resident((1, D_SSD), const2),
                     _resident((1, D_SSD), const2)]
        args += [p, yb, lw["d_skip"], lw["ssd_norm_w"]]
    return pl.pallas_call(
        functools.partial(_ssd_kernel, fwd=True, nc=nc, epilogue=epilogue),
        grid=(b, nc),
        in_specs=in_specs,
        out_specs=[pl.BlockSpec((1, CHUNK, D_SSD), chunk),
                   pl.BlockSpec(_STATE_BLOCK, state)],
        out_shape=[jax.ShapeDtypeStruct((b, l, D_SSD), BF16 if epilogue else F32),
                   jax.ShapeDtypeStruct((b,) + _STATE_BLOCK[1:], F32)],
        scratch_shapes=[pltpu.VMEM(_STATE_BLOCK[1:], F32)],
        compiler_params=_cparams(("parallel", "arbitrary")),
        name="ssd_fwd",
    )(*args)


def _ssd(p, pdt, lw, h0f, h0b):
    yb, xc, hb = _ssd_bwd(p, pdt, lw, h0b)
    y, hf = _ssd_fwd(xc, pdt, lw, h0f, p, yb)
    return y, hf, hb


def _expand_matrix(offset):
    e = np.zeros((DT_PAD, D_SSD), np.float32)
    for h in range(SSD_HEADS):
        e[offset + h, h * HEAD_DIM:(h + 1) * HEAD_DIM] = 1.0
    return jnp.asarray(e).astype(BF16)


def _layer_weights(l, conv_w, conv_b, a_log, dt_bias, d_skip, ssd_norm_w):
    pad_h = (0, DT_PAD - 2 * SSD_HEADS)
    return {
        "conv_w": jnp.pad(conv_w[l], ((0, 8 - CONV_W), (0, 0))),
        "conv_b": conv_b[l].reshape(1, XBC),
        "dt_bias": jnp.pad(dt_bias[l].reshape(-1), pad_h).reshape(1, DT_PAD),
        "a_log": jnp.pad(a_log[l].reshape(-1), pad_h).reshape(1, DT_PAD),
        "d_skip": jnp.repeat(d_skip[l], HEAD_DIM).reshape(1, D_SSD),
        "ssd_norm_w": ssd_norm_w[l].reshape(1, D_SSD),
        "expand_f": _expand_matrix(0),
        "expand_b": _expand_matrix(SSD_HEADS),
    }


def _mixer(x, nw, sc, sh, g_m, wts, lw, layer, grid_rows, h0f, h0b):
    b, l, dm = x.shape
    tok = (lambda a: a.reshape(1, b * l, a.shape[-1])) if sc.shape[0] == 1 else (lambda a: a)
    seq = lambda a: a.reshape(b, l, a.shape[-1])
    p, pdt = _inproj(tok(x), nw, sc, sh, wts["w_main"], wts["w_dt"], layer, P_TOT, 1024, 1024)
    y_n, hf, hb = _ssd(seq(p), seq(pdt), lw, h0f, h0b)
    d = _pool_diff(seq(p), grid_rows)
    f = _fourier(seq(p), 1024)
    x = seq(_merge_out(tok(d), tok(f), tok(y_n), p, tok(x), g_m, wts, layer, 256))
    return x, hf, hb


def kernel(x, c, ctx, c_ctx, w_ada, b_ada, norm_mix_w, norm_ffn_w, w_in, conv_w, conv_b, a_log, dt_bias,
           d_skip, ssd_norm_w, w_ssd_out, w_pool, pool_scale, w_fourier, w_out, w_ffn_gate, w_ffn_up,
           w_ffn_down, final_norm_w):
    b, seq, d = x.shape
    rows = seq // GRID_W
    cc = jnp.concatenate([c, c_ctx[None, :], jnp.zeros((8 - b - 1, d), F32)], axis=0)
    mod = _adaln(cc, w_ada, b_ada)
    h0 = jnp.zeros((b,) + _STATE_BLOCK[1:], F32)
    lc = ctx.shape[1]
    w_main, w_dt = _winprep(w_in)
    wts = {
        "w_main": w_main,
        "w_dt": w_dt,
        "w_pool": _to_bf16(w_pool.reshape(DEPTH, POOL_W, POOL_OUT)),
        "pool_scale": pool_scale.reshape(DEPTH, 1, D_MODEL),
        "w_fourier": _to_bf16(w_fourier),
        "w_ssd_out": _to_bf16(w_ssd_out),
        "w_out": _to_bf16(w_out),
    }
    wg, wu, wd = _to_bf16(w_ffn_gate), _to_bf16(w_ffn_up), _to_bf16(w_ffn_down)
    for l in range(DEPTH):
        last = l == DEPTH - 1
        lw = _layer_weights(l, conv_w, conv_b, a_log, dt_bias, d_skip, ssd_norm_w)
        sh_m, sc_m, g_m, sh_f, sc_f, g_f = [mod[l, :b, i * d:(i + 1) * d].reshape(b, 1, d) for i in range(6)]
        csh_m, csc_m, cg_m, csh_f, csc_f, cg_f = [mod[l, b, i * d:(i + 1) * d].reshape(1, 1, d) for i in range(6)]

        if last:
            pc, pdtc = _inproj(ctx.reshape(1, b * lc, d), norm_mix_w[l], csc_m, csh_m, w_main, w_dt, l,
                               XBC, 1024, 1024)
            pdtc = pdtc.reshape(b, lc, DT_PAD)
            _, xcc, hb = _ssd_bwd(pc.reshape(b, lc, XBC), pdtc, lw, h0)
            _, hf = _ssd_fwd(xcc, pdtc, lw, h0)
        else:
            ctx, hf, hb = _mixer(ctx, norm_mix_w[l], csc_m, csh_m, cg_m, wts, lw, l, None, h0, h0)
            ctx = _ffn(ctx.reshape(1, b * lc, d), norm_ffn_w[l], csc_f, csh_f, wg, wu, wd, l,
                       cg_f, final_norm_w, False, 512, 512).reshape(b, lc, d)

        x, _, _ = _mixer(x, norm_mix_w[l], sc_m, sh_m, g_m, wts, lw, l, rows, hf, hb)
        x = _ffn(x, norm_ffn_w[l], sc_f, sh_f, wg, wu, wd, l, g_f, final_norm_w, last, 512, 512)
    return x
```

```python
import functools
import math

import numpy as np
import jax
import jax.numpy as jnp
from jax import lax
from jax.experimental import pallas as pl
from jax.experimental.pallas import tpu as pltpu

F32 = jnp.float32
BF16 = jnp.bfloat16

D_MODEL = 2048
DEPTH = 2
GRID_W = 64
EPS = 1e-6

POOL_GROUPS = 4
POOL_WINDOWS = (2, 4, 8, 16)
POOL_W = D_MODEL // 2
POOL_GC = POOL_W // POOL_GROUPS
POOL_OUT = D_MODEL // POOL_GROUPS

FOUR_GROUPS = 4
FOUR_W = D_MODEL // 2
FOUR_GC = FOUR_W // FOUR_GROUPS

D_SSD = D_MODEL
HEAD_DIM = 64
SSD_HEADS = D_SSD // HEAD_DIM
SSD_GROUPS = 4
HEADS_PER_GROUP = SSD_HEADS // SSD_GROUPS
GROUP_W = HEADS_PER_GROUP * HEAD_DIM
D_STATE = 128
CONV_W = 5
CHUNK = 128

D_FF = ((8 * D_MODEL // 3 + 255) // 256) * 256

XBC = D_SSD + 2 * SSD_GROUPS * D_STATE
SSD_IN = XBC + 2 * SSD_HEADS
Z_OFF = SSD_IN
POOL_OFF = Z_OFF + D_SSD
FOUR_OFF = POOL_OFF + POOL_W
GATE_OFF = FOUR_OFF + FOUR_W
N_IN = GATE_OFF + 3 * D_MODEL

P_XBC = 0
P_POOL = XBC
P_Z = P_POOL + POOL_W
P_GATE = P_Z + D_SSD
P_FOUR = P_GATE + 3 * D_MODEL
P_TOT = P_FOUR + FOUR_W
DT_PAD = 128

V7X_VMEM_LIMIT = 56 * 1024 * 1024
HALO = 16


def _cparams(sem):
    return pltpu.CompilerParams(dimension_semantics=sem, vmem_limit_bytes=V7X_VMEM_LIMIT)


def _resident(shape, index_map):
    return pl.BlockSpec(shape, index_map, pipeline_mode=pl.Buffered(1))


def _split2(v):
    hi = v.astype(BF16)
    lo = (v - hi.astype(F32)).astype(BF16)
    return hi, lo


def _dot(a, b):
    return jnp.dot(a, b, preferred_element_type=F32)


def _sigmoid(v):
    return 0.5 * jnp.tanh(0.5 * v) + 0.5


def _silu(v):
    return v * _sigmoid(v)


def _np_split2(m):
    m = jnp.asarray(np.asarray(m, np.float32))
    hi = m.astype(BF16)
    lo = (m - hi.astype(F32)).astype(BF16)
    return hi, lo


CAST_ROWS = 256
WIN_CHUNK = 1024


def _cast_kernel(w_ref, o_ref):
    o_ref[...] = w_ref[...].astype(BF16)


def _to_bf16(w):
    depth, k, n = w.shape
    return pl.pallas_call(
        _cast_kernel,
        grid=(depth, k // CAST_ROWS),
        in_specs=[pl.BlockSpec((1, CAST_ROWS, n), lambda l, i: (l, i, 0))],
        out_specs=pl.BlockSpec((1, CAST_ROWS, n), lambda l, i: (l, i, 0)),
        out_shape=jax.ShapeDtypeStruct((depth, k, n), BF16),
        compiler_params=_cparams(("parallel", "parallel")),
        name="cast_bf16",
    )(w)


_WIN_SEGMENTS = ((P_XBC, 0, XBC), (P_POOL, POOL_OFF, POOL_W), (P_Z, Z_OFF, D_SSD),
                 (P_GATE, GATE_OFF, 3 * D_MODEL), (P_FOUR, FOUR_OFF, FOUR_W))


def _win_src(j):
    src = jnp.int32(0)
    for dst0, src0, width in _WIN_SEGMENTS:
        first = dst0 // WIN_CHUNK
        inside = (j >= first) & (j < first + width // WIN_CHUNK)
        src = jnp.where(inside, src0 + (j - first) * WIN_CHUNK, src)
    return pl.multiple_of(src, 2 * SSD_HEADS)


def _winprep_kernel(wt_ref, o_ref):
    o_ref[0] = wt_ref[0].T.astype(BF16)


def _winprep(w_in):
    depth, d, n = w_in.shape
    wt = jnp.swapaxes(w_in, 1, 2)
    return pl.pallas_call(
        _winprep_kernel,
        grid=(depth, P_TOT // WIN_CHUNK),
        in_specs=[pl.BlockSpec((pl.Element(1), pl.Element(WIN_CHUNK), pl.Element(d)),
                               lambda l, j: (l, _win_src(j), 0))],
        out_specs=pl.BlockSpec((1, d, WIN_CHUNK), lambda l, j: (l, 0, j)),
        out_shape=jax.ShapeDtypeStruct((depth, d, P_TOT), BF16),
        compiler_params=_cparams(("parallel", "parallel")),
        name="winprep",
    )(wt)


def _adaln_kernel(c_ref, w_ref, b_ref, o_ref):
    s = _silu(c_ref[...])
    w = w_ref[0]
    sh, sl = _split2(s)
    wh, wl = _split2(w)
    o_ref[0] = _dot(sh, wh) + _dot(sl, wh) + _dot(sh, wl) + b_ref[0]


def _adaln(cc, w_ada, b_ada, tn=1024):
    depth, d, n = w_ada.shape
    return pl.pallas_call(
        _adaln_kernel,
        grid=(depth, n // tn),
        in_specs=[pl.BlockSpec((8, d), lambda l, j: (0, 0)),
                  pl.BlockSpec((1, d, tn), lambda l, j: (l, 0, j)),
                  pl.BlockSpec((1, 1, tn), lambda l, j: (l, 0, j))],
        out_specs=pl.BlockSpec((1, 8, tn), lambda l, j: (l, 0, j)),
        out_shape=jax.ShapeDtypeStruct((depth, 8, n), F32),
        compiler_params=_cparams(("parallel", "parallel")),
        name="adaln",
    )(cc, w_ada, b_ada.reshape(depth, 1, n))


def _norm_mod(x, nw, sc, sh):
    ms = jnp.mean(x * x, axis=-1, keepdims=True)
    return (x * lax.rsqrt(ms + EPS)) * nw * (1.0 + sc) + sh


def _inproj_kernel(x_ref, nw_ref, sc_ref, sh_ref, w_ref, wdt_ref, o_ref, dt_ref, h_ref):
    @pl.when(pl.program_id(2) == 0)
    def _():
        h = _norm_mod(x_ref[0], nw_ref[...], sc_ref[0], sh_ref[0]).astype(BF16)
        h_ref[...] = h
        dt_ref[0] = _dot(h, wdt_ref[0])

    o_ref[0] = _dot(h_ref[...], w_ref[0]).astype(o_ref.dtype)


def _inproj(x, nw, sc, sh, w, wdt, layer, n_out, tm, tn):
    b, l, d = x.shape
    tm = min(tm, l)
    return pl.pallas_call(
        _inproj_kernel,
        grid=(b, l // tm, n_out // tn),
        in_specs=[pl.BlockSpec((1, tm, d), lambda bi, i, j: (bi, i, 0)),
                  pl.BlockSpec((1, d), lambda bi, i, j: (0, 0)),
                  pl.BlockSpec((1, 1, d), lambda bi, i, j: (bi, 0, 0)),
                  pl.BlockSpec((1, 1, d), lambda bi, i, j: (bi, 0, 0)),
                  pl.BlockSpec((1, d, tn), lambda bi, i, j: (layer, 0, j)),
                  _resident((1, d, DT_PAD), lambda bi, i, j: (layer, 0, 0))],
        out_specs=[pl.BlockSpec((1, tm, tn), lambda bi, i, j: (bi, i, j)),
                   pl.BlockSpec((1, tm, DT_PAD), lambda bi, i, j: (bi, i, 0))],
        out_shape=[jax.ShapeDtypeStruct((b, l, n_out), BF16),
                   jax.ShapeDtypeStruct((b, l, DT_PAD), F32)],
        scratch_shapes=[pltpu.VMEM((tm, d), BF16)],
        compiler_params=_cparams(("parallel", "parallel", "arbitrary")),
        name="inproj",
    )(x, nw.reshape(1, d), sc, sh, w, wdt)


def _merge_kernel(d_ref, f_ref, yn_ref, gp_ref, gf_ref, gs_ref, x_ref, gm_ref,
                  wp_ref, ps_ref, wf_ref, ws_ref, wo_ref, o_ref):
    d = d_ref[0]
    y_pool = jnp.concatenate(
        [_dot(d[:, g * POOL_GC:(g + 1) * POOL_GC], wp_ref[0, g * POOL_GC:(g + 1) * POOL_GC, :])
         for g in range(POOL_GROUPS)], axis=1)
    m = _sigmoid(gp_ref[0].astype(F32)) * (y_pool * ps_ref[0])
    m = m + _sigmoid(gf_ref[0].astype(F32)) * _dot(f_ref[0], wf_ref[0])
    m = m + _sigmoid(gs_ref[0].astype(F32)) * _dot(yn_ref[0], ws_ref[0])
    o_ref[0] = x_ref[0] + gm_ref[0] * _dot(m.astype(BF16), wo_ref[0])


def _merge_out(d, f, yn, p, x, g_m, wts, layer, tm):
    b, l, dm = x.shape
    tm = min(tm, l)
    gblk = P_GATE // dm
    row = lambda bi, i: (bi, i, 0)
    lyr = lambda bi, i: (layer, 0, 0)
    return pl.pallas_call(
        _merge_kernel,
        grid=(b, l // tm),
        in_specs=[pl.BlockSpec((1, tm, POOL_W), row),
                  pl.BlockSpec((1, tm, FOUR_W), row),
                  pl.BlockSpec((1, tm, D_SSD), row),
                  pl.BlockSpec((1, tm, dm), lambda bi, i: (bi, i, gblk)),
                  pl.BlockSpec((1, tm, dm), lambda bi, i: (bi, i, gblk + 1)),
                  pl.BlockSpec((1, tm, dm), lambda bi, i: (bi, i, gblk + 2)),
                  pl.BlockSpec((1, tm, dm), row),
                  pl.BlockSpec((1, 1, dm), lambda bi, i: (bi, 0, 0)),
                  _resident((1, POOL_W, POOL_OUT), lyr),
                  _resident((1, 1, dm), lyr),
                  _resident((1, FOUR_W, dm), lyr),
                  _resident((1, D_SSD, dm), lyr),
                  _resident((1, dm, dm), lyr)],
        out_specs=pl.BlockSpec((1, tm, dm), row),
        out_shape=jax.ShapeDtypeStruct((b, l, dm), F32),
        compiler_params=_cparams(("parallel", "parallel")),
        name="merge_out",
    )(d, f, yn, p, p, p, x, g_m, wts["w_pool"], wts["pool_scale"], wts["w_fourier"], wts["w_ssd_out"],
      wts["w_out"])


def _ffn_kernel(x_ref, nw_ref, sc_ref, sh_ref, wg_ref, wu_ref, wd_ref, gate_ref, fnw_ref,
                o_ref, h_ref, acc_ref, *, final_norm):
    j = pl.program_id(2)

    @pl.when(j == 0)
    def _():
        h_ref[...] = _norm_mod(x_ref[0], nw_ref[...], sc_ref[0], sh_ref[0]).astype(BF16)
        acc_ref[...] = jnp.zeros_like(acc_ref)

    h = h_ref[...]
    a = _silu(_dot(h, wg_ref[0])) * _dot(h, wu_ref[0])
    acc_ref[...] += _dot(a.astype(BF16), wd_ref[0])

    @pl.when(j == pl.num_programs(2) - 1)
    def _():
        o = x_ref[0] + gate_ref[0] * acc_ref[...]
        if final_norm:
            ms = jnp.mean(o * o, axis=-1, keepdims=True)
            o = (o * lax.rsqrt(ms + EPS)) * fnw_ref[...]
        o_ref[0] = o


def _ffn(x, nw, sc, sh, wg, wu, wd, layer, gate, fnw, final_norm, tm, tf):
    b, l, d = x.shape
    ff = wg.shape[2]
    tm = min(tm, l)
    vec = lambda bi, i, j: (bi, 0, 0)
    return pl.pallas_call(
        functools.partial(_ffn_kernel, final_norm=final_norm),
        grid=(b, l // tm, ff // tf),
        in_specs=[pl.BlockSpec((1, tm, d), lambda bi, i, j: (bi, i, 0)),
                  pl.BlockSpec((1, d), lambda bi, i, j: (0, 0)),
                  pl.BlockSpec((1, 1, d), vec),
                  pl.BlockSpec((1, 1, d), vec),
                  pl.BlockSpec((1, d, tf), lambda bi, i, j: (layer, 0, j)),
                  pl.BlockSpec((1, d, tf), lambda bi, i, j: (layer, 0, j)),
                  pl.BlockSpec((1, tf, d), lambda bi, i, j: (layer, j, 0)),
                  pl.BlockSpec((1, 1, d), vec),
                  pl.BlockSpec((1, d), lambda bi, i, j: (0, 0))],
        out_specs=pl.BlockSpec((1, tm, d), lambda bi, i, j: (bi, i, 0)),
        out_shape=jax.ShapeDtypeStruct((b, l, d), F32),
        scratch_shapes=[pltpu.VMEM((tm, d), BF16), pltpu.VMEM((tm, d), F32)],
        compiler_params=_cparams(("parallel", "parallel", "arbitrary")),
        name="ffn",
    )(x, nw.reshape(1, d), sc, sh, wg, wu, wd, gate, fnw.reshape(1, d))


def _box_matrix(n, w):
    idx = np.arange(n)
    lo = np.clip(idx - w // 2, 0, n)
    hi = np.clip(idx + (w - w // 2), 0, n)
    m = ((idx[None, :] >= lo[:, None]) & (idx[None, :] < hi[:, None])).astype(np.float64)
    return m / (hi - lo)[:, None]


POOL_TB = 256
POOL_PAD = 16


def _pool_kernel(u_ref, mh_ref, ml_ref, o_ref, *scratch, grid_rows):
    g = pl.program_id(1)
    l = u_ref.shape[1]
    mh = mh_ref[0]
    ml = ml_ref[0]
    if grid_rows is None:
        v = u_ref[0]
        o_ref[0] = (_dot(mh, v) + _dot(ml, v) - v.astype(F32)).astype(o_ref.dtype)
        return

    cp_ref, = scratch
    pad = POOL_PAD * GRID_W
    cp_ref[0:pad, :] = jnp.zeros((pad, POOL_GC), F32)
    cp_ref[pad + l:pad + l + pad, :] = jnp.zeros((pad, POOL_GC), F32)
    for i in range(l // POOL_TB):
        v = u_ref[0, i * POOL_TB:(i + 1) * POOL_TB, :]
        cp_ref[pad + i * POOL_TB:pad + (i + 1) * POOL_TB, :] = _dot(mh, v) + _dot(ml, v)

    for gi, w in enumerate(POOL_WINDOWS):
        lo_off, hi_off = -(w // 2), w - w // 2

        @pl.when(g == gi)
        def _(lo_off=lo_off, hi_off=hi_off):
            def body(r, carry):
                s = jnp.zeros((GRID_W, POOL_GC), F32)
                for o in range(lo_off, hi_off):
                    start = pl.multiple_of(pad + (r + o) * GRID_W, GRID_W)
                    s = s + cp_ref[pl.ds(start, GRID_W), :]
                cnt = jnp.minimum(r + hi_off, grid_rows) - jnp.maximum(r + lo_off, 0)
                cntv = jnp.full((GRID_W, POOL_GC), cnt, jnp.int32).astype(F32)
                t0 = pl.multiple_of(r * GRID_W, GRID_W)
                v = u_ref[0, pl.ds(t0, GRID_W), :].astype(F32)
                o_ref[0, pl.ds(t0, GRID_W), :] = (s / cntv - v).astype(o_ref.dtype)
                return carry

            lax.fori_loop(0, grid_rows, body, 0)


def _pool_diff(p, grid_rows):
    b, l, _ = p.shape
    mats = []
    for w in POOL_WINDOWS:
        if grid_rows is None:
            mats.append(_box_matrix(l, w))
        else:
            mats.append(np.kron(np.eye(POOL_TB // GRID_W), _box_matrix(GRID_W, w)))
    mh, ml = _np_split2(np.stack(mats))
    tb = mats[0].shape[0]
    scratch = []
    if grid_rows is not None:
        scratch = [pltpu.VMEM((l + 2 * POOL_PAD * GRID_W, POOL_GC), F32)]
    cblk = P_POOL // POOL_GC
    return pl.pallas_call(
        functools.partial(_pool_kernel, grid_rows=grid_rows),
        grid=(b, POOL_GROUPS),
        in_specs=[pl.BlockSpec((1, l, POOL_GC), lambda bi, g: (bi, 0, cblk + g)),
                  pl.BlockSpec((1, tb, tb), lambda bi, g: (g, 0, 0)),
                  pl.BlockSpec((1, tb, tb), lambda bi, g: (g, 0, 0))],
        out_specs=pl.BlockSpec((1, l, POOL_GC), lambda bi, g: (bi, 0, g)),
        out_shape=jax.ShapeDtypeStruct((b, l, POOL_W), BF16),
        scratch_shapes=scratch,
        compiler_params=_cparams(("parallel", "parallel")),
        name="pool_diff",
    )(p, mh, ml)


def _chdft_fold_kernel(u_ref, ua_ref, ub_ref, w_ref, flip_ref, o_ref):
    i = pl.program_id(2)
    c = FOUR_GC
    tm = u_ref.shape[1]
    w = w_ref[...]
    urev = _dot(flip_ref[...], ua_ref[0])
    row = lax.broadcasted_iota(jnp.int32, (tm, c), 0)
    urev = jnp.where(row == 0, jnp.broadcast_to(ub_ref[0, 0:1, :].astype(F32), (tm, c)), urev).astype(BF16)
    z = _dot(u_ref[0], w)
    zrev = _dot(urev, w)
    first = (row + jnp.minimum(i, 1)) == 0
    er = z[:, :c] + jnp.where(first, 0.0, zrev[:, :c])
    oi = jnp.where(first, zrev[:, :c], z[:, c:] - zrev[:, c:])
    o_ref[0, 0, :, :c] = er.astype(o_ref.dtype)
    o_ref[0, 0, :, c:] = oi.astype(o_ref.dtype)


def _chdft_fold(p, tm):
    b, l, _ = p.shape
    half = l // 2
    tm = min(tm, half)
    nb = l // tm
    k = np.arange(FOUR_GC)
    ang = -2.0 * np.pi * ((k[:, None] * k[None, :]) % FOUR_GC) / FOUR_GC
    w = jnp.asarray(np.concatenate([np.cos(ang), np.sin(ang)], axis=1).astype(np.float32)).astype(BF16)
    flip = np.zeros((tm, tm), np.float32)
    flip[np.arange(1, tm), tm - np.arange(1, tm)] = 1.0
    cblk = P_FOUR // FOUR_GC
    return pl.pallas_call(
        _chdft_fold_kernel,
        grid=(b, FOUR_GROUPS, half // tm),
        in_specs=[pl.BlockSpec((1, tm, FOUR_GC), lambda bi, g, i: (bi, i, cblk + g)),
                  pl.BlockSpec((1, tm, FOUR_GC), lambda bi, g, i: (bi, nb - 1 - i, cblk + g)),
                  pl.BlockSpec((1, tm, FOUR_GC), lambda bi, g, i: (bi, jnp.where(i == 0, nb // 2, nb - i), cblk + g)),
                  _resident((FOUR_GC, 2 * FOUR_GC), lambda bi, g, i: (0, 0)),
                  _resident((tm, tm), lambda bi, g, i: (0, 0))],
        out_specs=pl.BlockSpec((1, 1, tm, 2 * FOUR_GC), lambda bi, g, i: (bi, g, i, 0)),
        out_shape=jax.ShapeDtypeStruct((b, FOUR_GROUPS, half, 2 * FOUR_GC), BF16),
        compiler_params=_cparams(("parallel", "parallel", "parallel")),
        name="chdft",
    )(p, p, p, w, jnp.asarray(flip).astype(BF16))


def _seq_dft_tables(l):
    half = l // 2
    r = int(round(math.sqrt(l)))
    assert r * r == l
    col = np.arange(half)
    hi = 2.0 * np.pi * ((np.arange(r)[:, None] * r * col[None, :]) % l) / l
    lo = 2.0 * np.pi * ((np.arange(r)[:, None] * col[None, :]) % l) / l
    ch, sh = [jnp.asarray(f(hi).astype(np.float32))[:, None, :] for f in (np.cos, np.sin)]
    cl, sl = [jnp.asarray(f(lo).astype(np.float32))[None, :, :] for f in (np.cos, np.sin)]
    cos = (ch * cl - sh * sl).reshape(l, half)
    sin = (sh * cl + ch * sl).reshape(l, half)
    nyq = jnp.asarray((1.0 - 2.0 * (np.arange(l) % 2)).astype(np.float32))[:, None]
    sin = jnp.where(jnp.asarray(col)[None, :] == 0, nyq, sin)
    return cos.astype(BF16), sin.astype(BF16)


def _seqdft_kernel(c_ref, s_ref, e_ref, o_ref, *, scale):
    c = FOUR_GC
    e = e_ref[0, 0]
    y = _dot(c_ref[...], e[:, :c]) + _dot(s_ref[...], e[:, c:])
    o_ref[0] = (y * scale).astype(o_ref.dtype)


def _fourier(p, tm):
    b, l, _ = p.shape
    half = l // 2
    tm = min(tm, l)
    e = _chdft_fold(p, 512)
    cos, sin = _seq_dft_tables(l)
    scale = 1.0 / math.sqrt(l * FOUR_GC)
    return pl.pallas_call(
        functools.partial(_seqdft_kernel, scale=scale),
        grid=(l // tm, b, FOUR_GROUPS),
        in_specs=[pl.BlockSpec((tm, half), lambda i, bi, g: (i, 0)),
                  pl.BlockSpec((tm, half), lambda i, bi, g: (i, 0)),
                  pl.BlockSpec((1, 1, half, 2 * FOUR_GC), lambda i, bi, g: (bi, g, 0, 0))],
        out_specs=pl.BlockSpec((1, tm, FOUR_GC), lambda i, bi, g: (bi, i, g)),
        out_shape=jax.ShapeDtypeStruct((b, l, FOUR_W), BF16),
        compiler_params=_cparams(("parallel", "parallel", "parallel")),
        name="seqdft",
    )(cos, sin, e)


CONV_CB = 512
_SHIFT_TAPS = tuple(k for k in range(CONV_W) if k != CONV_W // 2)


def _shift_matrix():
    m = np.zeros((len(_SHIFT_TAPS) * CHUNK, CHUNK + 2 * HALO), np.float32)
    t = np.arange(CHUNK)
    for q, k in enumerate(_SHIFT_TAPS):
        m[q * CHUNK + t, HALO - CONV_W // 2 + k + t] = 1.0
    return jnp.asarray(m).astype(BF16)


def _ssd_kernel(*refs, fwd, nc, epilogue):
    if fwd:
        xc_ref, dt_ref, dtb_ref, alog_ref, e_ref, h0_ref = refs[:6]
        rest = refs[6:]
        if epilogue:
            z_ref, yb_ref, dsk_ref, nw_ref, y_ref, hout_ref, st_ref = rest
        else:
            y_ref, hout_ref, st_ref = rest
    else:
        (xm_ref, xp_ref, xn_ref, dt_ref, cw_ref, cb_ref, shift_ref, dtb_ref, alog_ref, e_ref, h0_ref,
         y_ref, xc_ref, hout_ref, st_ref, ext_ref) = refs
    c = pl.program_id(1)
    cc = c if fwd else nc - 1 - c
    T = CHUNK

    @pl.when(c == 0)
    def _():
        st_ref[...] = h0_ref[0]

    if fwd:
        xc = lambda cs: xc_ref[0, :, cs].astype(F32)
    else:
        @pl.when(cc > 0)
        def _():
            ext_ref[0:HALO, :] = xp_ref[0]

        @pl.when(cc == 0)
        def _():
            ext_ref[0:HALO, :] = jnp.zeros((HALO, XBC), ext_ref.dtype)

        ext_ref[HALO:HALO + T, :] = xm_ref[0]

        @pl.when(cc < nc - 1)
        def _():
            ext_ref[HALO + T:HALO + T + HALO, :] = xn_ref[0]

        @pl.when(cc == nc - 1)
        def _():
            ext_ref[HALO + T:HALO + T + HALO, :] = jnp.zeros((HALO, XBC), ext_ref.dtype)

        shift = shift_ref[...]
        for j in range(XBC // CONV_CB):
            cs = slice(j * CONV_CB, (j + 1) * CONV_CB)
            sh = _dot(shift, ext_ref[:, cs])
            acc = cb_ref[:, cs] + xm_ref[0, :, cs].astype(F32) * cw_ref[CONV_W // 2:CONV_W // 2 + 1, cs]
            for q, k in enumerate(_SHIFT_TAPS):
                acc = acc + sh[q * T:(q + 1) * T] * cw_ref[k:k + 1, cs]
            xc_ref[0, :, cs] = _silu(acc).astype(xc_ref.dtype)
        xc = lambda cs: xc_ref[0, :, cs].astype(F32)

    xdt = dt_ref[0] + dtb_ref[...]
    dt = jnp.maximum(xdt, 0.0) + jnp.log1p(jnp.exp(-jnp.abs(xdt)))
    a = -jnp.exp(alog_ref[...])
    da = dt * a
    row = lax.broadcasted_iota(jnp.int32, (T, T), 0)
    col = lax.broadcasted_iota(jnp.int32, (T, T), 1)
    causal = (col <= row) if fwd else (col >= row)
    tri = jnp.where(causal, 1.0, 0.0).astype(BF16)
    d1 = da.astype(BF16)
    r1 = da - d1.astype(F32)
    d2 = r1.astype(BF16)
    d3 = (r1 - d2.astype(F32)).astype(BF16)
    acs = _dot(tri, d1) + _dot(tri, d2) + _dot(tri, d3)
    tot = acs[T - 1:T, :] if fwd else acs[0:1, :]
    eacs = jnp.exp(acs)
    dte = jnp.exp(tot - acs)

    e = e_ref[...]

    eh, el = _split2(jnp.broadcast_to(jnp.exp(tot), (16, T)))
    ex = _dot(jnp.concatenate([dt.astype(BF16), eacs.astype(BF16), (dt * dte).astype(BF16), eh, el], axis=0), e)
    dtx = ex[0:T]
    eacs_x = ex[T:2 * T]
    w2 = ex[2 * T:3 * T]
    etot_x = ex[3 * T:3 * T + 1] + ex[3 * T + 16:3 * T + 17]
    acs_t = acs.T
    lane = lax.broadcasted_iota(jnp.int32, (T, 2 * HEAD_DIM), 1)
    hbase = 0 if fwd else SSD_HEADS

    for g in range(SSD_GROUPS):
        gs = slice(g * GROUP_W, (g + 1) * GROUP_W)
        bg = xc(slice(D_SSD + g * D_STATE, D_SSD + (g + 1) * D_STATE))
        cg = xc(slice(D_SSD + (SSD_GROUPS + g) * D_STATE, D_SSD + (SSD_GROUPS + g + 1) * D_STATE))
        bb = bg.astype(BF16)
        cbf = cg.astype(BF16)
        cb = lax.dot_general(cbf, bb, (((1,), (1,)), ((), ())), preferred_element_type=F32)
        st = st_ref[g]
        yoff = _dot(cbf, st.astype(BF16)) * eacs_x[:, gs]
        xg = xc(gs)
        xdb = (xg * dtx[:, gs]).astype(BF16)
        ys = []
        for jp in range(HEADS_PER_GROUP // 2):
            ms = []
            for jj in range(2):
                hc = hbase + g * HEADS_PER_GROUP + 2 * jp + jj
                diff = acs[:, hc:hc + 1] - acs_t[hc:hc + 1, :]
                lm = jnp.exp(jnp.where(causal, diff, -1e30))
                ms.append((cb * lm).astype(BF16))
            r = _dot(jnp.concatenate(ms, axis=0), xdb[:, jp * 2 * HEAD_DIM:(jp + 1) * 2 * HEAD_DIM])
            ys.append(jnp.where(lane < HEAD_DIM, r[:T], r[T:]))
        yg = jnp.concatenate(ys, axis=1) + yoff

        xds = (xg * w2[:, gs]).astype(BF16)
        st_ref[g] = st * etot_x[:, gs] + _dot(bg.T.astype(BF16), xds)

        if epilogue:
            yt = yg + yb_ref[0, :, gs] + dsk_ref[:, gs] * xg
            v = yt * _silu(z_ref[0, :, gs].astype(F32))
            ms_ = jnp.mean(v * v, axis=-1, keepdims=True)
            y_ref[0, :, gs] = ((v * lax.rsqrt(ms_ + EPS)) * nw_ref[:, gs]).astype(y_ref.dtype)
        else:
            y_ref[0, :, gs] = yg

    @pl.when(c == nc - 1)
    def _():
        hout_ref[0] = st_ref[...]


_STATE_BLOCK = (1, SSD_GROUPS, D_STATE, GROUP_W)


def _ssd_bwd(p, pdt, lw, h0):
    b, l, _ = p.shape
    nc = l // CHUNK
    nh = l // HALO
    per = CHUNK // HALO
    cidx = lambda c: nc - 1 - c
    chunk = lambda bi, c: (bi, cidx(c), 0)
    const2 = lambda bi, c: (0, 0)
    state = lambda bi, c: (bi, 0, 0, 0)
    return pl.pallas_call(
        functools.partial(_ssd_kernel, fwd=False, nc=nc, epilogue=False),
        grid=(b, nc),
        in_specs=[pl.BlockSpec((1, CHUNK, XBC), chunk),
                  pl.BlockSpec((1, HALO, XBC), lambda bi, c: (bi, jnp.maximum(cidx(c) * per - 1, 0), 0)),
                  pl.BlockSpec((1, HALO, XBC), lambda bi, c: (bi, jnp.minimum((cidx(c) + 1) * per, nh - 1), 0)),
                  pl.BlockSpec((1, CHUNK, DT_PAD), chunk),
                  _resident((8, XBC), const2),
                  _resident((1, XBC), const2),
                  _resident((len(_SHIFT_TAPS) * CHUNK, CHUNK + 2 * HALO), const2),
                  _resident((1, DT_PAD), const2),
                  _resident((1, DT_PAD), const2),
                  _resident((DT_PAD, D_SSD), const2),
                  pl.BlockSpec(_STATE_BLOCK, state)],
        out_specs=[pl.BlockSpec((1, CHUNK, D_SSD), chunk),
                   pl.BlockSpec((1, CHUNK, XBC), chunk),
                   pl.BlockSpec(_STATE_BLOCK, state)],
        out_shape=[jax.ShapeDtypeStruct((b, l, D_SSD), F32),
                   jax.ShapeDtypeStruct((b, l, XBC), BF16),
                   jax.ShapeDtypeStruct((b,) + _STATE_BLOCK[1:], F32)],
        scratch_shapes=[pltpu.VMEM(_STATE_BLOCK[1:], F32),
                        pltpu.VMEM((CHUNK + 2 * HALO, XBC), BF16)],
        compiler_params=_cparams(("parallel", "arbitrary")),
        name="ssd_bwd",
    )(p, p, p, pdt, lw["conv_w"], lw["conv_b"], _shift_matrix(), lw["dt_bias"], lw["a_log"], lw["expand_b"], h0)


def _ssd_fwd(xc, pdt, lw, h0, p=None, yb=None):
    b, l, _ = xc.shape
    nc = l // CHUNK
    chunk = lambda bi, c: (bi, c, 0)
    const2 = lambda bi, c: (0, 0)
    state = lambda bi, c: (bi, 0, 0, 0)
    epilogue = p is not None
    in_specs = [pl.BlockSpec((1, CHUNK, XBC), chunk),
                pl.BlockSpec((1, CHUNK, DT_PAD), chunk),
                _resident((1, DT_PAD), const2),
                _resident((1, DT_PAD), const2),
                _resident((DT_PAD, D_SSD), const2),
                pl.BlockSpec(_STATE_BLOCK, state)]
    args = [xc, pdt, lw["dt_bias"], lw["a_log"], lw["expand_f"], h0]
    if epilogue:
        z_blk = P_Z // D_SSD
        in_specs += [pl.BlockSpec((1, CHUNK, D_SSD), lambda bi, c: (bi, c, z_blk)),
                     pl.BlockSpec((1, CHUNK, D_SSD), chunk),
                     _resident((1, D_SSD), const2),
                     _resident((1, D_SSD), const2)]
        args += [p, yb, lw["d_skip"], lw["ssd_norm_w"]]
    return pl.pallas_call(
        functools.partial(_ssd_kernel, fwd=True, nc=nc, epilogue=epilogue),
        grid=(b, nc),
        in_specs=in_specs,
        out_specs=[pl.BlockSpec((1, CHUNK, D_SSD), chunk),
                   pl.BlockSpec(_STATE_BLOCK, state)],
        out_shape=[jax.ShapeDtypeStruct((b, l, D_SSD), BF16 if epilogue else F32),
                   jax.ShapeDtypeStruct((b,) + _STATE_BLOCK[1:], F32)],
        scratch_shapes=[pltpu.VMEM(_STATE_BLOCK[1:], F32)],
        compiler_params=_cparams(("parallel", "arbitrary")),
        name="ssd_fwd",
    )(*args)


def _ssd(p, pdt, lw, h0f, h0b):
    yb, xc, hb = _ssd_bwd(p, pdt, lw, h0b)
    y, hf = _ssd_fwd(xc, pdt, lw, h0f, p, yb)
    return y, hf, hb


def _expand_matrix(offset):
    e = np.zeros((DT_PAD, D_SSD), np.float32)
    for h in range(SSD_HEADS):
        e[offset + h, h * HEAD_DIM:(h + 1) * HEAD_DIM] = 1.0
    return jnp.asarray(e).astype(BF16)


def _layer_weights(l, conv_w, conv_b, a_log, dt_bias, d_skip, ssd_norm_w):
    pad_h = (0, DT_PAD - 2 * SSD_HEADS)
    return {
        "conv_w": jnp.pad(conv_w[l], ((0, 8 - CONV_W), (0, 0))),
        "conv_b": conv_b[l].reshape(1, XBC),
        "dt_bias": jnp.pad(dt_bias[l].reshape(-1), pad_h).reshape(1, DT_PAD),
        "a_log": jnp.pad(a_log[l].reshape(-1), pad_h).reshape(1, DT_PAD),
        "d_skip": jnp.repeat(d_skip[l], HEAD_DIM).reshape(1, D_SSD),
        "ssd_norm_w": ssd_norm_w[l].reshape(1, D_SSD),
        "expand_f": _expand_matrix(0),
        "expand_b": _expand_matrix(SSD_HEADS),
    }


def _mixer(x, nw, sc, sh, g_m, wts, lw, layer, grid_rows, h0f, h0b):
    b, l, dm = x.shape
    tok = (lambda a: a.reshape(1, b * l, a.shape[-1])) if sc.shape[0] == 1 else (lambda a: a)
    seq = lambda a: a.reshape(b, l, a.shape[-1])
    p, pdt = _inproj(tok(x), nw, sc, sh, wts["w_main"], wts["w_dt"], layer, P_TOT, 1024, 1024)
    y_n, hf, hb = _ssd(seq(p), seq(pdt), lw, h0f, h0b)
    d = _pool_diff(seq(p), grid_rows)
    f = _fourier(seq(p), 2048)
    x = seq(_merge_out(tok(d), tok(f), tok(y_n), p, tok(x), g_m, wts, layer, 256))
    return x, hf, hb


def kernel(x, c, ctx, c_ctx, w_ada, b_ada, norm_mix_w, norm_ffn_w, w_in, conv_w, conv_b, a_log, dt_bias,
           d_skip, ssd_norm_w, w_ssd_out, w_pool, pool_scale, w_fourier, w_out, w_ffn_gate, w_ffn_up,
           w_ffn_down, final_norm_w):
    b, seq, d = x.shape
    rows = seq // GRID_W
    cc = jnp.concatenate([c, c_ctx[None, :], jnp.zeros((8 - b - 1, d), F32)], axis=0)
    mod = _adaln(cc, w_ada, b_ada)
    h0 = jnp.zeros((b,) + _STATE_BLOCK[1:], F32)
    lc = ctx.shape[1]
    w_main = _winprep(w_in)
    w_dt = jnp.pad(w_in[:, :, XBC:SSD_IN], ((0, 0), (0, 0), (0, DT_PAD - 2 * SSD_HEADS))).astype(BF16)
    wts = {
        "w_main": w_main,
        "w_dt": w_dt,
        "w_pool": _to_bf16(w_pool.reshape(DEPTH, POOL_W, POOL_OUT)),
        "pool_scale": pool_scale.reshape(DEPTH, 1, D_MODEL),
        "w_fourier": _to_bf16(w_fourier),
        "w_ssd_out": _to_bf16(w_ssd_out),
        "w_out": _to_bf16(w_out),
    }
    wg, wu, wd = _to_bf16(w_ffn_gate), _to_bf16(w_ffn_up), _to_bf16(w_ffn_down)
    for l in range(DEPTH):
        last = l == DEPTH - 1
        lw = _layer_weights(l, conv_w, conv_b, a_log, dt_bias, d_skip, ssd_norm_w)
        sh_m, sc_m, g_m, sh_f, sc_f, g_f = [mod[l, :b, i * d:(i + 1) * d].reshape(b, 1, d) for i in range(6)]
        csh_m, csc_m, cg_m, csh_f, csc_f, cg_f = [mod[l, b, i * d:(i + 1) * d].reshape(1, 1, d) for i in range(6)]

        if last:
            pc, pdtc = _inproj(ctx.reshape(1, b * lc, d), norm_mix_w[l], csc_m, csh_m, w_main, w_dt, l,
                               XBC, 1024, 1024)
            pdtc = pdtc.reshape(b, lc, DT_PAD)
            _, xcc, hb = _ssd_bwd(pc.reshape(b, lc, XBC), pdtc, lw, h0)
            _, hf = _ssd_fwd(xcc, pdtc, lw, h0)
        else:
            ctx, hf, hb = _mixer(ctx, norm_mix_w[l], csc_m, csh_m, cg_m, wts, lw, l, None, h0, h0)
            ctx = _ffn(ctx.reshape(1, b * lc, d), norm_ffn_w[l], csc_f, csh_f, wg, wu, wd, l,
                       cg_f, final_norm_w, False, 512, 512).reshape(b, lc, d)

        x, _, _ = _mixer(x, norm_mix_w[l], sc_m, sh_m, g_m, wts, lw, l, rows, hf, hb)
        x = _ffn(x, norm_ffn_w[l], sc_f, sh_f, wg, wu, wd, l, g_f, final_norm_w, last, 512, 512)
    return x
```

```python
import functools
import math

import numpy as np
import jax
import jax.numpy as jnp
from jax import lax
from jax.experimental import pallas as pl
from jax.experimental.pallas import tpu as pltpu

F32 = jnp.float32
BF16 = jnp.bfloat16

D_MODEL = 2048
DEPTH = 2
GRID_W = 64
EPS = 1e-6

POOL_GROUPS = 4
POOL_WINDOWS = (2, 4, 8, 16)
POOL_W = D_MODEL // 2
POOL_GC = POOL_W // POOL_GROUPS
POOL_OUT = D_MODEL // POOL_GROUPS

FOUR_GROUPS = 4
FOUR_W = D_MODEL // 2
FOUR_GC = FOUR_W // FOUR_GROUPS

D_SSD = D_MODEL
HEAD_DIM = 64
SSD_HEADS = D_SSD // HEAD_DIM
SSD_GROUPS = 4
HEADS_PER_GROUP = SSD_HEADS // SSD_GROUPS
GROUP_W = HEADS_PER_GROUP * HEAD_DIM
D_STATE = 128
CONV_W = 5
CHUNK = 128

D_FF = ((8 * D_MODEL // 3 + 255) // 256) * 256

XBC = D_SSD + 2 * SSD_GROUPS * D_STATE
SSD_IN = XBC + 2 * SSD_HEADS
Z_OFF = SSD_IN
POOL_OFF = Z_OFF + D_SSD
FOUR_OFF = POOL_OFF + POOL_W
GATE_OFF = FOUR_OFF + FOUR_W
N_IN = GATE_OFF + 3 * D_MODEL

P_XBC = 0
P_POOL = XBC
P_Z = P_POOL + POOL_W
P_GATE = P_Z + D_SSD
P_FOUR = P_GATE + 3 * D_MODEL
P_TOT = P_FOUR + FOUR_W
DT_PAD = 128

V7X_VMEM_LIMIT = 56 * 1024 * 1024
HALO = 16


def _cparams(sem):
    return pltpu.CompilerParams(dimension_semantics=sem, vmem_limit_bytes=V7X_VMEM_LIMIT)


def _resident(shape, index_map):
    return pl.BlockSpec(shape, index_map, pipeline_mode=pl.Buffered(1))


def _split2(v):
    hi = v.astype(BF16)
    lo = (v - hi.astype(F32)).astype(BF16)
    return hi, lo


def _dot(a, b):
    return jnp.dot(a, b, preferred_element_type=F32)


def _sigmoid(v):
    return 0.5 * jnp.tanh(0.5 * v) + 0.5


def _silu(v):
    return v * _sigmoid(v)


def _np_split2(m):
    m = jnp.asarray(np.asarray(m, np.float32))
    hi = m.astype(BF16)
    lo = (m - hi.astype(F32)).astype(BF16)
    return hi, lo


CAST_ROWS = 256
WIN_CHUNK = 1024


def _cast_kernel(w_ref, o_ref):
    o_ref[...] = w_ref[...].astype(BF16)


def _to_bf16(w):
    depth, k, n = w.shape
    return pl.pallas_call(
        _cast_kernel,
        grid=(depth, k // CAST_ROWS),
        in_specs=[pl.BlockSpec((1, CAST_ROWS, n), lambda l, i: (l, i, 0))],
        out_specs=pl.BlockSpec((1, CAST_ROWS, n), lambda l, i: (l, i, 0)),
        out_shape=jax.ShapeDtypeStruct((depth, k, n), BF16),
        compiler_params=_cparams(("parallel", "parallel")),
        name="cast_bf16",
    )(w)


_WIN_SEGMENTS = ((P_XBC, 0, XBC), (P_POOL, POOL_OFF, POOL_W), (P_Z, Z_OFF, D_SSD),
                 (P_GATE, GATE_OFF, 3 * D_MODEL), (P_FOUR, FOUR_OFF, FOUR_W))


def _win_src(j):
    src = jnp.int32(0)
    for dst0, src0, width in _WIN_SEGMENTS:
        first = dst0 // WIN_CHUNK
        inside = (j >= first) & (j < first + width // WIN_CHUNK)
        src = jnp.where(inside, src0 + (j - first) * WIN_CHUNK, src)
    return pl.multiple_of(src, 2 * SSD_HEADS)


def _winprep_kernel(wt_ref, o_ref):
    o_ref[0] = wt_ref[0].T.astype(BF16)


def _winprep(w_in):
    depth, d, n = w_in.shape
    wt = jnp.swapaxes(w_in, 1, 2)
    return pl.pallas_call(
        _winprep_kernel,
        grid=(depth, P_TOT // WIN_CHUNK),
        in_specs=[pl.BlockSpec((pl.Element(1), pl.Element(WIN_CHUNK), pl.Element(d)),
                               lambda l, j: (l, _win_src(j), 0))],
        out_specs=pl.BlockSpec((1, d, WIN_CHUNK), lambda l, j: (l, 0, j)),
        out_shape=jax.ShapeDtypeStruct((depth, d, P_TOT), BF16),
        compiler_params=_cparams(("parallel", "parallel")),
        name="winprep",
    )(wt)


def _adaln_kernel(c_ref, w_ref, b_ref, o_ref):
    s = _silu(c_ref[...])
    w = w_ref[0]
    sh, sl = _split2(s)
    wh, wl = _split2(w)
    o_ref[0] = _dot(sh, wh) + _dot(sl, wh) + _dot(sh, wl) + b_ref[0]


def _adaln(cc, w_ada, b_ada, tn=1024):
    depth, d, n = w_ada.shape
    return pl.pallas_call(
        _adaln_kernel,
        grid=(depth, n // tn),
        in_specs=[pl.BlockSpec((8, d), lambda l, j: (0, 0)),
                  pl.BlockSpec((1, d, tn), lambda l, j: (l, 0, j)),
                  pl.BlockSpec((1, 1, tn), lambda l, j: (l, 0, j))],
        out_specs=pl.BlockSpec((1, 8, tn), lambda l, j: (l, 0, j)),
        out_shape=jax.ShapeDtypeStruct((depth, 8, n), F32),
        compiler_params=_cparams(("parallel", "parallel")),
        name="adaln",
    )(cc, w_ada, b_ada.reshape(depth, 1, n))


def _norm_mod(x, nw, sc, sh):
    ms = jnp.mean(x * x, axis=-1, keepdims=True)
    return (x * lax.rsqrt(ms + EPS)) * nw * (1.0 + sc) + sh


def _inproj_kernel(x_ref, nw_ref, sc_ref, sh_ref, w_ref, wdt_ref, o_ref, dt_ref, h_ref):
    @pl.when(pl.program_id(2) == 0)
    def _():
        h = _norm_mod(x_ref[0], nw_ref[...], sc_ref[0], sh_ref[0]).astype(BF16)
        h_ref[...] = h
        dt_ref[0] = _dot(h, wdt_ref[0])

    o_ref[0] = _dot(h_ref[...], w_ref[0]).astype(o_ref.dtype)


def _inproj(x, nw, sc, sh, w, wdt, layer, n_out, tm, tn):
    b, l, d = x.shape
    tm = min(tm, l)
    return pl.pallas_call(
        _inproj_kernel,
        grid=(b, l // tm, n_out // tn),
        in_specs=[pl.BlockSpec((1, tm, d), lambda bi, i, j: (bi, i, 0)),
                  pl.BlockSpec((1, d), lambda bi, i, j: (0, 0)),
                  pl.BlockSpec((1, 1, d), lambda bi, i, j: (bi, 0, 0)),
                  pl.BlockSpec((1, 1, d), lambda bi, i, j: (bi, 0, 0)),
                  pl.BlockSpec((1, d, tn), lambda bi, i, j: (layer, 0, j)),
                  _resident((1, d, DT_PAD), lambda bi, i, j: (layer, 0, 0))],
        out_specs=[pl.BlockSpec((1, tm, tn), lambda bi, i, j: (bi, i, j)),
                   pl.BlockSpec((1, tm, DT_PAD), lambda bi, i, j: (bi, i, 0))],
        out_shape=[jax.ShapeDtypeStruct((b, l, n_out), BF16),
                   jax.ShapeDtypeStruct((b, l, DT_PAD), F32)],
        scratch_shapes=[pltpu.VMEM((tm, d), BF16)],
        compiler_params=_cparams(("parallel", "parallel", "arbitrary")),
        name="inproj",
    )(x, nw.reshape(1, d), sc, sh, w, wdt)


def _merge_kernel(d_ref, f_ref, yn_ref, gp_ref, gf_ref, gs_ref, x_ref, gm_ref,
                  wp_ref, ps_ref, wf_ref, ws_ref, wo_ref, o_ref):
    d = d_ref[0]
    y_pool = jnp.concatenate(
        [_dot(d[:, g * POOL_GC:(g + 1) * POOL_GC], wp_ref[0, g * POOL_GC:(g + 1) * POOL_GC, :])
         for g in range(POOL_GROUPS)], axis=1)
    m = _sigmoid(gp_ref[0].astype(F32)) * (y_pool * ps_ref[0])
    m = m + _sigmoid(gf_ref[0].astype(F32)) * _dot(f_ref[0], wf_ref[0])
    m = m + _sigmoid(gs_ref[0].astype(F32)) * _dot(yn_ref[0], ws_ref[0])
    o_ref[0] = x_ref[0] + gm_ref[0] * _dot(m.astype(BF16), wo_ref[0])


def _merge_out(d, f, yn, p, x, g_m, wts, layer, tm):
    b, l, dm = x.shape
    tm = min(tm, l)
    gblk = P_GATE // dm
    row = lambda bi, i: (bi, i, 0)
    lyr = lambda bi, i: (layer, 0, 0)
    return pl.pallas_call(
        _merge_kernel,
        grid=(b, l // tm),
        in_specs=[pl.BlockSpec((1, tm, POOL_W), row),
                  pl.BlockSpec((1, tm, FOUR_W), row),
                  pl.BlockSpec((1, tm, D_SSD), row),
                  pl.BlockSpec((1, tm, dm), lambda bi, i: (bi, i, gblk)),
                  pl.BlockSpec((1, tm, dm), lambda bi, i: (bi, i, gblk + 1)),
                  pl.BlockSpec((1, tm, dm), lambda bi, i: (bi, i, gblk + 2)),
                  pl.BlockSpec((1, tm, dm), row),
                  pl.BlockSpec((1, 1, dm), lambda bi, i: (bi, 0, 0)),
                  _resident((1, POOL_W, POOL_OUT), lyr),
                  _resident((1, 1, dm), lyr),
                  _resident((1, FOUR_W, dm), lyr),
                  _resident((1, D_SSD, dm), lyr),
                  _resident((1, dm, dm), lyr)],
        out_specs=pl.BlockSpec((1, tm, dm), row),
        out_shape=jax.ShapeDtypeStruct((b, l, dm), F32),
        compiler_params=_cparams(("parallel", "parallel")),
        name="merge_out",
    )(d, f, yn, p, p, p, x, g_m, wts["w_pool"], wts["pool_scale"], wts["w_fourier"], wts["w_ssd_out"],
      wts["w_out"])


def _ffn_kernel(x_ref, nw_ref, sc_ref, sh_ref, wg_ref, wu_ref, wd_ref, gate_ref, fnw_ref,
                o_ref, h_ref, acc_ref, *, final_norm):
    j = pl.program_id(2)

    @pl.when(j == 0)
    def _():
        h_ref[...] = _norm_mod(x_ref[0], nw_ref[...], sc_ref[0], sh_ref[0]).astype(BF16)
        acc_ref[...] = jnp.zeros_like(acc_ref)

    h = h_ref[...]
    a = _silu(_dot(h, wg_ref[0])) * _dot(h, wu_ref[0])
    acc_ref[...] += _dot(a.astype(BF16), wd_ref[0])

    @pl.when(j == pl.num_programs(2) - 1)
    def _():
        o = x_ref[0] + gate_ref[0] * acc_ref[...]
        if final_norm:
            ms = jnp.mean(o * o, axis=-1, keepdims=True)
            o = (o * lax.rsqrt(ms + EPS)) * fnw_ref[...]
        o_ref[0] = o


def _ffn(x, nw, sc, sh, wg, wu, wd, layer, gate, fnw, final_norm, tm, tf):
    b, l, d = x.shape
    ff = wg.shape[2]
    tm = min(tm, l)
    vec = lambda bi, i, j: (bi, 0, 0)
    return pl.pallas_call(
        functools.partial(_ffn_kernel, final_norm=final_norm),
        grid=(b, l // tm, ff // tf),
        in_specs=[pl.BlockSpec((1, tm, d), lambda bi, i, j: (bi, i, 0)),
                  pl.BlockSpec((1, d), lambda bi, i, j: (0, 0)),
                  pl.BlockSpec((1, 1, d), vec),
                  pl.BlockSpec((1, 1, d), vec),
                  pl.BlockSpec((1, d, tf), lambda bi, i, j: (layer, 0, j)),
                  pl.BlockSpec((1, d, tf), lambda bi, i, j: (layer, 0, j)),
                  pl.BlockSpec((1, tf, d), lambda bi, i, j: (layer, j, 0)),
                  pl.BlockSpec((1, 1, d), vec),
                  pl.BlockSpec((1, d), lambda bi, i, j: (0, 0))],
        out_specs=pl.BlockSpec((1, tm, d), lambda bi, i, j: (bi, i, 0)),
        out_shape=jax.ShapeDtypeStruct((b, l, d), F32),
        scratch_shapes=[pltpu.VMEM((tm, d), BF16), pltpu.VMEM((tm, d), F32)],
        compiler_params=_cparams(("parallel", "parallel", "arbitrary")),
        name="ffn",
    )(x, nw.reshape(1, d), sc, sh, wg, wu, wd, gate, fnw.reshape(1, d))


def _box_matrix(n, w):
    idx = np.arange(n)
    lo = np.clip(idx - w // 2, 0, n)
    hi = np.clip(idx + (w - w // 2), 0, n)
    m = ((idx[None, :] >= lo[:, None]) & (idx[None, :] < hi[:, None])).astype(np.float64)
    return m / (hi - lo)[:, None]


POOL_TB = 256
POOL_PAD = 16


def _pool_kernel(u_ref, mh_ref, ml_ref, o_ref, *scratch, grid_rows):
    g = pl.program_id(1)
    l = u_ref.shape[1]
    mh = mh_ref[0]
    ml = ml_ref[0]
    if grid_rows is None:
        v = u_ref[0]
        o_ref[0] = (_dot(mh, v) + _dot(ml, v) - v.astype(F32)).astype(o_ref.dtype)
        return

    cp_ref, = scratch
    pad = POOL_PAD * GRID_W
    cp_ref[0:pad, :] = jnp.zeros((pad, POOL_GC), F32)
    cp_ref[pad + l:pad + l + pad, :] = jnp.zeros((pad, POOL_GC), F32)
    for i in range(l // POOL_TB):
        v = u_ref[0, i * POOL_TB:(i + 1) * POOL_TB, :]
        cp_ref[pad + i * POOL_TB:pad + (i + 1) * POOL_TB, :] = _dot(mh, v) + _dot(ml, v)

    for gi, w in enumerate(POOL_WINDOWS):
        lo_off, hi_off = -(w // 2), w - w // 2

        @pl.when(g == gi)
        def _(lo_off=lo_off, hi_off=hi_off):
            def body(r, carry):
                s = jnp.zeros((GRID_W, POOL_GC), F32)
                for o in range(lo_off, hi_off):
                    start = pl.multiple_of(pad + (r + o) * GRID_W, GRID_W)
                    s = s + cp_ref[pl.ds(start, GRID_W), :]
                cnt = jnp.minimum(r + hi_off, grid_rows) - jnp.maximum(r + lo_off, 0)
                cntv = jnp.full((GRID_W, POOL_GC), cnt, jnp.int32).astype(F32)
                t0 = pl.multiple_of(r * GRID_W, GRID_W)
                v = u_ref[0, pl.ds(t0, GRID_W), :].astype(F32)
                o_ref[0, pl.ds(t0, GRID_W), :] = (s / cntv - v).astype(o_ref.dtype)
                return carry

            lax.fori_loop(0, grid_rows, body, 0)


def _pool_diff(p, grid_rows):
    b, l, _ = p.shape
    mats = []
    for w in POOL_WINDOWS:
        if grid_rows is None:
            mats.append(_box_matrix(l, w))
        else:
            mats.append(np.kron(np.eye(POOL_TB // GRID_W), _box_matrix(GRID_W, w)))
    mh, ml = _np_split2(np.stack(mats))
    tb = mats[0].shape[0]
    scratch = []
    if grid_rows is not None:
        scratch = [pltpu.VMEM((l + 2 * POOL_PAD * GRID_W, POOL_GC), F32)]
    cblk = P_POOL // POOL_GC
    return pl.pallas_call(
        functools.partial(_pool_kernel, grid_rows=grid_rows),
        grid=(b, POOL_GROUPS),
        in_specs=[pl.BlockSpec((1, l, POOL_GC), lambda bi, g: (bi, 0, cblk + g)),
                  pl.BlockSpec((1, tb, tb), lambda bi, g: (g, 0, 0)),
                  pl.BlockSpec((1, tb, tb), lambda bi, g: (g, 0, 0))],
        out_specs=pl.BlockSpec((1, l, POOL_GC), lambda bi, g: (bi, 0, g)),
        out_shape=jax.ShapeDtypeStruct((b, l, POOL_W), BF16),
        scratch_shapes=scratch,
        compiler_params=_cparams(("parallel", "parallel")),
        name="pool_diff",
    )(p, mh, ml)


def _chdft_fold_kernel(u_ref, ua_ref, ub_ref, w_ref, flip_ref, o_ref):
    i = pl.program_id(2)
    c = FOUR_GC
    tm = u_ref.shape[1]
    w = w_ref[...]
    urev = _dot(flip_ref[...], ua_ref[0])
    row = lax.broadcasted_iota(jnp.int32, (tm, c), 0)
    urev = jnp.where(row == 0, jnp.broadcast_to(ub_ref[0, 0:1, :].astype(F32), (tm, c)), urev).astype(BF16)
    z = _dot(u_ref[0], w)
    zrev = _dot(urev, w)
    first = (row + jnp.minimum(i, 1)) == 0
    er = z[:, :c] + jnp.where(first, 0.0, zrev[:, :c])
    oi = jnp.where(first, zrev[:, :c], z[:, c:] - zrev[:, c:])
    o_ref[0, 0, :, :c] = er.astype(o_ref.dtype)
    o_ref[0, 0, :, c:] = oi.astype(o_ref.dtype)


def _chdft_fold(p, tm):
    b, l, _ = p.shape
    half = l // 2
    tm = min(tm, half)
    nb = l // tm
    k = np.arange(FOUR_GC)
    ang = -2.0 * np.pi * ((k[:, None] * k[None, :]) % FOUR_GC) / FOUR_GC
    w = jnp.asarray(np.concatenate([np.cos(ang), np.sin(ang)], axis=1).astype(np.float32)).astype(BF16)
    flip = np.zeros((tm, tm), np.float32)
    flip[np.arange(1, tm), tm - np.arange(1, tm)] = 1.0
    cblk = P_FOUR // FOUR_GC
    return pl.pallas_call(
        _chdft_fold_kernel,
        grid=(b, FOUR_GROUPS, half // tm),
        in_specs=[pl.BlockSpec((1, tm, FOUR_GC), lambda bi, g, i: (bi, i, cblk + g)),
                  pl.BlockSpec((1, tm, FOUR_GC), lambda bi, g, i: (bi, nb - 1 - i, cblk + g)),
                  pl.BlockSpec((1, tm, FOUR_GC), lambda bi, g, i: (bi, jnp.where(i == 0, nb // 2, nb - i), cblk + g)),
                  _resident((FOUR_GC, 2 * FOUR_GC), lambda bi, g, i: (0, 0)),
                  _resident((tm, tm), lambda bi, g, i: (0, 0))],
        out_specs=pl.BlockSpec((1, 1, tm, 2 * FOUR_GC), lambda bi, g, i: (bi, g, i, 0)),
        out_shape=jax.ShapeDtypeStruct((b, FOUR_GROUPS, half, 2 * FOUR_GC), BF16),
        compiler_params=_cparams(("parallel", "parallel", "parallel")),
        name="chdft",
    )(p, p, p, w, jnp.asarray(flip).astype(BF16))


def _seq_dft_tables(l):
    half = l // 2
    r = int(round(math.sqrt(l)))
    assert r * r == l
    col = np.arange(half)
    hi = 2.0 * np.pi * ((np.arange(r)[:, None] * r * col[None, :]) % l) / l
    lo = 2.0 * np.pi * ((np.arange(r)[:, None] * col[None, :]) % l) / l
    ch, sh = [jnp.asarray(f(hi).astype(np.float32))[:, None, :] for f in (np.cos, np.sin)]
    cl, sl = [jnp.asarray(f(lo).astype(np.float32))[None, :, :] for f in (np.cos, np.sin)]
    cos = (ch * cl - sh * sl).reshape(l, half)
    sin = (sh * cl + ch * sl).reshape(l, half)
    nyq = jnp.asarray((1.0 - 2.0 * (np.arange(l) % 2)).astype(np.float32))[:, None]
    sin = jnp.where(jnp.asarray(col)[None, :] == 0, nyq, sin)
    return cos.astype(BF16), sin.astype(BF16)


def _seqdft_kernel(c_ref, s_ref, e_ref, o_ref, *, scale):
    c = FOUR_GC
    e = e_ref[0, 0]
    y = _dot(c_ref[...], e[:, :c]) + _dot(s_ref[...], e[:, c:])
    o_ref[0] = (y * scale).astype(o_ref.dtype)


def _fourier(p, tm):
    b, l, _ = p.shape
    half = l // 2
    tm = min(tm, l)
    e = _chdft_fold(p, 512)
    cos, sin = _seq_dft_tables(l)
    scale = 1.0 / math.sqrt(l * FOUR_GC)
    return pl.pallas_call(
        functools.partial(_seqdft_kernel, scale=scale),
        grid=(l // tm, b, FOUR_GROUPS),
        in_specs=[pl.BlockSpec((tm, half), lambda i, bi, g: (i, 0)),
                  pl.BlockSpec((tm, half), lambda i, bi, g: (i, 0)),
                  pl.BlockSpec((1, 1, half, 2 * FOUR_GC), lambda i, bi, g: (bi, g, 0, 0))],
        out_specs=pl.BlockSpec((1, tm, FOUR_GC), lambda i, bi, g: (bi, i, g)),
        out_shape=jax.ShapeDtypeStruct((b, l, FOUR_W), BF16),
        compiler_params=_cparams(("parallel", "parallel", "parallel")),
        name="seqdft",
    )(cos, sin, e)


CONV_CB = 512
_SHIFT_TAPS = tuple(k for k in range(CONV_W) if k != CONV_W // 2)


def _shift_matrix():
    m = np.zeros((len(_SHIFT_TAPS) * CHUNK, CHUNK + 2 * HALO), np.float32)
    t = np.arange(CHUNK)
    for q, k in enumerate(_SHIFT_TAPS):
        m[q * CHUNK + t, HALO - CONV_W // 2 + k + t] = 1.0
    return jnp.asarray(m).astype(BF16)


def _ssd_kernel(*refs, fwd, nc, epilogue):
    if fwd:
        xc_ref, dt_ref, dtb_ref, alog_ref, e_ref, h0_ref = refs[:6]
        rest = refs[6:]
        if epilogue:
            z_ref, yb_ref, dsk_ref, nw_ref, y_ref, hout_ref, st_ref = rest
        else:
            y_ref, hout_ref, st_ref = rest
    else:
        (xm_ref, xp_ref, xn_ref, dt_ref, cw_ref, cb_ref, shift_ref, dtb_ref, alog_ref, e_ref, h0_ref,
         y_ref, xc_ref, hout_ref, st_ref, ext_ref) = refs
    c = pl.program_id(0)
    cc = c if fwd else nc - 1 - c
    T = CHUNK
    nb = dt_ref.shape[0]

    @pl.when(c == 0)
    def _():
        st_ref[...] = h0_ref[...]

    if not fwd:
        @pl.when(cc > 0)
        def _():
            ext_ref[:, 0:HALO, :] = xp_ref[...]

        @pl.when(cc == 0)
        def _():
            ext_ref[:, 0:HALO, :] = jnp.zeros((nb, HALO, XBC), ext_ref.dtype)

        ext_ref[:, HALO:HALO + T, :] = xm_ref[...]

        @pl.when(cc < nc - 1)
        def _():
            ext_ref[:, HALO + T:HALO + T + HALO, :] = xn_ref[...]

        @pl.when(cc == nc - 1)
        def _():
            ext_ref[:, HALO + T:HALO + T + HALO, :] = jnp.zeros((nb, HALO, XBC), ext_ref.dtype)

        shift = shift_ref[...]
        for j in range(XBC // CONV_CB):
            cs = slice(j * CONV_CB, (j + 1) * CONV_CB)
            for bi in range(nb):
                sh = _dot(shift, ext_ref[bi, :, cs])
                acc = cb_ref[:, cs] + xm_ref[bi, :, cs].astype(F32) * cw_ref[CONV_W // 2:CONV_W // 2 + 1, cs]
                for q, k in enumerate(_SHIFT_TAPS):
                    acc = acc + sh[q * T:(q + 1) * T] * cw_ref[k:k + 1, cs]
                xc_ref[bi, :, cs] = _silu(acc).astype(xc_ref.dtype)

    row = lax.broadcasted_iota(jnp.int32, (T, T), 0)
    col = lax.broadcasted_iota(jnp.int32, (T, T), 1)
    causal = (col <= row) if fwd else (col >= row)
    tri = jnp.where(causal, 1.0, 0.0).astype(BF16)
    lane = lax.broadcasted_iota(jnp.int32, (T, 2 * HEAD_DIM), 1)
    hbase = 0 if fwd else SSD_HEADS
    e = e_ref[...]
    a = -jnp.exp(alog_ref[...])

    def head_factors(bi):
        xdt = dt_ref[bi] + dtb_ref[...]
        dt = jnp.maximum(xdt, 0.0) + jnp.log1p(jnp.exp(-jnp.abs(xdt)))
        da = dt * a
        d1 = da.astype(BF16)
        r1 = da - d1.astype(F32)
        d2 = r1.astype(BF16)
        d3 = (r1 - d2.astype(F32)).astype(BF16)
        acs = _dot(tri, d1) + _dot(tri, d2) + _dot(tri, d3)
        tot = acs[T - 1:T, :] if fwd else acs[0:1, :]
        eh, el = _split2(jnp.broadcast_to(jnp.exp(tot), (16, T)))
        ex = _dot(jnp.concatenate([jnp.exp(acs).astype(BF16), (dt * jnp.exp(tot - acs)).astype(BF16), eh, el],
                                  axis=0), e)
        return dict(acs=acs, arow_t=(acs - jnp.log(dt)).T, eacs_x=ex[0:T], w2=ex[T:2 * T],
                    etot_x=ex[2 * T:2 * T + 1] + ex[2 * T + 16:2 * T + 17])

    hf = [head_factors(bi) for bi in range(nb)]

    for g in range(SSD_GROUPS):
        gs = slice(g * GROUP_W, (g + 1) * GROUP_W)
        bs = slice(D_SSD + g * D_STATE, D_SSD + (g + 1) * D_STATE)
        cs_ = slice(D_SSD + (SSD_GROUPS + g) * D_STATE, D_SSD + (SSD_GROUPS + g + 1) * D_STATE)
        for bi in range(nb):
            f = hf[bi]
            bb = xc_ref[bi, :, bs]
            cbf = xc_ref[bi, :, cs_]
            cb = lax.dot_general(cbf, bb, (((1,), (1,)), ((), ())), preferred_element_type=F32)
            st = st_ref[bi, g]
            yoff = _dot(cbf, st.astype(BF16)) * f["eacs_x"][:, gs]
            xb = xc_ref[bi, :, gs]
            ys = []
            for jp in range(HEADS_PER_GROUP // 2):
                ms = []
                for jj in range(2):
                    hc = hbase + g * HEADS_PER_GROUP + 2 * jp + jj
                    diff = f["acs"][:, hc:hc + 1] - f["arow_t"][hc:hc + 1, :]
                    ms.append((cb * jnp.exp(jnp.where(causal, diff, -1e30))).astype(BF16))
                r = _dot(jnp.concatenate(ms, axis=0), xb[:, jp * 2 * HEAD_DIM:(jp + 1) * 2 * HEAD_DIM])
                ys.append(jnp.where(lane < HEAD_DIM, r[:T], r[T:]))
            yg = jnp.concatenate(ys, axis=1) + yoff

            xg = xb.astype(F32)
            xds = (xg * f["w2"][:, gs]).astype(BF16)
            st_ref[bi, g] = st * f["etot_x"][:, gs] + _dot(bb.astype(F32).T.astype(BF16), xds)

            if epilogue:
                yt = yg + yb_ref[bi, :, gs] + dsk_ref[:, gs] * xg
                v = yt * _silu(z_ref[bi, :, gs].astype(F32))
                ms_ = jnp.mean(v * v, axis=-1, keepdims=True)
                y_ref[bi, :, gs] = ((v * lax.rsqrt(ms_ + EPS)) * nw_ref[:, gs]).astype(y_ref.dtype)
            else:
                y_ref[bi, :, gs] = yg

    @pl.when(c == nc - 1)
    def _():
        hout_ref[...] = st_ref[...]


_STATE_TAIL = (SSD_GROUPS, D_STATE, GROUP_W)


def _ssd_bwd(p, pdt, lw, h0):
    b, l, _ = p.shape
    nc = l // CHUNK
    nh = l // HALO
    per = CHUNK // HALO
    cidx = lambda c: nc - 1 - c
    chunk = lambda c: (0, cidx(c), 0)
    const2 = lambda c: (0, 0)
    state = lambda c: (0, 0, 0, 0)
    state_block = (b,) + _STATE_TAIL
    return pl.pallas_call(
        functools.partial(_ssd_kernel, fwd=False, nc=nc, epilogue=False),
        grid=(nc,),
        in_specs=[pl.BlockSpec((b, CHUNK, XBC), chunk),
                  pl.BlockSpec((b, HALO, XBC), lambda c: (0, jnp.maximum(cidx(c) * per - 1, 0), 0)),
                  pl.BlockSpec((b, HALO, XBC), lambda c: (0, jnp.minimum((cidx(c) + 1) * per, nh - 1), 0)),
                  pl.BlockSpec((b, CHUNK, DT_PAD), chunk),
                  _resident((8, XBC), const2),
                  _resident((1, XBC), const2),
                  _resident((len(_SHIFT_TAPS) * CHUNK, CHUNK + 2 * HALO), const2),
                  _resident((1, DT_PAD), const2),
                  _resident((1, DT_PAD), const2),
                  _resident((DT_PAD, D_SSD), const2),
                  _resident(state_block, state)],
        out_specs=[pl.BlockSpec((b, CHUNK, D_SSD), chunk),
                   pl.BlockSpec((b, CHUNK, XBC), chunk),
                   pl.BlockSpec(state_block, state)],
        out_shape=[jax.ShapeDtypeStruct((b, l, D_SSD), F32),
                   jax.ShapeDtypeStruct((b, l, XBC), BF16),
                   jax.ShapeDtypeStruct(state_block, F32)],
        scratch_shapes=[pltpu.VMEM(state_block, F32),
                        pltpu.VMEM((b, CHUNK + 2 * HALO, XBC), BF16)],
        compiler_params=_cparams(("arbitrary",)),
        name="ssd_bwd",
    )(p, p, p, pdt, lw["conv_w"], lw["conv_b"], _shift_matrix(), lw["dt_bias"], lw["a_log"], lw["expand_b"], h0)


def _ssd_fwd(xc, pdt, lw, h0, p=None, yb=None):
    b, l, _ = xc.shape
    nc = l // CHUNK
    chunk = lambda c: (0, c, 0)
    const2 = lambda c: (0, 0)
    state = lambda c: (0, 0, 0, 0)
    state_block = (b,) + _STATE_TAIL
    epilogue = p is not None
    in_specs = [pl.BlockSpec((b, CHUNK, XBC), chunk),
                pl.BlockSpec((b, CHUNK, DT_PAD), chunk),
                _resident((1, DT_PAD), const2),
                _resident((1, DT_PAD), const2),
                _resident((DT_PAD, D_SSD), const2),
                _resident(state_block, state)]
    args = [xc, pdt, lw["dt_bias"], lw["a_log"], lw["expand_f"], h0]
    if epilogue:
        z_blk = P_Z // D_SSD
        in_specs += [pl.BlockSpec((b, CHUNK, D_SSD), lambda c: (0, c, z_blk)),
                     pl.BlockSpec((b, CHUNK, D_SSD), chunk),
                     _resident((1, D_SSD), const2),
                     _resident((1, D_SSD), const2)]
        args += [p, yb, lw["d_skip"], lw["ssd_norm_w"]]
    return pl.pallas_call(
        functools.partial(_ssd_kernel, fwd=True, nc=nc, epilogue=epilogue),
        grid=(nc,),
        in_specs=in_specs,
        out_specs=[pl.BlockSpec((b, CHUNK, D_SSD), chunk),
                   pl.BlockSpec(state_block, state)],
        out_shape=[jax.ShapeDtypeStruct((b, l, D_SSD), BF16 if epilogue else F32),
                   jax.ShapeDtypeStruct(state_block, F32)],
        scratch_shapes=[pltpu.VMEM(state_block, F32)],
        compiler_params=_cparams(("arbitrary",)),
        name="ssd_fwd",
    )(*args)


def _ssd(p, pdt, lw, h0f, h0b):
    yb, xc, hb = _ssd_bwd(p, pdt, lw, h0b)
    y, hf = _ssd_fwd(xc, pdt, lw, h0f, p, yb)
    return y, hf, hb


def _expand_matrix(offset):
    e = np.zeros((DT_PAD, D_SSD), np.float32)
    for h in range(SSD_HEADS):
        e[offset + h, h * HEAD_DIM:(h + 1) * HEAD_DIM] = 1.0
    return jnp.asarray(e).astype(BF16)


def _layer_weights(l, conv_w, conv_b, a_log, dt_bias, d_skip, ssd_norm_w):
    pad_h = (0, DT_PAD - 2 * SSD_HEADS)
    return {
        "conv_w": jnp.pad(conv_w[l], ((0, 8 - CONV_W), (0, 0))),
        "conv_b": conv_b[l].reshape(1, XBC),
        "dt_bias": jnp.pad(dt_bias[l].reshape(-1), pad_h).reshape(1, DT_PAD),
        "a_log": jnp.pad(a_log[l].reshape(-1), pad_h).reshape(1, DT_PAD),
        "d_skip": jnp.repeat(d_skip[l], HEAD_DIM).reshape(1, D_SSD),
        "ssd_norm_w": ssd_norm_w[l].reshape(1, D_SSD),
        "expand_f": _expand_matrix(0),
        "expand_b": _expand_matrix(SSD_HEADS),
    }


def _mixer(x, nw, sc, sh, g_m, wts, lw, layer, grid_rows, h0f, h0b):
    b, l, dm = x.shape
    tok = (lambda a: a.reshape(1, b * l, a.shape[-1])) if sc.shape[0] == 1 else (lambda a: a)
    seq = lambda a: a.reshape(b, l, a.shape[-1])
    p, pdt = _inproj(tok(x), nw, sc, sh, wts["w_main"], wts["w_dt"], layer, P_TOT, 1024, 1024)
    y_n, hf, hb = _ssd(seq(p), seq(pdt), lw, h0f, h0b)
    d = _pool_diff(seq(p), grid_rows)
    f = _fourier(seq(p), 2048)
    x = seq(_merge_out(tok(d), tok(f), tok(y_n), p, tok(x), g_m, wts, layer, 256))
    return x, hf, hb


def kernel(x, c, ctx, c_ctx, w_ada, b_ada, norm_mix_w, norm_ffn_w, w_in, conv_w, conv_b, a_log, dt_bias,
           d_skip, ssd_norm_w, w_ssd_out, w_pool, pool_scale, w_fourier, w_out, w_ffn_gate, w_ffn_up,
           w_ffn_down, final_norm_w):
    b, seq, d = x.shape
    rows = seq // GRID_W
    cc = jnp.concatenate([c, c_ctx[None, :], jnp.zeros((8 - b - 1, d), F32)], axis=0)
    mod = _adaln(cc, w_ada, b_ada)
    h0 = jnp.zeros((b,) + _STATE_TAIL, F32)
    lc = ctx.shape[1]
    w_main = _winprep(w_in)
    w_dt = jnp.pad(w_in[:, :, XBC:SSD_IN], ((0, 0), (0, 0), (0, DT_PAD - 2 * SSD_HEADS))).astype(BF16)
    wts = {
        "w_main": w_main,
        "w_dt": w_dt,
        "w_pool": _to_bf16(w_pool.reshape(DEPTH, POOL_W, POOL_OUT)),
        "pool_scale": pool_scale.reshape(DEPTH, 1, D_MODEL),
        "w_fourier": _to_bf16(w_fourier),
        "w_ssd_out": _to_bf16(w_ssd_out),
        "w_out": _to_bf16(w_out),
    }
    wg, wu, wd = _to_bf16(w_ffn_gate), _to_bf16(w_ffn_up), _to_bf16(w_ffn_down)
    for l in range(DEPTH):
        last = l == DEPTH - 1
        lw = _layer_weights(l, conv_w, conv_b, a_log, dt_bias, d_skip, ssd_norm_w)
        sh_m, sc_m, g_m, sh_f, sc_f, g_f = [mod[l, :b, i * d:(i + 1) * d].reshape(b, 1, d) for i in range(6)]
        csh_m, csc_m, cg_m, csh_f, csc_f, cg_f = [mod[l, b, i * d:(i + 1) * d].reshape(1, 1, d) for i in range(6)]

        if last:
            pc, pdtc = _inproj(ctx.reshape(1, b * lc, d), norm_mix_w[l], csc_m, csh_m, w_main, w_dt, l,
                               XBC, 1024, 1024)
            pdtc = pdtc.reshape(b, lc, DT_PAD)
            _, xcc, hb = _ssd_bwd(pc.reshape(b, lc, XBC), pdtc, lw, h0)
            _, hf = _ssd_fwd(xcc, pdtc, lw, h0)
        else:
            ctx, hf, hb = _mixer(ctx, norm_mix_w[l], csc_m, csh_m, cg_m, wts, lw, l, None, h0, h0)
            ctx = _ffn(ctx.reshape(1, b * lc, d), norm_ffn_w[l], csc_f, csh_f, wg, wu, wd, l,
                       cg_f, final_norm_w, False, 512, 512).reshape(b, lc, d)

        x, _, _ = _mixer(x, norm_mix_w[l], sc_m, sh_m, g_m, wts, lw, l, rows, hf, hb)
        x = _ffn(x, norm_ffn_w[l], sc_f, sh_f, wg, wu, wd, l, g_f, final_norm_w, last, 512, 512)
    return x
```

```python
import functools
import math

import numpy as np
import jax
import jax.numpy as jnp
from jax import lax
from jax.experimental import pallas as pl
from jax.experimental.pallas import tpu as pltpu

F32 = jnp.float32
BF16 = jnp.bfloat16

D_MODEL = 2048
DEPTH = 2
GRID_W = 64
EPS = 1e-6

POOL_GROUPS = 4
POOL_WINDOWS = (2, 4, 8, 16)
POOL_W = D_MODEL // 2
POOL_GC = POOL_W // POOL_GROUPS
POOL_OUT = D_MODEL // POOL_GROUPS

FOUR_GROUPS = 4
FOUR_W = D_MODEL // 2
FOUR_GC = FOUR_W // FOUR_GROUPS

D_SSD = D_MODEL
HEAD_DIM = 64
SSD_HEADS = D_SSD // HEAD_DIM
SSD_GROUPS = 4
HEADS_PER_GROUP = SSD_HEADS // SSD_GROUPS
GROUP_W = HEADS_PER_GROUP * HEAD_DIM
D_STATE = 128
CONV_W = 5
CHUNK = 128

D_FF = ((8 * D_MODEL // 3 + 255) // 256) * 256

XBC = D_SSD + 2 * SSD_GROUPS * D_STATE
SSD_IN = XBC + 2 * SSD_HEADS
Z_OFF = SSD_IN
POOL_OFF = Z_OFF + D_SSD
FOUR_OFF = POOL_OFF + POOL_W
GATE_OFF = FOUR_OFF + FOUR_W
N_IN = GATE_OFF + 3 * D_MODEL

P_XBC = 0
P_POOL = XBC
P_Z = P_POOL + POOL_W
P_GATE = P_Z + D_SSD
P_FOUR = P_GATE + 3 * D_MODEL
P_TOT = P_FOUR + FOUR_W
DT_PAD = 128

V7X_VMEM_LIMIT = 56 * 1024 * 1024
HALO = 16


def _cparams(sem):
    return pltpu.CompilerParams(dimension_semantics=sem, vmem_limit_bytes=V7X_VMEM_LIMIT)


def _resident(shape, index_map):
    return pl.BlockSpec(shape, index_map, pipeline_mode=pl.Buffered(1))


def _split2(v):
    hi = v.astype(BF16)
    lo = (v - hi.astype(F32)).astype(BF16)
    return hi, lo


def _dot(a, b):
    return jnp.dot(a, b, preferred_element_type=F32)


def _sigmoid(v):
    return 0.5 * jnp.tanh(0.5 * v) + 0.5


def _silu(v):
    return v * _sigmoid(v)


def _np_split2(m):
    m = jnp.asarray(np.asarray(m, np.float32))
    hi = m.astype(BF16)
    lo = (m - hi.astype(F32)).astype(BF16)
    return hi, lo


CAST_ROWS = 256
WIN_CHUNK = 1024


def _cast_kernel(w_ref, o_ref):
    o_ref[...] = w_ref[...].astype(BF16)


def _to_bf16(w):
    depth, k, n = w.shape
    return pl.pallas_call(
        _cast_kernel,
        grid=(depth, k // CAST_ROWS),
        in_specs=[pl.BlockSpec((1, CAST_ROWS, n), lambda l, i: (l, i, 0))],
        out_specs=pl.BlockSpec((1, CAST_ROWS, n), lambda l, i: (l, i, 0)),
        out_shape=jax.ShapeDtypeStruct((depth, k, n), BF16),
        compiler_params=_cparams(("parallel", "parallel")),
        name="cast_bf16",
    )(w)


_WIN_SEGMENTS = ((P_XBC, 0, XBC), (P_POOL, POOL_OFF, POOL_W), (P_Z, Z_OFF, D_SSD),
                 (P_GATE, GATE_OFF, 3 * D_MODEL), (P_FOUR, FOUR_OFF, FOUR_W))


def _win_src(j):
    src = jnp.int32(0)
    for dst0, src0, width in _WIN_SEGMENTS:
        first = dst0 // WIN_CHUNK
        inside = (j >= first) & (j < first + width // WIN_CHUNK)
        src = jnp.where(inside, src0 + (j - first) * WIN_CHUNK, src)
    return pl.multiple_of(src, 2 * SSD_HEADS)


def _winprep_kernel(wt_ref, o_ref):
    o_ref[0] = wt_ref[0].T.astype(BF16)


def _winprep(w_in):
    depth, d, n = w_in.shape
    wt = jnp.swapaxes(w_in, 1, 2)
    return pl.pallas_call(
        _winprep_kernel,
        grid=(depth, P_TOT // WIN_CHUNK),
        in_specs=[pl.BlockSpec((pl.Element(1), pl.Element(WIN_CHUNK), pl.Element(d)),
                               lambda l, j: (l, _win_src(j), 0))],
        out_specs=pl.BlockSpec((1, d, WIN_CHUNK), lambda l, j: (l, 0, j)),
        out_shape=jax.ShapeDtypeStruct((depth, d, P_TOT), BF16),
        compiler_params=_cparams(("parallel", "parallel")),
        name="winprep",
    )(wt)


ADA_ROWS = 8


def _adaln_kernel(ct_ref, w_ref, b_ref, o_ref, *, n_rows):
    s = _silu(ct_ref[...])
    w = w_ref[0]
    tn = w.shape[1]
    rows = [jnp.sum(w * s[:, r:r + 1], axis=0, keepdims=True) for r in range(n_rows)]
    rows.append(jnp.zeros((ADA_ROWS - n_rows, tn), F32))
    o_ref[0] = jnp.concatenate(rows, axis=0) + b_ref[0]


def _adaln(cc, w_ada, b_ada, tn=1024):
    depth, d, n = w_ada.shape
    n_rows = cc.shape[0]
    ct = jnp.pad(cc, ((0, ADA_ROWS - n_rows), (0, 0))).T
    return pl.pallas_call(
        functools.partial(_adaln_kernel, n_rows=n_rows),
        grid=(depth, n // tn),
        in_specs=[pl.BlockSpec((d, ADA_ROWS), lambda l, j: (0, 0)),
                  pl.BlockSpec((1, d, tn), lambda l, j: (l, 0, j)),
                  pl.BlockSpec((1, 1, tn), lambda l, j: (l, 0, j))],
        out_specs=pl.BlockSpec((1, ADA_ROWS, tn), lambda l, j: (l, 0, j)),
        out_shape=jax.ShapeDtypeStruct((depth, ADA_ROWS, n), F32),
        compiler_params=_cparams(("parallel", "parallel")),
        name="adaln",
    )(ct, w_ada, b_ada.reshape(depth, 1, n))


def _norm_mod(x, nw, sc, sh):
    ms = jnp.mean(x * x, axis=-1, keepdims=True)
    return (x * lax.rsqrt(ms + EPS)) * nw * (1.0 + sc) + sh


def _inproj_kernel(x_ref, nw_ref, sc_ref, sh_ref, w_ref, wdt_ref, o_ref, dt_ref, h_ref):
    @pl.when(pl.program_id(2) == 0)
    def _():
        h = _norm_mod(x_ref[0], nw_ref[...], sc_ref[0], sh_ref[0]).astype(BF16)
        h_ref[...] = h
        dt_ref[0] = _dot(h, wdt_ref[0])

    o_ref[0] = _dot(h_ref[...], w_ref[0]).astype(o_ref.dtype)


def _inproj(x, nw, sc, sh, w, wdt, layer, n_out, tm, tn):
    b, l, d = x.shape
    tm = min(tm, l)
    return pl.pallas_call(
        _inproj_kernel,
        grid=(b, l // tm, n_out // tn),
        in_specs=[pl.BlockSpec((1, tm, d), lambda bi, i, j: (bi, i, 0)),
                  pl.BlockSpec((1, d), lambda bi, i, j: (0, 0)),
                  pl.BlockSpec((1, 1, d), lambda bi, i, j: (bi, 0, 0)),
                  pl.BlockSpec((1, 1, d), lambda bi, i, j: (bi, 0, 0)),
                  pl.BlockSpec((1, d, tn), lambda bi, i, j: (layer, 0, j)),
                  _resident((1, d, DT_PAD), lambda bi, i, j: (layer, 0, 0))],
        out_specs=[pl.BlockSpec((1, tm, tn), lambda bi, i, j: (bi, i, j)),
                   pl.BlockSpec((1, tm, DT_PAD), lambda bi, i, j: (bi, i, 0))],
        out_shape=[jax.ShapeDtypeStruct((b, l, n_out), BF16),
                   jax.ShapeDtypeStruct((b, l, DT_PAD), F32)],
        scratch_shapes=[pltpu.VMEM((tm, d), BF16)],
        compiler_params=_cparams(("parallel", "parallel", "arbitrary")),
        name="inproj",
    )(x, nw.reshape(1, d), sc, sh, w, wdt)


def _merge_kernel(d_ref, f_ref, yn_ref, gp_ref, gf_ref, gs_ref, x_ref, gm_ref,
                  wp_ref, ps_ref, wf_ref, ws_ref, wo_ref, o_ref):
    d = d_ref[0]
    y_pool = jnp.concatenate(
        [_dot(d[:, g * POOL_GC:(g + 1) * POOL_GC], wp_ref[0, g * POOL_GC:(g + 1) * POOL_GC, :])
         for g in range(POOL_GROUPS)], axis=1)
    m = _sigmoid(gp_ref[0].astype(F32)) * (y_pool * ps_ref[0])
    m = m + _sigmoid(gf_ref[0].astype(F32)) * _dot(f_ref[0], wf_ref[0])
    m = m + _sigmoid(gs_ref[0].astype(F32)) * _dot(yn_ref[0], ws_ref[0])
    o_ref[0] = x_ref[0] + gm_ref[0] * _dot(m.astype(BF16), wo_ref[0])


def _merge_out(d, f, yn, p, x, g_m, wts, layer, tm):
    b, l, dm = x.shape
    tm = min(tm, l)
    gblk = P_GATE // dm
    row = lambda bi, i: (bi, i, 0)
    lyr = lambda bi, i: (layer, 0, 0)
    return pl.pallas_call(
        _merge_kernel,
        grid=(b, l // tm),
        in_specs=[pl.BlockSpec((1, tm, POOL_W), row),
                  pl.BlockSpec((1, tm, FOUR_W), row),
                  pl.BlockSpec((1, tm, D_SSD), row),
                  pl.BlockSpec((1, tm, dm), lambda bi, i: (bi, i, gblk)),
                  pl.BlockSpec((1, tm, dm), lambda bi, i: (bi, i, gblk + 1)),
                  pl.BlockSpec((1, tm, dm), lambda bi, i: (bi, i, gblk + 2)),
                  pl.BlockSpec((1, tm, dm), row),
                  pl.BlockSpec((1, 1, dm), lambda bi, i: (bi, 0, 0)),
                  _resident((1, POOL_W, POOL_OUT), lyr),
                  _resident((1, 1, dm), lyr),
                  _resident((1, FOUR_W, dm), lyr),
                  _resident((1, D_SSD, dm), lyr),
                  _resident((1, dm, dm), lyr)],
        out_specs=pl.BlockSpec((1, tm, dm), row),
        out_shape=jax.ShapeDtypeStruct((b, l, dm), F32),
        compiler_params=_cparams(("parallel", "parallel")),
        name="merge_out",
    )(d, f, yn, p, p, p, x, g_m, wts["w_pool"], wts["pool_scale"], wts["w_fourier"], wts["w_ssd_out"],
      wts["w_out"])


FFN_PRO_SLICES = 8


def _ffn_kernel(x_ref, xn_ref, nw_ref, sc_ref, sh_ref, scn_ref, shn_ref, wg_ref, wu_ref, wd_ref, gate_ref,
                fnw_ref, o_ref, h_ref, acc_ref, *, final_norm):
    j = pl.program_id(2)
    tile = pl.program_id(0) * pl.num_programs(1) + pl.program_id(1)
    slot = lax.rem(tile, 2)
    rows = x_ref.shape[1] // FFN_PRO_SLICES

    @pl.when((tile == 0) & (j == 0))
    def _():
        h_ref[0] = _norm_mod(x_ref[0], nw_ref[...], sc_ref[0], sh_ref[0]).astype(BF16)

    @pl.when(j == 0)
    def _():
        acc_ref[...] = jnp.zeros_like(acc_ref)

    h = h_ref[slot]
    a = _silu(_dot(h, wg_ref[0])) * _dot(h, wu_ref[0])
    acc_ref[...] += _dot(a.astype(BF16), wd_ref[0])

    r0 = pl.multiple_of(jnp.clip(j - 1, 0, FFN_PRO_SLICES - 1) * rows, rows)
    h_ref[1 - slot, pl.ds(r0, rows), :] = _norm_mod(
        xn_ref[0, pl.ds(r0, rows), :], nw_ref[...], scn_ref[0], shn_ref[0]).astype(BF16)

    @pl.when(j == pl.num_programs(2) - 1)
    def _():
        o = x_ref[0] + gate_ref[0] * acc_ref[...]
        if final_norm:
            ms = jnp.mean(o * o, axis=-1, keepdims=True)
            o = (o * lax.rsqrt(ms + EPS)) * fnw_ref[...]
        o_ref[0] = o


def _ffn(x, nw, sc, sh, wg, wu, wd, layer, gate, fnw, final_norm, tm, tf):
    b, l, d = x.shape
    ff = wg.shape[2]
    tm = min(tm, l)
    n_i = l // tm
    assert ff // tf > FFN_PRO_SLICES and tm % (16 * FFN_PRO_SLICES) == 0
    vec = lambda bi, i, j: (bi, 0, 0)

    def nxt_b(bi, i):
        return jnp.where(i + 1 < n_i, bi, jnp.minimum(bi + 1, b - 1))

    def nxt_i(bi, i):
        return jnp.where(i + 1 < n_i, i + 1, jnp.where(bi + 1 < b, 0, i))

    vec_n = lambda bi, i, j: (nxt_b(bi, i), 0, 0)
    return pl.pallas_call(
        functools.partial(_ffn_kernel, final_norm=final_norm),
        grid=(b, n_i, ff // tf),
        in_specs=[pl.BlockSpec((1, tm, d), lambda bi, i, j: (bi, i, 0)),
                  pl.BlockSpec((1, tm, d), lambda bi, i, j: (nxt_b(bi, i), nxt_i(bi, i), 0)),
                  pl.BlockSpec((1, d), lambda bi, i, j: (0, 0)),
                  pl.BlockSpec((1, 1, d), vec),
                  pl.BlockSpec((1, 1, d), vec),
                  pl.BlockSpec((1, 1, d), vec_n),
                  pl.BlockSpec((1, 1, d), vec_n),
                  pl.BlockSpec((1, d, tf), lambda bi, i, j: (layer, 0, j)),
                  pl.BlockSpec((1, d, tf), lambda bi, i, j: (layer, 0, j)),
                  pl.BlockSpec((1, tf, d), lambda bi, i, j: (layer, j, 0)),
                  pl.BlockSpec((1, 1, d), vec),
                  pl.BlockSpec((1, d), lambda bi, i, j: (0, 0))],
        out_specs=pl.BlockSpec((1, tm, d), lambda bi, i, j: (bi, i, 0)),
        out_shape=jax.ShapeDtypeStruct((b, l, d), F32),
        scratch_shapes=[pltpu.VMEM((2, tm, d), BF16), pltpu.VMEM((tm, d), F32)],
        compiler_params=_cparams(("arbitrary", "arbitrary", "arbitrary")),
        name="ffn",
    )(x, x, nw.reshape(1, d), sc, sh, sc, sh, wg, wu, wd, gate, fnw.reshape(1, d))


def _box_matrix(n, w):
    idx = np.arange(n)
    lo = np.clip(idx - w // 2, 0, n)
    hi = np.clip(idx + (w - w // 2), 0, n)
    m = ((idx[None, :] >= lo[:, None]) & (idx[None, :] < hi[:, None])).astype(np.float64)
    return m / (hi - lo)[:, None]


POOL_TB = 256
POOL_PAD = 16


def _pool_kernel(u_ref, mh_ref, ml_ref, o_ref, *scratch, grid_rows):
    g = pl.program_id(1)
    l = u_ref.shape[1]
    mh = mh_ref[0]
    ml = ml_ref[0]
    if grid_rows is None:
        v = u_ref[0]
        o_ref[0] = (_dot(mh, v) + _dot(ml, v) - v.astype(F32)).astype(o_ref.dtype)
        return

    cp_ref, = scratch
    pad = POOL_PAD * GRID_W
    cp_ref[0:pad, :] = jnp.zeros((pad, POOL_GC), F32)
    cp_ref[pad + l:pad + l + pad, :] = jnp.zeros((pad, POOL_GC), F32)
    for i in range(l // POOL_TB):
        v = u_ref[0, i * POOL_TB:(i + 1) * POOL_TB, :]
        cp_ref[pad + i * POOL_TB:pad + (i + 1) * POOL_TB, :] = _dot(mh, v) + _dot(ml, v)

    for gi, w in enumerate(POOL_WINDOWS):
        lo_off, hi_off = -(w // 2), w - w // 2

        @pl.when(g == gi)
        def _(lo_off=lo_off, hi_off=hi_off):
            def body(r, carry):
                s = jnp.zeros((GRID_W, POOL_GC), F32)
                for o in range(lo_off, hi_off):
                    start = pl.multiple_of(pad + (r + o) * GRID_W, GRID_W)
                    s = s + cp_ref[pl.ds(start, GRID_W), :]
                cnt = jnp.minimum(r + hi_off, grid_rows) - jnp.maximum(r + lo_off, 0)
                cntv = jnp.full((GRID_W, POOL_GC), cnt, jnp.int32).astype(F32)
                t0 = pl.multiple_of(r * GRID_W, GRID_W)
                v = u_ref[0, pl.ds(t0, GRID_W), :].astype(F32)
                o_ref[0, pl.ds(t0, GRID_W), :] = (s / cntv - v).astype(o_ref.dtype)
                return carry

            lax.fori_loop(0, grid_rows, body, 0)


def _pool_diff(p, grid_rows):
    b, l, _ = p.shape
    mats = []
    for w in POOL_WINDOWS:
        if grid_rows is None:
            mats.append(_box_matrix(l, w))
        else:
            mats.append(np.kron(np.eye(POOL_TB // GRID_W), _box_matrix(GRID_W, w)))
    mh, ml = _np_split2(np.stack(mats))
    tb = mats[0].shape[0]
    scratch = []
    if grid_rows is not None:
        scratch = [pltpu.VMEM((l + 2 * POOL_PAD * GRID_W, POOL_GC), F32)]
    cblk = P_POOL // POOL_GC
    return pl.pallas_call(
        functools.partial(_pool_kernel, grid_rows=grid_rows),
        grid=(b, POOL_GROUPS),
        in_specs=[pl.BlockSpec((1, l, POOL_GC), lambda bi, g: (bi, 0, cblk + g)),
                  pl.BlockSpec((1, tb, tb), lambda bi, g: (g, 0, 0)),
                  pl.BlockSpec((1, tb, tb), lambda bi, g: (g, 0, 0))],
        out_specs=pl.BlockSpec((1, l, POOL_GC), lambda bi, g: (bi, 0, g)),
        out_shape=jax.ShapeDtypeStruct((b, l, POOL_W), BF16),
        scratch_shapes=scratch,
        compiler_params=_cparams(("parallel", "parallel")),
        name="pool_diff",
    )(p, mh, ml)


def _chdft_fold_kernel(u_ref, ua_ref, ub_ref, w_ref, flip_ref, o_ref):
    i = pl.program_id(2)
    c = FOUR_GC
    tm = u_ref.shape[1]
    w = w_ref[...]
    urev = _dot(flip_ref[...], ua_ref[0])
    row = lax.broadcasted_iota(jnp.int32, (tm, c), 0)
    urev = jnp.where(row == 0, jnp.broadcast_to(ub_ref[0, 0:1, :].astype(F32), (tm, c)), urev).astype(BF16)
    z = _dot(u_ref[0], w)
    zrev = _dot(urev, w)
    first = (row + jnp.minimum(i, 1)) == 0
    er = z[:, :c] + jnp.where(first, 0.0, zrev[:, :c])
    oi = jnp.where(first, zrev[:, :c], z[:, c:] - zrev[:, c:])
    o_ref[0, 0, :, :c] = er.astype(o_ref.dtype)
    o_ref[0, 0, :, c:] = oi.astype(o_ref.dtype)


def _chdft_fold(p, tm):
    b, l, _ = p.shape
    half = l // 2
    tm = min(tm, half)
    nb = l // tm
    k = np.arange(FOUR_GC)
    ang = -2.0 * np.pi * ((k[:, None] * k[None, :]) % FOUR_GC) / FOUR_GC
    w = jnp.asarray(np.concatenate([np.cos(ang), np.sin(ang)], axis=1).astype(np.float32)).astype(BF16)
    flip = np.zeros((tm, tm), np.float32)
    flip[np.arange(1, tm), tm - np.arange(1, tm)] = 1.0
    cblk = P_FOUR // FOUR_GC
    return pl.pallas_call(
        _chdft_fold_kernel,
        grid=(b, FOUR_GROUPS, half // tm),
        in_specs=[pl.BlockSpec((1, tm, FOUR_GC), lambda bi, g, i: (bi, i, cblk + g)),
                  pl.BlockSpec((1, tm, FOUR_GC), lambda bi, g, i: (bi, nb - 1 - i, cblk + g)),
                  pl.BlockSpec((1, tm, FOUR_GC), lambda bi, g, i: (bi, jnp.where(i == 0, nb // 2, nb - i), cblk + g)),
                  _resident((FOUR_GC, 2 * FOUR_GC), lambda bi, g, i: (0, 0)),
                  _resident((tm, tm), lambda bi, g, i: (0, 0))],
        out_specs=pl.BlockSpec((1, 1, tm, 2 * FOUR_GC), lambda bi, g, i: (bi, g, i, 0)),
        out_shape=jax.ShapeDtypeStruct((b, FOUR_GROUPS, half, 2 * FOUR_GC), BF16),
        compiler_params=_cparams(("parallel", "parallel", "parallel")),
        name="chdft",
    )(p, p, p, w, jnp.asarray(flip).astype(BF16))


def _seq_dft_tables(l):
    half = l // 2
    r = int(round(math.sqrt(l)))
    assert r * r == l
    col = np.arange(half)
    hi = 2.0 * np.pi * ((np.arange(r)[:, None] * r * col[None, :]) % l) / l
    lo = 2.0 * np.pi * ((np.arange(r)[:, None] * col[None, :]) % l) / l
    ch, sh = [jnp.asarray(f(hi).astype(np.float32))[:, None, :] for f in (np.cos, np.sin)]
    cl, sl = [jnp.asarray(f(lo).astype(np.float32))[None, :, :] for f in (np.cos, np.sin)]
    cos = (ch * cl - sh * sl).reshape(l, half)
    sin = (sh * cl + ch * sl).reshape(l, half)
    nyq = jnp.asarray((1.0 - 2.0 * (np.arange(l) % 2)).astype(np.float32))[:, None]
    sin = jnp.where(jnp.asarray(col)[None, :] == 0, nyq, sin)
    return cos.astype(BF16), sin.astype(BF16)


def _seqdft_kernel(c_ref, s_ref, e_ref, o_ref, *, scale):
    c = FOUR_GC
    e = e_ref[0, 0]
    y = _dot(c_ref[...], e[:, :c]) + _dot(s_ref[...], e[:, c:])
    o_ref[0] = (y * scale).astype(o_ref.dtype)


def _fourier(p, tm):
    b, l, _ = p.shape
    half = l // 2
    tm = min(tm, l)
    e = _chdft_fold(p, 512)
    cos, sin = _seq_dft_tables(l)
    scale = 1.0 / math.sqrt(l * FOUR_GC)
    return pl.pallas_call(
        functools.partial(_seqdft_kernel, scale=scale),
        grid=(l // tm, b, FOUR_GROUPS),
        in_specs=[pl.BlockSpec((tm, half), lambda i, bi, g: (i, 0)),
                  pl.BlockSpec((tm, half), lambda i, bi, g: (i, 0)),
                  pl.BlockSpec((1, 1, half, 2 * FOUR_GC), lambda i, bi, g: (bi, g, 0, 0))],
        out_specs=pl.BlockSpec((1, tm, FOUR_GC), lambda i, bi, g: (bi, i, g)),
        out_shape=jax.ShapeDtypeStruct((b, l, FOUR_W), BF16),
        compiler_params=_cparams(("parallel", "parallel", "parallel")),
        name="seqdft",
    )(cos, sin, e)


CONV_CB = 512
_SHIFT_TAPS = tuple(k for k in range(CONV_W) if k != CONV_W // 2)


def _shift_matrix():
    m = np.zeros((len(_SHIFT_TAPS) * CHUNK, CHUNK + 2 * HALO), np.float32)
    t = np.arange(CHUNK)
    for q, k in enumerate(_SHIFT_TAPS):
        m[q * CHUNK + t, HALO - CONV_W // 2 + k + t] = 1.0
    return jnp.asarray(m).astype(BF16)


def _ssd_kernel(*refs, fwd, nc, epilogue):
    if fwd:
        xc_ref, dt_ref, dtb_ref, alog_ref, e_ref, h0_ref = refs[:6]
        rest = refs[6:]
        if epilogue:
            z_ref, yb_ref, dsk_ref, nw_ref, y_ref, hout_ref, st_ref = rest
        else:
            y_ref, hout_ref, st_ref = rest
    else:
        (xm_ref, xp_ref, xn_ref, dt_ref, cw_ref, cb_ref, shift_ref, dtb_ref, alog_ref, e_ref, h0_ref,
         y_ref, xc_ref, hout_ref, st_ref, ext_ref) = refs
    c = pl.program_id(0)
    cc = c if fwd else nc - 1 - c
    T = CHUNK
    nb = dt_ref.shape[0]

    @pl.when(c == 0)
    def _():
        st_ref[...] = h0_ref[...]

    if not fwd:
        @pl.when(cc > 0)
        def _():
            ext_ref[:, 0:HALO, :] = xp_ref[...]

        @pl.when(cc == 0)
        def _():
            ext_ref[:, 0:HALO, :] = jnp.zeros((nb, HALO, XBC), ext_ref.dtype)

        ext_ref[:, HALO:HALO + T, :] = xm_ref[...]

        @pl.when(cc < nc - 1)
        def _():
            ext_ref[:, HALO + T:HALO + T + HALO, :] = xn_ref[...]

        @pl.when(cc == nc - 1)
        def _():
            ext_ref[:, HALO + T:HALO + T + HALO, :] = jnp.zeros((nb, HALO, XBC), ext_ref.dtype)

        shift = shift_ref[...]
        for j in range(XBC // CONV_CB):
            cs = slice(j * CONV_CB, (j + 1) * CONV_CB)
            for bi in range(nb):
                sh = _dot(shift, ext_ref[bi, :, cs])
                acc = cb_ref[:, cs] + xm_ref[bi, :, cs].astype(F32) * cw_ref[CONV_W // 2:CONV_W // 2 + 1, cs]
                for q, k in enumerate(_SHIFT_TAPS):
                    acc = acc + sh[q * T:(q + 1) * T] * cw_ref[k:k + 1, cs]
                xc_ref[bi, :, cs] = _silu(acc).astype(xc_ref.dtype)

    row = lax.broadcasted_iota(jnp.int32, (T, T), 0)
    col = lax.broadcasted_iota(jnp.int32, (T, T), 1)
    causal = (col <= row) if fwd else (col >= row)
    tri = jnp.where(causal, 1.0, 0.0).astype(BF16)
    lane = lax.broadcasted_iota(jnp.int32, (T, 2 * HEAD_DIM), 1)
    hbase = 0 if fwd else SSD_HEADS
    e = e_ref[...]
    a = -jnp.exp(alog_ref[...])

    def head_factors(bi):
        xdt = dt_ref[bi] + dtb_ref[...]
        dt = jnp.maximum(xdt, 0.0) + jnp.log1p(jnp.exp(-jnp.abs(xdt)))
        da = dt * a
        d1 = da.astype(BF16)
        r1 = da - d1.astype(F32)
        d2 = r1.astype(BF16)
        d3 = (r1 - d2.astype(F32)).astype(BF16)
        acs = _dot(tri, d1) + _dot(tri, d2) + _dot(tri, d3)
        tot = acs[T - 1:T, :] if fwd else acs[0:1, :]
        eh, el = _split2(jnp.broadcast_to(jnp.exp(tot), (16, T)))
        ex = _dot(jnp.concatenate([jnp.exp(acs).astype(BF16), (dt * jnp.exp(tot - acs)).astype(BF16), eh, el],
                                  axis=0), e)
        return dict(acs=acs, arow_t=(acs - jnp.log(dt)).T, eacs_x=ex[0:T], w2=ex[T:2 * T],
                    etot_x=ex[2 * T:2 * T + 1] + ex[2 * T + 16:2 * T + 17])

    hf = [head_factors(bi) for bi in range(nb)]

    for g in range(SSD_GROUPS):
        gs = slice(g * GROUP_W, (g + 1) * GROUP_W)
        bs = slice(D_SSD + g * D_STATE, D_SSD + (g + 1) * D_STATE)
        cs_ = slice(D_SSD + (SSD_GROUPS + g) * D_STATE, D_SSD + (SSD_GROUPS + g + 1) * D_STATE)
        for bi in range(nb):
            f = hf[bi]
            bb = xc_ref[bi, :, bs]
            cbf = xc_ref[bi, :, cs_]
            cb = lax.dot_general(cbf, bb, (((1,), (1,)), ((), ())), preferred_element_type=F32)
            st = st_ref[bi, g]
            yoff = _dot(cbf, st.astype(BF16)) * f["eacs_x"][:, gs]
            xb = xc_ref[bi, :, gs]
            ys = []
            for jp in range(HEADS_PER_GROUP // 2):
                ms = []
                for jj in range(2):
                    hc = hbase + g * HEADS_PER_GROUP + 2 * jp + jj
                    diff = f["acs"][:, hc:hc + 1] - f["arow_t"][hc:hc + 1, :]
                    ms.append((cb * jnp.exp(jnp.where(causal, diff, -1e30))).astype(BF16))
                r = _dot(jnp.concatenate(ms, axis=0), xb[:, jp * 2 * HEAD_DIM:(jp + 1) * 2 * HEAD_DIM])
                ys.append(jnp.where(lane < HEAD_DIM, r[:T], r[T:]))
            yg = jnp.concatenate(ys, axis=1) + yoff

            xg = xb.astype(F32)
            xds = (xg * f["w2"][:, gs]).astype(BF16)
            st_ref[bi, g] = st * f["etot_x"][:, gs] + _dot(bb.astype(F32).T.astype(BF16), xds)

            if epilogue:
                yt = yg + yb_ref[bi, :, gs] + dsk_ref[:, gs] * xg
                v = yt * _silu(z_ref[bi, :, gs].astype(F32))
                ms_ = jnp.mean(v * v, axis=-1, keepdims=True)
                y_ref[bi, :, gs] = ((v * lax.rsqrt(ms_ + EPS)) * nw_ref[:, gs]).astype(y_ref.dtype)
            else:
                y_ref[bi, :, gs] = yg

    @pl.when(c == nc - 1)
    def _():
        hout_ref[...] = st_ref[...]


_STATE_TAIL = (SSD_GROUPS, D_STATE, GROUP_W)


def _ssd_bwd(p, pdt, lw, h0):
    b, l, _ = p.shape
    nc = l // CHUNK
    nh = l // HALO
    per = CHUNK // HALO
    cidx = lambda c: nc - 1 - c
    chunk = lambda c: (0, cidx(c), 0)
    const2 = lambda c: (0, 0)
    state = lambda c: (0, 0, 0, 0)
    state_block = (b,) + _STATE_TAIL
    return pl.pallas_call(
        functools.partial(_ssd_kernel, fwd=False, nc=nc, epilogue=False),
        grid=(nc,),
        in_specs=[pl.BlockSpec((b, CHUNK, XBC), chunk),
                  pl.BlockSpec((b, HALO, XBC), lambda c: (0, jnp.maximum(cidx(c) * per - 1, 0), 0)),
                  pl.BlockSpec((b, HALO, XBC), lambda c: (0, jnp.minimum((cidx(c) + 1) * per, nh - 1), 0)),
                  pl.BlockSpec((b, CHUNK, DT_PAD), chunk),
                  _resident((8, XBC), const2),
                  _resident((1, XBC), const2),
                  _resident((len(_SHIFT_TAPS) * CHUNK, CHUNK + 2 * HALO), const2),
                  _resident((1, DT_PAD), const2),
                  _resident((1, DT_PAD), const2),
                  _resident((DT_PAD, D_SSD), const2),
                  _resident(state_block, state)],
        out_specs=[pl.BlockSpec((b, CHUNK, D_SSD), chunk),
                   pl.BlockSpec((b, CHUNK, XBC), chunk),
                   pl.BlockSpec(state_block, state)],
        out_shape=[jax.ShapeDtypeStruct((b, l, D_SSD), F32),
                   jax.ShapeDtypeStruct((b, l, XBC), BF16),
                   jax.ShapeDtypeStruct(state_block, F32)],
        scratch_shapes=[pltpu.VMEM(state_block, F32),
                        pltpu.VMEM((b, CHUNK + 2 * HALO, XBC), BF16)],
        compiler_params=_cparams(("arbitrary",)),
        name="ssd_bwd",
    )(p, p, p, pdt, lw["conv_w"], lw["conv_b"], _shift_matrix(), lw["dt_bias"], lw["a_log"], lw["expand_b"], h0)


def _ssd_fwd(xc, pdt, lw, h0, p=None, yb=None):
    b, l, _ = xc.shape
    nc = l // CHUNK
    chunk = lambda c: (0, c, 0)
    const2 = lambda c: (0, 0)
    state = lambda c: (0, 0, 0, 0)
    state_block = (b,) + _STATE_TAIL
    epilogue = p is not None
    in_specs = [pl.BlockSpec((b, CHUNK, XBC), chunk),
                pl.BlockSpec((b, CHUNK, DT_PAD), chunk),
                _resident((1, DT_PAD), const2),
                _resident((1, DT_PAD), const2),
                _resident((DT_PAD, D_SSD), const2),
                _resident(state_block, state)]
    args = [xc, pdt, lw["dt_bias"], lw["a_log"], lw["expand_f"], h0]
    if epilogue:
        z_blk = P_Z // D_SSD
        in_specs += [pl.BlockSpec((b, CHUNK, D_SSD), lambda c: (0, c, z_blk)),
                     pl.BlockSpec((b, CHUNK, D_SSD), chunk),
                     _resident((1, D_SSD), const2),
                     _resident((1, D_SSD), const2)]
        args += [p, yb, lw["d_skip"], lw["ssd_norm_w"]]
    return pl.pallas_call(
        functools.partial(_ssd_kernel, fwd=True, nc=nc, epilogue=epilogue),
        grid=(nc,),
        in_specs=in_specs,
        out_specs=[pl.BlockSpec((b, CHUNK, D_SSD), chunk),
                   pl.BlockSpec(state_block, state)],
        out_shape=[jax.ShapeDtypeStruct((b, l, D_SSD), BF16 if epilogue else F32),
                   jax.ShapeDtypeStruct(state_block, F32)],
        scratch_shapes=[pltpu.VMEM(state_block, F32)],
        compiler_params=_cparams(("arbitrary",)),
        name="ssd_fwd",
    )(*args)


def _ssd(p, pdt, lw, h0f, h0b):
    yb, xc, hb = _ssd_bwd(p, pdt, lw, h0b)
    y, hf = _ssd_fwd(xc, pdt, lw, h0f, p, yb)
    return y, hf, hb


def _expand_matrix(offset):
    e = np.zeros((DT_PAD, D_SSD), np.float32)
    for h in range(SSD_HEADS):
        e[offset + h, h * HEAD_DIM:(h + 1) * HEAD_DIM] = 1.0
    return jnp.asarray(e).astype(BF16)


def _layer_weights(l, conv_w, conv_b, a_log, dt_bias, d_skip, ssd_norm_w):
    pad_h = (0, DT_PAD - 2 * SSD_HEADS)
    return {
        "conv_w": jnp.pad(conv_w[l], ((0, 8 - CONV_W), (0, 0))),
        "conv_b": conv_b[l].reshape(1, XBC),
        "dt_bias": jnp.pad(dt_bias[l].reshape(-1), pad_h).reshape(1, DT_PAD),
        "a_log": jnp.pad(a_log[l].reshape(-1), pad_h).reshape(1, DT_PAD),
        "d_skip": jnp.repeat(d_skip[l], HEAD_DIM).reshape(1, D_SSD),
        "ssd_norm_w": ssd_norm_w[l].reshape(1, D_SSD),
        "expand_f": _expand_matrix(0),
        "expand_b": _expand_matrix(SSD_HEADS),
    }


def _mixer(x, nw, sc, sh, g_m, wts, lw, layer, grid_rows, h0f, h0b):
    b, l, dm = x.shape
    tok = (lambda a: a.reshape(1, b * l, a.shape[-1])) if sc.shape[0] == 1 else (lambda a: a)
    seq = lambda a: a.reshape(b, l, a.shape[-1])
    p, pdt = _inproj(tok(x), nw, sc, sh, wts["w_main"], wts["w_dt"], layer, P_TOT, 1024, 1024)
    y_n, hf, hb = _ssd(seq(p), seq(pdt), lw, h0f, h0b)
    d = _pool_diff(seq(p), grid_rows)
    f = _fourier(seq(p), 2048)
    x = seq(_merge_out(tok(d), tok(f), tok(y_n), p, tok(x), g_m, wts, layer, 256))
    return x, hf, hb


def kernel(x, c, ctx, c_ctx, w_ada, b_ada, norm_mix_w, norm_ffn_w, w_in, conv_w, conv_b, a_log, dt_bias,
           d_skip, ssd_norm_w, w_ssd_out, w_pool, pool_scale, w_fourier, w_out, w_ffn_gate, w_ffn_up,
           w_ffn_down, final_norm_w):
    b, seq, d = x.shape
    rows = seq // GRID_W
    mod = _adaln(jnp.concatenate([c, c_ctx[None, :]], axis=0), w_ada, b_ada)
    h0 = jnp.zeros((b,) + _STATE_TAIL, F32)
    lc = ctx.shape[1]
    w_main = _winprep(w_in)
    w_dt = jnp.pad(w_in[:, :, XBC:SSD_IN], ((0, 0), (0, 0), (0, DT_PAD - 2 * SSD_HEADS))).astype(BF16)
    wts = {
        "w_main": w_main,
        "w_dt": w_dt,
        "w_pool": _to_bf16(w_pool.reshape(DEPTH, POOL_W, POOL_OUT)),
        "pool_scale": pool_scale.reshape(DEPTH, 1, D_MODEL),
        "w_fourier": _to_bf16(w_fourier),
        "w_ssd_out": _to_bf16(w_ssd_out),
        "w_out": _to_bf16(w_out),
    }
    wg, wu, wd = _to_bf16(w_ffn_gate), _to_bf16(w_ffn_up), _to_bf16(w_ffn_down)
    for l in range(DEPTH):
        last = l == DEPTH - 1
        lw = _layer_weights(l, conv_w, conv_b, a_log, dt_bias, d_skip, ssd_norm_w)
        sh_m, sc_m, g_m, sh_f, sc_f, g_f = [mod[l, :b, i * d:(i + 1) * d].reshape(b, 1, d) for i in range(6)]
        csh_m, csc_m, cg_m, csh_f, csc_f, cg_f = [mod[l, b, i * d:(i + 1) * d].reshape(1, 1, d) for i in range(6)]

        if last:
            pc, pdtc = _inproj(ctx.reshape(1, b * lc, d), norm_mix_w[l], csc_m, csh_m, w_main, w_dt, l,
                               XBC, 1024, 1024)
            pdtc = pdtc.reshape(b, lc, DT_PAD)
            _, xcc, hb = _ssd_bwd(pc.reshape(b, lc, XBC), pdtc, lw, h0)
            _, hf = _ssd_fwd(xcc, pdtc, lw, h0)
        else:
            ctx, hf, hb = _mixer(ctx, norm_mix_w[l], csc_m, csh_m, cg_m, wts, lw, l, None, h0, h0)
            ctx = _ffn(ctx.reshape(1, b * lc, d), norm_ffn_w[l], csc_f, csh_f, wg, wu, wd, l,
                       cg_f, final_norm_w, False, 512, 512).reshape(b, lc, d)

        x, _, _ = _mixer(x, norm_mix_w[l], sc_m, sh_m, g_m, wts, lw, l, rows, hf, hb)
        x = _ffn(x, norm_ffn_w[l], sc_f, sh_f, wg, wu, wd, l, g_f, final_norm_w, last, 512, 512)
    return x
```

```python
import functools
import math

import numpy as np
import jax
import jax.numpy as jnp
from jax import lax
from jax.experimental import pallas as pl
from jax.experimental.pallas import tpu as pltpu

F32 = jnp.float32
BF16 = jnp.bfloat16

D_MODEL = 2048
DEPTH = 2
GRID_W = 64
EPS = 1e-6

POOL_GROUPS = 4
POOL_WINDOWS = (2, 4, 8, 16)
POOL_W = D_MODEL // 2
POOL_GC = POOL_W // POOL_GROUPS
POOL_OUT = D_MODEL // POOL_GROUPS

FOUR_GROUPS = 4
FOUR_W = D_MODEL // 2
FOUR_GC = FOUR_W // FOUR_GROUPS

D_SSD = D_MODEL
HEAD_DIM = 64
SSD_HEADS = D_SSD // HEAD_DIM
SSD_GROUPS = 4
HEADS_PER_GROUP = SSD_HEADS // SSD_GROUPS
GROUP_W = HEADS_PER_GROUP * HEAD_DIM
D_STATE = 128
CONV_W = 5
CHUNK = 128

D_FF = ((8 * D_MODEL // 3 + 255) // 256) * 256

XBC = D_SSD + 2 * SSD_GROUPS * D_STATE
SSD_IN = XBC + 2 * SSD_HEADS
Z_OFF = SSD_IN
POOL_OFF = Z_OFF + D_SSD
FOUR_OFF = POOL_OFF + POOL_W
GATE_OFF = FOUR_OFF + FOUR_W
N_IN = GATE_OFF + 3 * D_MODEL

P_XBC = 0
P_POOL = XBC
P_Z = P_POOL + POOL_W
P_GATE = P_Z + D_SSD
P_FOUR = P_GATE + 3 * D_MODEL
P_TOT = P_FOUR + FOUR_W
DT_PAD = 128

V7X_VMEM_LIMIT = 56 * 1024 * 1024
HALO = 16


def _cparams(sem):
    return pltpu.CompilerParams(dimension_semantics=sem, vmem_limit_bytes=V7X_VMEM_LIMIT)


def _resident(shape, index_map):
    return pl.BlockSpec(shape, index_map, pipeline_mode=pl.Buffered(1))


def _split2(v):
    hi = v.astype(BF16)
    lo = (v - hi.astype(F32)).astype(BF16)
    return hi, lo


def _dot(a, b):
    return jnp.dot(a, b, preferred_element_type=F32)


def _sigmoid(v):
    return 0.5 * jnp.tanh(0.5 * v) + 0.5


def _silu(v):
    return v * _sigmoid(v)


def _np_split2(m):
    m = jnp.asarray(np.asarray(m, np.float32))
    hi = m.astype(BF16)
    lo = (m - hi.astype(F32)).astype(BF16)
    return hi, lo


CAST_ROWS = 256
WIN_CHUNK = 1024


def _cast_kernel(w_ref, o_ref):
    o_ref[...] = w_ref[...].astype(BF16)


def _to_bf16(w):
    depth, k, n = w.shape
    return pl.pallas_call(
        _cast_kernel,
        grid=(depth, k // CAST_ROWS),
        in_specs=[pl.BlockSpec((1, CAST_ROWS, n), lambda l, i: (l, i, 0))],
        out_specs=pl.BlockSpec((1, CAST_ROWS, n), lambda l, i: (l, i, 0)),
        out_shape=jax.ShapeDtypeStruct((depth, k, n), BF16),
        compiler_params=_cparams(("parallel", "parallel")),
        name="cast_bf16",
    )(w)


_WIN_SEGMENTS = ((P_XBC, 0, XBC), (P_POOL, POOL_OFF, POOL_W), (P_Z, Z_OFF, D_SSD),
                 (P_GATE, GATE_OFF, 3 * D_MODEL), (P_FOUR, FOUR_OFF, FOUR_W))


def _win_src(j):
    src = jnp.int32(0)
    for dst0, src0, width in _WIN_SEGMENTS:
        first = dst0 // WIN_CHUNK
        inside = (j >= first) & (j < first + width // WIN_CHUNK)
        src = jnp.where(inside, src0 + (j - first) * WIN_CHUNK, src)
    return pl.multiple_of(src, 2 * SSD_HEADS)


def _winprep_kernel(wt_ref, o_ref):
    o_ref[0] = wt_ref[0].T.astype(BF16)


def _winprep(w_in):
    depth, d, n = w_in.shape
    wt = jnp.swapaxes(w_in, 1, 2)
    return pl.pallas_call(
        _winprep_kernel,
        grid=(depth, P_TOT // WIN_CHUNK),
        in_specs=[pl.BlockSpec((pl.Element(1), pl.Element(WIN_CHUNK), pl.Element(d)),
                               lambda l, j: (l, _win_src(j), 0))],
        out_specs=pl.BlockSpec((1, d, WIN_CHUNK), lambda l, j: (l, 0, j)),
        out_shape=jax.ShapeDtypeStruct((depth, d, P_TOT), BF16),
        compiler_params=_cparams(("parallel", "parallel")),
        name="winprep",
    )(wt)


ADA_ROWS = 8


def _adaln_kernel(ct_ref, w_ref, b_ref, o_ref, *, n_rows):
    s = _silu(ct_ref[...])
    w = w_ref[0]
    tn = w.shape[1]
    rows = [jnp.sum(w * s[:, r:r + 1], axis=0, keepdims=True) for r in range(n_rows)]
    rows.append(jnp.zeros((ADA_ROWS - n_rows, tn), F32))
    o_ref[0] = jnp.concatenate(rows, axis=0) + b_ref[0]


def _adaln(cc, w_ada, b_ada, tn=1024):
    depth, d, n = w_ada.shape
    n_rows = cc.shape[0]
    ct = jnp.pad(cc, ((0, ADA_ROWS - n_rows), (0, 0))).T
    return pl.pallas_call(
        functools.partial(_adaln_kernel, n_rows=n_rows),
        grid=(depth, n // tn),
        in_specs=[pl.BlockSpec((d, ADA_ROWS), lambda l, j: (0, 0)),
                  pl.BlockSpec((1, d, tn), lambda l, j: (l, 0, j)),
                  pl.BlockSpec((1, 1, tn), lambda l, j: (l, 0, j))],
        out_specs=pl.BlockSpec((1, ADA_ROWS, tn), lambda l, j: (l, 0, j)),
        out_shape=jax.ShapeDtypeStruct((depth, ADA_ROWS, n), F32),
        compiler_params=_cparams(("parallel", "parallel")),
        name="adaln",
    )(ct, w_ada, b_ada.reshape(depth, 1, n))


def _norm_mod(x, nw, sc, sh):
    ms = jnp.mean(x * x, axis=-1, keepdims=True)
    return (x * lax.rsqrt(ms + EPS)) * nw * (1.0 + sc) + sh


def _inproj_kernel(x_ref, nw_ref, sc_ref, sh_ref, w_ref, wdt_ref, o_ref, dt_ref, h_ref):
    @pl.when(pl.program_id(2) == 0)
    def _():
        h = _norm_mod(x_ref[0], nw_ref[...], sc_ref[0], sh_ref[0]).astype(BF16)
        h_ref[...] = h
        dt_ref[0] = _dot(h, wdt_ref[0])

    o_ref[0] = _dot(h_ref[...], w_ref[0]).astype(o_ref.dtype)


def _inproj(x, nw, sc, sh, w, wdt, layer, n_out, tm, tn):
    b, l, d = x.shape
    tm = min(tm, l)
    return pl.pallas_call(
        _inproj_kernel,
        grid=(b, l // tm, n_out // tn),
        in_specs=[pl.BlockSpec((1, tm, d), lambda bi, i, j: (bi, i, 0)),
                  pl.BlockSpec((1, d), lambda bi, i, j: (0, 0)),
                  pl.BlockSpec((1, 1, d), lambda bi, i, j: (bi, 0, 0)),
                  pl.BlockSpec((1, 1, d), lambda bi, i, j: (bi, 0, 0)),
                  pl.BlockSpec((1, d, tn), lambda bi, i, j: (layer, 0, j)),
                  _resident((1, d, DT_PAD), lambda bi, i, j: (layer, 0, 0))],
        out_specs=[pl.BlockSpec((1, tm, tn), lambda bi, i, j: (bi, i, j)),
                   pl.BlockSpec((1, tm, DT_PAD), lambda bi, i, j: (bi, i, 0))],
        out_shape=[jax.ShapeDtypeStruct((b, l, n_out), BF16),
                   jax.ShapeDtypeStruct((b, l, DT_PAD), F32)],
        scratch_shapes=[pltpu.VMEM((tm, d), BF16)],
        compiler_params=_cparams(("parallel", "parallel", "arbitrary")),
        name="inproj",
    )(x, nw.reshape(1, d), sc, sh, w, wdt)


def _merge_kernel(d_ref, f_ref, yn_ref, gp_ref, gf_ref, gs_ref, x_ref, gm_ref,
                  wp_ref, ps_ref, wf_ref, ws_ref, wo_ref, o_ref):
    d = d_ref[0]
    y_pool = jnp.concatenate(
        [_dot(d[:, g * POOL_GC:(g + 1) * POOL_GC], wp_ref[0, g * POOL_GC:(g + 1) * POOL_GC, :])
         for g in range(POOL_GROUPS)], axis=1)
    m = _sigmoid(gp_ref[0].astype(F32)) * (y_pool * ps_ref[0])
    m = m + _sigmoid(gf_ref[0].astype(F32)) * _dot(f_ref[0], wf_ref[0])
    m = m + _sigmoid(gs_ref[0].astype(F32)) * _dot(yn_ref[0], ws_ref[0])
    o_ref[0] = x_ref[0] + gm_ref[0] * _dot(m.astype(BF16), wo_ref[0])


def _merge_out(d, f, yn, p, x, g_m, wts, layer, tm):
    b, l, dm = x.shape
    tm = min(tm, l)
    gblk = P_GATE // dm
    row = lambda bi, i: (bi, i, 0)
    lyr = lambda bi, i: (layer, 0, 0)
    return pl.pallas_call(
        _merge_kernel,
        grid=(b, l // tm),
        in_specs=[pl.BlockSpec((1, tm, POOL_W), row),
                  pl.BlockSpec((1, tm, FOUR_W), row),
                  pl.BlockSpec((1, tm, D_SSD), row),
                  pl.BlockSpec((1, tm, dm), lambda bi, i: (bi, i, gblk)),
                  pl.BlockSpec((1, tm, dm), lambda bi, i: (bi, i, gblk + 1)),
                  pl.BlockSpec((1, tm, dm), lambda bi, i: (bi, i, gblk + 2)),
                  pl.BlockSpec((1, tm, dm), row),
                  pl.BlockSpec((1, 1, dm), lambda bi, i: (bi, 0, 0)),
                  _resident((1, POOL_W, POOL_OUT), lyr),
                  _resident((1, 1, dm), lyr),
                  _resident((1, FOUR_W, dm), lyr),
                  _resident((1, D_SSD, dm), lyr),
                  _resident((1, dm, dm), lyr)],
        out_specs=pl.BlockSpec((1, tm, dm), row),
        out_shape=jax.ShapeDtypeStruct((b, l, dm), F32),
        compiler_params=_cparams(("parallel", "parallel")),
        name="merge_out",
    )(d, f, yn, p, p, p, x, g_m, wts["w_pool"], wts["pool_scale"], wts["w_fourier"], wts["w_ssd_out"],
      wts["w_out"])


def _ffn_kernel(x_ref, nw_ref, sc_ref, sh_ref, wg_ref, wu_ref, wd_ref, gate_ref, fnw_ref,
                o_ref, h_ref, acc_ref, *, final_norm):
    j = pl.program_id(2)

    @pl.when(j == 0)
    def _():
        h_ref[...] = _norm_mod(x_ref[0], nw_ref[...], sc_ref[0], sh_ref[0]).astype(BF16)
        acc_ref[...] = jnp.zeros_like(acc_ref)

    h = h_ref[...]
    a = _silu(_dot(h, wg_ref[0])) * _dot(h, wu_ref[0])
    acc_ref[...] += _dot(a.astype(BF16), wd_ref[0])

    @pl.when(j == pl.num_programs(2) - 1)
    def _():
        o = x_ref[0] + gate_ref[0] * acc_ref[...]
        if final_norm:
            ms = jnp.mean(o * o, axis=-1, keepdims=True)
            o = (o * lax.rsqrt(ms + EPS)) * fnw_ref[...]
        o_ref[0] = o


def _ffn(x, nw, sc, sh, wg, wu, wd, layer, gate, fnw, final_norm, tm, tf):
    b, l, d = x.shape
    ff = wg.shape[2]
    tm = min(tm, l)
    vec = lambda bi, i, j: (bi, 0, 0)
    return pl.pallas_call(
        functools.partial(_ffn_kernel, final_norm=final_norm),
        grid=(b, l // tm, ff // tf),
        in_specs=[pl.BlockSpec((1, tm, d), lambda bi, i, j: (bi, i, 0)),
                  pl.BlockSpec((1, d), lambda bi, i, j: (0, 0)),
                  pl.BlockSpec((1, 1, d), vec),
                  pl.BlockSpec((1, 1, d), vec),
                  pl.BlockSpec((1, d, tf), lambda bi, i, j: (layer, 0, j)),
                  pl.BlockSpec((1, d, tf), lambda bi, i, j: (layer, 0, j)),
                  pl.BlockSpec((1, tf, d), lambda bi, i, j: (layer, j, 0)),
                  pl.BlockSpec((1, 1, d), vec),
                  pl.BlockSpec((1, d), lambda bi, i, j: (0, 0))],
        out_specs=pl.BlockSpec((1, tm, d), lambda bi, i, j: (bi, i, 0)),
        out_shape=jax.ShapeDtypeStruct((b, l, d), F32),
        scratch_shapes=[pltpu.VMEM((tm, d), BF16), pltpu.VMEM((tm, d), F32)],
        compiler_params=_cparams(("parallel", "parallel", "arbitrary")),
        name="ffn",
    )(x, nw.reshape(1, d), sc, sh, wg, wu, wd, gate, fnw.reshape(1, d))


def _box_matrix(n, w):
    idx = np.arange(n)
    lo = np.clip(idx - w // 2, 0, n)
    hi = np.clip(idx + (w - w // 2), 0, n)
    m = ((idx[None, :] >= lo[:, None]) & (idx[None, :] < hi[:, None])).astype(np.float64)
    return m / (hi - lo)[:, None]


POOL_TB = 256
POOL_PAD = 16


def _pool_kernel(u_ref, mh_ref, ml_ref, o_ref, *scratch, grid_rows):
    g = pl.program_id(1)
    l = u_ref.shape[1]
    mh = mh_ref[0]
    ml = ml_ref[0]
    if grid_rows is None:
        v = u_ref[0]
        o_ref[0] = (_dot(mh, v) + _dot(ml, v) - v.astype(F32)).astype(o_ref.dtype)
        return

    cp_ref, = scratch
    pad = POOL_PAD * GRID_W
    cp_ref[0:pad, :] = jnp.zeros((pad, POOL_GC), F32)
    cp_ref[pad + l:pad + l + pad, :] = jnp.zeros((pad, POOL_GC), F32)
    for i in range(l // POOL_TB):
        v = u_ref[0, i * POOL_TB:(i + 1) * POOL_TB, :]
        cp_ref[pad + i * POOL_TB:pad + (i + 1) * POOL_TB, :] = _dot(mh, v) + _dot(ml, v)

    for gi, w in enumerate(POOL_WINDOWS):
        lo_off, hi_off = -(w // 2), w - w // 2

        @pl.when(g == gi)
        def _(lo_off=lo_off, hi_off=hi_off):
            def body(r, carry):
                s = jnp.zeros((GRID_W, POOL_GC), F32)
                for o in range(lo_off, hi_off):
                    start = pl.multiple_of(pad + (r + o) * GRID_W, GRID_W)
                    s = s + cp_ref[pl.ds(start, GRID_W), :]
                cnt = jnp.minimum(r + hi_off, grid_rows) - jnp.maximum(r + lo_off, 0)
                cntv = jnp.full((GRID_W, POOL_GC), cnt, jnp.int32).astype(F32)
                t0 = pl.multiple_of(r * GRID_W, GRID_W)
                v = u_ref[0, pl.ds(t0, GRID_W), :].astype(F32)
                o_ref[0, pl.ds(t0, GRID_W), :] = (s / cntv - v).astype(o_ref.dtype)
                return carry

            lax.fori_loop(0, grid_rows, body, 0)


def _pool_diff(p, grid_rows):
    b, l, _ = p.shape
    mats = []
    for w in POOL_WINDOWS:
        if grid_rows is None:
            mats.append(_box_matrix(l, w))
        else:
            mats.append(np.kron(np.eye(POOL_TB // GRID_W), _box_matrix(GRID_W, w)))
    mh, ml = _np_split2(np.stack(mats))
    tb = mats[0].shape[0]
    scratch = []
    if grid_rows is not None:
        scratch = [pltpu.VMEM((l + 2 * POOL_PAD * GRID_W, POOL_GC), F32)]
    cblk = P_POOL // POOL_GC
    return pl.pallas_call(
        functools.partial(_pool_kernel, grid_rows=grid_rows),
        grid=(b, POOL_GROUPS),
        in_specs=[pl.BlockSpec((1, l, POOL_GC), lambda bi, g: (bi, 0, cblk + g)),
                  pl.BlockSpec((1, tb, tb), lambda bi, g: (g, 0, 0)),
                  pl.BlockSpec((1, tb, tb), lambda bi, g: (g, 0, 0))],
        out_specs=pl.BlockSpec((1, l, POOL_GC), lambda bi, g: (bi, 0, g)),
        out_shape=jax.ShapeDtypeStruct((b, l, POOL_W), BF16),
        scratch_shapes=scratch,
        compiler_params=_cparams(("parallel", "parallel")),
        name="pool_diff",
    )(p, mh, ml)


FLIP_ROWS = 512


def _flip_matrix(t):
    m = np.zeros((t, t), np.float32)
    m[np.arange(1, t), t - np.arange(1, t)] = 1.0
    return jnp.asarray(m).astype(BF16)


def _chdft_fold_kernel(u_ref, ua_ref, ub_ref, w_ref, flip_ref, o_ref, nyq_ref):
    i = pl.program_id(2)
    c = FOUR_GC
    tm = u_ref.shape[1]
    w = w_ref[...]
    urev = _dot(flip_ref[...], ua_ref[0])
    row = lax.broadcasted_iota(jnp.int32, (tm, c), 0)
    urev = jnp.where(row == 0, jnp.broadcast_to(ub_ref[0, 0:1, :].astype(F32), (tm, c)), urev).astype(BF16)
    z = _dot(u_ref[0], w)
    zrev = _dot(urev, w)
    first = (row + jnp.minimum(i, 1)) == 0
    o_ref[0, 0, :, :c] = (z[:, :c] + jnp.where(first, 0.0, zrev[:, :c])).astype(o_ref.dtype)
    o_ref[0, 0, :, c:] = (z[:, c:] - zrev[:, c:]).astype(o_ref.dtype)

    @pl.when(i == 0)
    def _():
        nyq_ref[0, 0] = zrev[0:8, :c]


def _chdft_fold(p):
    b, l, _ = p.shape
    half = l // 2
    tm = min(FLIP_ROWS, half)
    nb = l // tm
    k = np.arange(FOUR_GC)
    ang = -2.0 * np.pi * ((k[:, None] * k[None, :]) % FOUR_GC) / FOUR_GC
    w = jnp.asarray(np.concatenate([np.cos(ang), np.sin(ang)], axis=1).astype(np.float32)).astype(BF16)
    cblk = P_FOUR // FOUR_GC
    return pl.pallas_call(
        _chdft_fold_kernel,
        grid=(b, FOUR_GROUPS, half // tm),
        in_specs=[pl.BlockSpec((1, tm, FOUR_GC), lambda bi, g, i: (bi, i, cblk + g)),
                  pl.BlockSpec((1, tm, FOUR_GC), lambda bi, g, i: (bi, nb - 1 - i, cblk + g)),
                  pl.BlockSpec((1, tm, FOUR_GC), lambda bi, g, i: (bi, jnp.where(i == 0, nb // 2, nb - i), cblk + g)),
                  _resident((FOUR_GC, 2 * FOUR_GC), lambda bi, g, i: (0, 0)),
                  _resident((tm, tm), lambda bi, g, i: (0, 0))],
        out_specs=[pl.BlockSpec((1, 1, tm, 2 * FOUR_GC), lambda bi, g, i: (bi, g, i, 0)),
                   pl.BlockSpec((1, 1, 8, FOUR_GC), lambda bi, g, i: (bi, g, 0, 0))],
        out_shape=[jax.ShapeDtypeStruct((b, FOUR_GROUPS, half, 2 * FOUR_GC), BF16),
                   jax.ShapeDtypeStruct((b, FOUR_GROUPS, 8, FOUR_GC), F32)],
        compiler_params=_cparams(("parallel", "parallel", "arbitrary")),
        name="chdft",
    )(p, p, p, w, _flip_matrix(tm))


def _seq_dft_tables(l):
    half = l // 2
    r = int(round(math.sqrt(l)))
    assert r * r == l and r % 2 == 0
    col = np.arange(half)
    hi = 2.0 * np.pi * ((np.arange(r // 2)[:, None] * r * col[None, :]) % l) / l
    lo = 2.0 * np.pi * ((np.arange(r)[:, None] * col[None, :]) % l) / l
    ch, sh = [jnp.asarray(f(hi).astype(np.float32))[:, None, :] for f in (np.cos, np.sin)]
    cl, sl = [jnp.asarray(f(lo).astype(np.float32))[None, :, :] for f in (np.cos, np.sin)]
    cos = (ch * cl - sh * sl).reshape(half, half)
    sin = (sh * cl + ch * sl).reshape(half, half)
    return cos.astype(BF16), sin.astype(BF16)


def _seqdft_kernel(c_ref, s_ref, alt_ref, flip_ref, e_ref, nyq_ref, o_ref, *, scale):
    c = FOUR_GC
    half = e_ref.shape[2]
    ft = flip_ref.shape[0]
    nblk = half // ft
    e = e_ref[0, 0]
    p = _dot(c_ref[...], e[:, :c])
    q = _dot(s_ref[...], e[:, c:])
    zn = nyq_ref[0, 0, 0:1, :]
    k = lax.broadcasted_iota(jnp.int32, (half, c), 0)
    n = jnp.where((k & 1) == 0, zn, -zn)
    o_ref[0, 0:half, :] = ((p + q + n) * scale).astype(o_ref.dtype)
    mir = (p - q + n) * scale
    mir_b = mir.astype(BF16)
    y_half = (_dot(alt_ref[...], e[:, :c])[0:1] + zn) * scale
    row = lax.broadcasted_iota(jnp.int32, (ft, c), 0)
    for jb in range(nblk):
        src = (nblk - 1 - jb) * ft
        hi = _dot(flip_ref[...], mir_b[src:src + ft])
        first = y_half if jb == 0 else mir[src + ft:src + ft + 1]
        hi = jnp.where(row == 0, jnp.broadcast_to(first, (ft, c)), hi)
        o_ref[0, half + jb * ft:half + (jb + 1) * ft, :] = hi.astype(o_ref.dtype)


def _fourier(p):
    b, l, _ = p.shape
    half = l // 2
    ft = min(FLIP_ROWS, half)
    e, nyq = _chdft_fold(p)
    cos, sin = _seq_dft_tables(l)
    alt = jnp.asarray(np.broadcast_to(1.0 - 2.0 * (np.arange(half) % 2), (16, half)).astype(np.float32)).astype(BF16)
    scale = 1.0 / math.sqrt(l * FOUR_GC)
    const2 = lambda bi, g: (0, 0)
    return pl.pallas_call(
        functools.partial(_seqdft_kernel, scale=scale),
        grid=(b, FOUR_GROUPS),
        in_specs=[_resident((half, half), const2),
                  _resident((half, half), const2),
                  _resident((16, half), const2),
                  _resident((ft, ft), const2),
                  pl.BlockSpec((1, 1, half, 2 * FOUR_GC), lambda bi, g: (bi, g, 0, 0)),
                  pl.BlockSpec((1, 1, 8, FOUR_GC), lambda bi, g: (bi, g, 0, 0))],
        out_specs=pl.BlockSpec((1, l, FOUR_GC), lambda bi, g: (bi, 0, g)),
        out_shape=jax.ShapeDtypeStruct((b, l, FOUR_W), BF16),
        compiler_params=_cparams(("parallel", "parallel")),
        name="seqdft",
    )(cos, sin, alt, _flip_matrix(ft), e, nyq)


CONV_CB = 512
_SHIFT_TAPS = tuple(k for k in range(CONV_W) if k != CONV_W // 2)


def _shift_matrix():
    m = np.zeros((len(_SHIFT_TAPS) * CHUNK, CHUNK + 2 * HALO), np.float32)
    t = np.arange(CHUNK)
    for q, k in enumerate(_SHIFT_TAPS):
        m[q * CHUNK + t, HALO - CONV_W // 2 + k + t] = 1.0
    return jnp.asarray(m).astype(BF16)


def _ssd_kernel(*refs, fwd, nc, epilogue):
    if fwd:
        xc_ref, dt_ref, dtb_ref, alog_ref, e_ref, h0_ref = refs[:6]
        rest = refs[6:]
        if epilogue:
            z_ref, yb_ref, dsk_ref, nw_ref, y_ref, hout_ref, st_ref = rest
        else:
            y_ref, hout_ref, st_ref = rest
    else:
        (xm_ref, xp_ref, xn_ref, dt_ref, cw_ref, cb_ref, shift_ref, dtb_ref, alog_ref, e_ref, h0_ref,
         y_ref, xc_ref, hout_ref, st_ref, ext_ref) = refs
    c = pl.program_id(0)
    cc = c if fwd else nc - 1 - c
    T = CHUNK
    nb = dt_ref.shape[0]

    @pl.when(c == 0)
    def _():
        st_ref[...] = h0_ref[...]

    if not fwd:
        @pl.when(cc > 0)
        def _():
            ext_ref[:, 0:HALO, :] = xp_ref[...]

        @pl.when(cc == 0)
        def _():
            ext_ref[:, 0:HALO, :] = jnp.zeros((nb, HALO, XBC), ext_ref.dtype)

        ext_ref[:, HALO:HALO + T, :] = xm_ref[...]

        @pl.when(cc < nc - 1)
        def _():
            ext_ref[:, HALO + T:HALO + T + HALO, :] = xn_ref[...]

        @pl.when(cc == nc - 1)
        def _():
            ext_ref[:, HALO + T:HALO + T + HALO, :] = jnp.zeros((nb, HALO, XBC), ext_ref.dtype)

        shift = shift_ref[...]
        for j in range(XBC // CONV_CB):
            cs = slice(j * CONV_CB, (j + 1) * CONV_CB)
            for bi in range(nb):
                sh = _dot(shift, ext_ref[bi, :, cs])
                acc = cb_ref[:, cs] + xm_ref[bi, :, cs].astype(F32) * cw_ref[CONV_W // 2:CONV_W // 2 + 1, cs]
                for q, k in enumerate(_SHIFT_TAPS):
                    acc = acc + sh[q * T:(q + 1) * T] * cw_ref[k:k + 1, cs]
                xc_ref[bi, :, cs] = _silu(acc).astype(xc_ref.dtype)

    row = lax.broadcasted_iota(jnp.int32, (T, T), 0)
    col = lax.broadcasted_iota(jnp.int32, (T, T), 1)
    causal = (col <= row) if fwd else (col >= row)
    tri = jnp.where(causal, 1.0, 0.0).astype(BF16)
    lane = lax.broadcasted_iota(jnp.int32, (T, 2 * HEAD_DIM), 1)
    hbase = 0 if fwd else SSD_HEADS
    e = e_ref[...]
    a = -jnp.exp(alog_ref[...])

    def head_factors(bi):
        xdt = dt_ref[bi] + dtb_ref[...]
        dt = jnp.maximum(xdt, 0.0) + jnp.log1p(jnp.exp(-jnp.abs(xdt)))
        da = dt * a
        d1 = da.astype(BF16)
        r1 = da - d1.astype(F32)
        d2 = r1.astype(BF16)
        d3 = (r1 - d2.astype(F32)).astype(BF16)
        acs = _dot(tri, d1) + _dot(tri, d2) + _dot(tri, d3)
        tot = acs[T - 1:T, :] if fwd else acs[0:1, :]
        eh, el = _split2(jnp.broadcast_to(jnp.exp(tot), (16, T)))
        ex = _dot(jnp.concatenate([jnp.exp(acs).astype(BF16), (dt * jnp.exp(tot - acs)).astype(BF16), eh, el],
                                  axis=0), e)
        return dict(acs=acs, arow_t=(acs - jnp.log(dt)).T, eacs_x=ex[0:T], w2=ex[T:2 * T],
                    etot_x=ex[2 * T:2 * T + 1] + ex[2 * T + 16:2 * T + 17])

    hf = [head_factors(bi) for bi in range(nb)]

    for g in range(SSD_GROUPS):
        gs = slice(g * GROUP_W, (g + 1) * GROUP_W)
        bs = slice(D_SSD + g * D_STATE, D_SSD + (g + 1) * D_STATE)
        cs_ = slice(D_SSD + (SSD_GROUPS + g) * D_STATE, D_SSD + (SSD_GROUPS + g + 1) * D_STATE)
        for bi in range(nb):
            f = hf[bi]
            bb = xc_ref[bi, :, bs]
            cbf = xc_ref[bi, :, cs_]
            cb = lax.dot_general(cbf, bb, (((1,), (1,)), ((), ())), preferred_element_type=F32)
            st = st_ref[bi, g]
            yoff = _dot(cbf, st.astype(BF16)) * f["eacs_x"][:, gs]
            xb = xc_ref[bi, :, gs]
            ys = []
            for jp in range(HEADS_PER_GROUP // 2):
                ms = []
                for jj in range(2):
                    hc = hbase + g * HEADS_PER_GROUP + 2 * jp + jj
                    diff = f["acs"][:, hc:hc + 1] - f["arow_t"][hc:hc + 1, :]
                    ms.append((cb * jnp.exp(jnp.where(causal, diff, -1e30))).astype(BF16))
                r = _dot(jnp.concatenate(ms, axis=0), xb[:, jp * 2 * HEAD_DIM:(jp + 1) * 2 * HEAD_DIM])
                ys.append(jnp.where(lane < HEAD_DIM, r[:T], r[T:]))
            yg = jnp.concatenate(ys, axis=1) + yoff

            xg = xb.astype(F32)
            xds = (xg * f["w2"][:, gs]).astype(BF16)
            st_ref[bi, g] = st * f["etot_x"][:, gs] + _dot(bb.astype(F32).T.astype(BF16), xds)

            if epilogue:
                yt = yg + yb_ref[bi, :, gs] + dsk_ref[:, gs] * xg
                v = yt * _silu(z_ref[bi, :, gs].astype(F32))
                ms_ = jnp.mean(v * v, axis=-1, keepdims=True)
                y_ref[bi, :, gs] = ((v * lax.rsqrt(ms_ + EPS)) * nw_ref[:, gs]).astype(y_ref.dtype)
            else:
                y_ref[bi, :, gs] = yg

    @pl.when(c == nc - 1)
    def _():
        hout_ref[...] = st_ref[...]


_STATE_TAIL = (SSD_GROUPS, D_STATE, GROUP_W)


def _ssd_bwd(p, pdt, lw, h0):
    b, l, _ = p.shape
    nc = l // CHUNK
    nh = l // HALO
    per = CHUNK // HALO
    cidx = lambda c: nc - 1 - c
    chunk = lambda c: (0, cidx(c), 0)
    const2 = lambda c: (0, 0)
    state = lambda c: (0, 0, 0, 0)
    state_block = (b,) + _STATE_TAIL
    return pl.pallas_call(
        functools.partial(_ssd_kernel, fwd=False, nc=nc, epilogue=False),
        grid=(nc,),
        in_specs=[pl.BlockSpec((b, CHUNK, XBC), chunk),
                  pl.BlockSpec((b, HALO, XBC), lambda c: (0, jnp.maximum(cidx(c) * per - 1, 0), 0)),
                  pl.BlockSpec((b, HALO, XBC), lambda c: (0, jnp.minimum((cidx(c) + 1) * per, nh - 1), 0)),
                  pl.BlockSpec((b, CHUNK, DT_PAD), chunk),
                  _resident((8, XBC), const2),
                  _resident((1, XBC), const2),
                  _resident((len(_SHIFT_TAPS) * CHUNK, CHUNK + 2 * HALO), const2),
                  _resident((1, DT_PAD), const2),
                  _resident((1, DT_PAD), const2),
                  _resident((DT_PAD, D_SSD), const2),
                  _resident(state_block, state)],
        out_specs=[pl.BlockSpec((b, CHUNK, D_SSD), chunk),
                   pl.BlockSpec((b, CHUNK, XBC), chunk),
                   pl.BlockSpec(state_block, state)],
        out_shape=[jax.ShapeDtypeStruct((b, l, D_SSD), F32),
                   jax.ShapeDtypeStruct((b, l, XBC), BF16),
                   jax.ShapeDtypeStruct(state_block, F32)],
        scratch_shapes=[pltpu.VMEM(state_block, F32),
                        pltpu.VMEM((b, CHUNK + 2 * HALO, XBC), BF16)],
        compiler_params=_cparams(("arbitrary",)),
        name="ssd_bwd",
    )(p, p, p, pdt, lw["conv_w"], lw["conv_b"], _shift_matrix(), lw["dt_bias"], lw["a_log"], lw["expand_b"], h0)


def _ssd_fwd(xc, pdt, lw, h0, p=None, yb=None):
    b, l, _ = xc.shape
    nc = l // CHUNK
    chunk = lambda c: (0, c, 0)
    const2 = lambda c: (0, 0)
    state = lambda c: (0, 0, 0, 0)
    state_block = (b,) + _STATE_TAIL
    epilogue = p is not None
    in_specs = [pl.BlockSpec((b, CHUNK, XBC), chunk),
                pl.BlockSpec((b, CHUNK, DT_PAD), chunk),
                _resident((1, DT_PAD), const2),
                _resident((1, DT_PAD), const2),
                _resident((DT_PAD, D_SSD), const2),
                _resident(state_block, state)]
    args = [xc, pdt, lw["dt_bias"], lw["a_log"], lw["expand_f"], h0]
    if epilogue:
        z_blk = P_Z // D_SSD
        in_specs += [pl.BlockSpec((b, CHUNK, D_SSD), lambda c: (0, c, z_blk)),
                     pl.BlockSpec((b, CHUNK, D_SSD), chunk),
                     _resident((1, D_SSD), const2),
                     _resident((1, D_SSD), const2)]
        args += [p, yb, lw["d_skip"], lw["ssd_norm_w"]]
    return pl.pallas_call(
        functools.partial(_ssd_kernel, fwd=True, nc=nc, epilogue=epilogue),
        grid=(nc,),
        in_specs=in_specs,
        out_specs=[pl.BlockSpec((b, CHUNK, D_SSD), chunk),
                   pl.BlockSpec(state_block, state)],
        out_shape=[jax.ShapeDtypeStruct((b, l, D_SSD), BF16 if epilogue else F32),
                   jax.ShapeDtypeStruct(state_block, F32)],
        scratch_shapes=[pltpu.VMEM(state_block, F32)],
        compiler_params=_cparams(("arbitrary",)),
        name="ssd_fwd",
    )(*args)


def _ssd(p, pdt, lw, h0f, h0b):
    yb, xc, hb = _ssd_bwd(p, pdt, lw, h0b)
    y, hf = _ssd_fwd(xc, pdt, lw, h0f, p, yb)
    return y, hf, hb


def _expand_matrix(offset):
    e = np.zeros((DT_PAD, D_SSD), np.float32)
    for h in range(SSD_HEADS):
        e[offset + h, h * HEAD_DIM:(h + 1) * HEAD_DIM] = 1.0
    return jnp.asarray(e).astype(BF16)


def _layer_weights(l, conv_w, conv_b, a_log, dt_bias, d_skip, ssd_norm_w):
    pad_h = (0, DT_PAD - 2 * SSD_HEADS)
    return {
        "conv_w": jnp.pad(conv_w[l], ((0, 8 - CONV_W), (0, 0))),
        "conv_b": conv_b[l].reshape(1, XBC),
        "dt_bias": jnp.pad(dt_bias[l].reshape(-1), pad_h).reshape(1, DT_PAD),
        "a_log": jnp.pad(a_log[l].reshape(-1), pad_h).reshape(1, DT_PAD),
        "d_skip": jnp.repeat(d_skip[l], HEAD_DIM).reshape(1, D_SSD),
        "ssd_norm_w": ssd_norm_w[l].reshape(1, D_SSD),
        "expand_f": _expand_matrix(0),
        "expand_b": _expand_matrix(SSD_HEADS),
    }


def _mixer(x, nw, sc, sh, g_m, wts, lw, layer, grid_rows, h0f, h0b):
    b, l, dm = x.shape
    tok = (lambda a: a.reshape(1, b * l, a.shape[-1])) if sc.shape[0] == 1 else (lambda a: a)
    seq = lambda a: a.reshape(b, l, a.shape[-1])
    p, pdt = _inproj(tok(x), nw, sc, sh, wts["w_main"], wts["w_dt"], layer, P_TOT, 1024, 1024)
    y_n, hf, hb = _ssd(seq(p), seq(pdt), lw, h0f, h0b)
    d = _pool_diff(seq(p), grid_rows)
    f = _fourier(seq(p))
    x = seq(_merge_out(tok(d), tok(f), tok(y_n), p, tok(x), g_m, wts, layer, 256))
    return x, hf, hb


def kernel(x, c, ctx, c_ctx, w_ada, b_ada, norm_mix_w, norm_ffn_w, w_in, conv_w, conv_b, a_log, dt_bias,
           d_skip, ssd_norm_w, w_ssd_out, w_pool, pool_scale, w_fourier, w_out, w_ffn_gate, w_ffn_up,
           w_ffn_down, final_norm_w):
    b, seq, d = x.shape
    rows = seq // GRID_W
    mod = _adaln(jnp.concatenate([c, c_ctx[None, :]], axis=0), w_ada, b_ada)
    h0 = jnp.zeros((b,) + _STATE_TAIL, F32)
    lc = ctx.shape[1]
    w_main = _winprep(w_in)
    w_dt = jnp.pad(w_in[:, :, XBC:SSD_IN], ((0, 0), (0, 0), (0, DT_PAD - 2 * SSD_HEADS))).astype(BF16)
    wts = {
        "w_main": w_main,
        "w_dt": w_dt,
        "w_pool": _to_bf16(w_pool.reshape(DEPTH, POOL_W, POOL_OUT)),
        "pool_scale": pool_scale.reshape(DEPTH, 1, D_MODEL),
        "w_fourier": _to_bf16(w_fourier),
        "w_ssd_out": _to_bf16(w_ssd_out),
        "w_out": _to_bf16(w_out),
    }
    wg, wu, wd = _to_bf16(w_ffn_gate), _to_bf16(w_ffn_up), _to_bf16(w_ffn_down)
    for l in range(DEPTH):
        last = l == DEPTH - 1
        lw = _layer_weights(l, conv_w, conv_b, a_log, dt_bias, d_skip, ssd_norm_w)
        sh_m, sc_m, g_m, sh_f, sc_f, g_f = [mod[l, :b, i * d:(i + 1) * d].reshape(b, 1, d) for i in range(6)]
        csh_m, csc_m, cg_m, csh_f, csc_f, cg_f = [mod[l, b, i * d:(i + 1) * d].reshape(1, 1, d) for i in range(6)]

        if last:
            pc, pdtc = _inproj(ctx.reshape(1, b * lc, d), norm_mix_w[l], csc_m, csh_m, w_main, w_dt, l,
                               XBC, 1024, 1024)
            pdtc = pdtc.reshape(b, lc, DT_PAD)
            _, xcc, hb = _ssd_bwd(pc.reshape(b, lc, XBC), pdtc, lw, h0)
            _, hf = _ssd_fwd(xcc, pdtc, lw, h0)
        else:
            ctx, hf, hb = _mixer(ctx, norm_mix_w[l], csc_m, csh_m, cg_m, wts, lw, l, None, h0, h0)
            ctx = _ffn(ctx.reshape(1, b * lc, d), norm_ffn_w[l], csc_f, csh_f, wg, wu, wd, l,
                       cg_f, final_norm_w, False, 512, 512).reshape(b, lc, d)

        x, _, _ = _mixer(x, norm_mix_w[l], sc_m, sh_m, g_m, wts, lw, l, rows, hf, hb)
        x = _ffn(x, norm_ffn_w[l], sc_f, sh_f, wg, wu, wd, l, g_f, final_norm_w, last, 512, 512)
    return x
```

```python
import functools
import math

import numpy as np
import jax
import jax.numpy as jnp
from jax import lax
from jax.experimental import pallas as pl
from jax.experimental.pallas import tpu as pltpu

F32 = jnp.float32
BF16 = jnp.bfloat16

D_MODEL = 2048
DEPTH = 2
GRID_W = 64
EPS = 1e-6

POOL_GROUPS = 4
POOL_WINDOWS = (2, 4, 8, 16)
POOL_W = D_MODEL // 2
POOL_GC = POOL_W // POOL_GROUPS
POOL_OUT = D_MODEL // POOL_GROUPS

FOUR_GROUPS = 4
FOUR_W = D_MODEL // 2
FOUR_GC = FOUR_W // FOUR_GROUPS

D_SSD = D_MODEL
HEAD_DIM = 64
SSD_HEADS = D_SSD // HEAD_DIM
SSD_GROUPS = 4
HEADS_PER_GROUP = SSD_HEADS // SSD_GROUPS
GROUP_W = HEADS_PER_GROUP * HEAD_DIM
D_STATE = 128
CONV_W = 5
CHUNK = 128

XBC = D_SSD + 2 * SSD_GROUPS * D_STATE
SSD_IN = XBC + 2 * SSD_HEADS
Z_OFF = SSD_IN
POOL_OFF = Z_OFF + D_SSD
FOUR_OFF = POOL_OFF + POOL_W
GATE_OFF = FOUR_OFF + FOUR_W
N_IN = GATE_OFF + 3 * D_MODEL

P_XBC = 0
P_POOL = XBC
P_Z = P_POOL + POOL_W
P_GATE = P_Z + D_SSD
P_FOUR = P_GATE + 3 * D_MODEL
P_TOT = P_FOUR + FOUR_W
DT_PAD = 128

V7X_VMEM_LIMIT = 56 * 1024 * 1024
HALO = 16

INPROJ_TM, INPROJ_TN = 1024, 1024
MERGE_TM = 256
FFN_TM, FFN_TF = 512, 512
ADALN_TN = 1024


def _cparams(sem):
    return pltpu.CompilerParams(dimension_semantics=sem, vmem_limit_bytes=V7X_VMEM_LIMIT)


def _resident(shape, index_map):
    return pl.BlockSpec(shape, index_map, pipeline_mode=pl.Buffered(1))


def _split2(v):
    hi = v.astype(BF16)
    lo = (v - hi.astype(F32)).astype(BF16)
    return hi, lo


def _dot(a, b):
    return jnp.dot(a, b, preferred_element_type=F32)


def _sigmoid(v):
    return 0.5 * jnp.tanh(0.5 * v) + 0.5


def _silu(v):
    return v * _sigmoid(v)


def _np_split2(m):
    m = jnp.asarray(np.asarray(m, np.float32))
    hi = m.astype(BF16)
    lo = (m - hi.astype(F32)).astype(BF16)
    return hi, lo


CAST_ROWS = 256
WIN_CHUNK = 1024


def _cast_kernel(w_ref, o_ref):
    o_ref[...] = w_ref[...].astype(BF16)


def _to_bf16(w):
    depth, k, n = w.shape
    return pl.pallas_call(
        _cast_kernel,
        grid=(depth, k // CAST_ROWS),
        in_specs=[pl.BlockSpec((1, CAST_ROWS, n), lambda l, i: (l, i, 0))],
        out_specs=pl.BlockSpec((1, CAST_ROWS, n), lambda l, i: (l, i, 0)),
        out_shape=jax.ShapeDtypeStruct((depth, k, n), BF16),
        compiler_params=_cparams(("parallel", "parallel")),
        name="cast_bf16",
    )(w)


_WIN_SEGMENTS = ((P_XBC, 0, XBC), (P_POOL, POOL_OFF, POOL_W), (P_Z, Z_OFF, D_SSD),
                 (P_GATE, GATE_OFF, 3 * D_MODEL), (P_FOUR, FOUR_OFF, FOUR_W))


def _win_src(j):
    src = jnp.int32(0)
    for dst0, src0, width in _WIN_SEGMENTS:
        first = dst0 // WIN_CHUNK
        inside = (j >= first) & (j < first + width // WIN_CHUNK)
        src = jnp.where(inside, src0 + (j - first) * WIN_CHUNK, src)
    return pl.multiple_of(src, 2 * SSD_HEADS)


def _winprep_kernel(wt_ref, o_ref):
    o_ref[0] = wt_ref[0].T.astype(BF16)


def _winprep(w_in):
    depth, d, n = w_in.shape
    wt = jnp.swapaxes(w_in, 1, 2)
    return pl.pallas_call(
        _winprep_kernel,
        grid=(depth, P_TOT // WIN_CHUNK),
        in_specs=[pl.BlockSpec((pl.Element(1), pl.Element(WIN_CHUNK), pl.Element(d)),
                               lambda l, j: (l, _win_src(j), 0))],
        out_specs=pl.BlockSpec((1, d, WIN_CHUNK), lambda l, j: (l, 0, j)),
        out_shape=jax.ShapeDtypeStruct((depth, d, P_TOT), BF16),
        compiler_params=_cparams(("parallel", "parallel")),
        name="winprep",
    )(wt)


ADA_ROWS = 8


def _adaln_kernel(ct_ref, w_ref, b_ref, o_ref, *, n_rows):
    s = _silu(ct_ref[...])
    w = w_ref[0]
    tn = w.shape[1]
    rows = [jnp.sum(w * s[:, r:r + 1], axis=0, keepdims=True) for r in range(n_rows)]
    rows.append(jnp.zeros((ADA_ROWS - n_rows, tn), F32))
    o_ref[0] = jnp.concatenate(rows, axis=0) + b_ref[0]


def _adaln(cc, w_ada, b_ada):
    depth, d, n = w_ada.shape
    tn = ADALN_TN
    n_rows = cc.shape[0]
    ct = jnp.pad(cc, ((0, ADA_ROWS - n_rows), (0, 0))).T
    return pl.pallas_call(
        functools.partial(_adaln_kernel, n_rows=n_rows),
        grid=(depth, n // tn),
        in_specs=[pl.BlockSpec((d, ADA_ROWS), lambda l, j: (0, 0)),
                  pl.BlockSpec((1, d, tn), lambda l, j: (l, 0, j)),
                  pl.BlockSpec((1, 1, tn), lambda l, j: (l, 0, j))],
        out_specs=pl.BlockSpec((1, ADA_ROWS, tn), lambda l, j: (l, 0, j)),
        out_shape=jax.ShapeDtypeStruct((depth, ADA_ROWS, n), F32),
        compiler_params=_cparams(("parallel", "parallel")),
        name="adaln",
    )(ct, w_ada, b_ada.reshape(depth, 1, n))


def _norm_mod(x, nw, sc, sh):
    ms = jnp.mean(x * x, axis=-1, keepdims=True)
    return (x * lax.rsqrt(ms + EPS)) * nw * (1.0 + sc) + sh


def _inproj_kernel(x_ref, nw_ref, sc_ref, sh_ref, w_ref, wdt_ref, o_ref, dt_ref, h_ref):
    @pl.when(pl.program_id(2) == 0)
    def _():
        h = _norm_mod(x_ref[0], nw_ref[...], sc_ref[0], sh_ref[0]).astype(BF16)
        h_ref[...] = h
        dt_ref[0] = _dot(h, wdt_ref[0])

    o_ref[0] = _dot(h_ref[...], w_ref[0]).astype(o_ref.dtype)


def _inproj(x, nw, sc, sh, w, wdt, layer, n_out, tm, tn):
    b, l, d = x.shape
    tm = min(tm, l)
    return pl.pallas_call(
        _inproj_kernel,
        grid=(b, l // tm, n_out // tn),
        in_specs=[pl.BlockSpec((1, tm, d), lambda bi, i, j: (bi, i, 0)),
                  pl.BlockSpec((1, d), lambda bi, i, j: (0, 0)),
                  pl.BlockSpec((1, 1, d), lambda bi, i, j: (bi, 0, 0)),
                  pl.BlockSpec((1, 1, d), lambda bi, i, j: (bi, 0, 0)),
                  pl.BlockSpec((1, d, tn), lambda bi, i, j: (layer, 0, j)),
                  _resident((1, d, DT_PAD), lambda bi, i, j: (layer, 0, 0))],
        out_specs=[pl.BlockSpec((1, tm, tn), lambda bi, i, j: (bi, i, j)),
                   pl.BlockSpec((1, tm, DT_PAD), lambda bi, i, j: (bi, i, 0))],
        out_shape=[jax.ShapeDtypeStruct((b, l, n_out), BF16),
                   jax.ShapeDtypeStruct((b, l, DT_PAD), F32)],
        scratch_shapes=[pltpu.VMEM((tm, d), BF16)],
        compiler_params=_cparams(("parallel", "parallel", "arbitrary")),
        name="inproj",
    )(x, nw.reshape(1, d), sc, sh, w, wdt)


def _merge_kernel(d_ref, f_ref, yn_ref, gp_ref, gf_ref, gs_ref, x_ref, gm_ref,
                  wp_ref, ps_ref, wf_ref, ws_ref, wo_ref, o_ref):
    d = d_ref[0]
    y_pool = jnp.concatenate(
        [_dot(d[:, g * POOL_GC:(g + 1) * POOL_GC], wp_ref[0, g * POOL_GC:(g + 1) * POOL_GC, :])
         for g in range(POOL_GROUPS)], axis=1)
    m = _sigmoid(gp_ref[0].astype(F32)) * (y_pool * ps_ref[0])
    m = m + _sigmoid(gf_ref[0].astype(F32)) * _dot(f_ref[0], wf_ref[0])
    m = m + _sigmoid(gs_ref[0].astype(F32)) * _dot(yn_ref[0], ws_ref[0])
    o_ref[0] = x_ref[0] + gm_ref[0] * _dot(m.astype(BF16), wo_ref[0])


def _merge_out(d, f, yn, p, x, g_m, wts, layer, tm):
    b, l, dm = x.shape
    tm = min(tm, l)
    gblk = P_GATE // dm
    row = lambda bi, i: (bi, i, 0)
    lyr = lambda bi, i: (layer, 0, 0)
    return pl.pallas_call(
        _merge_kernel,
        grid=(b, l // tm),
        in_specs=[pl.BlockSpec((1, tm, POOL_W), row),
                  pl.BlockSpec((1, tm, FOUR_W), row),
                  pl.BlockSpec((1, tm, D_SSD), row),
                  pl.BlockSpec((1, tm, dm), lambda bi, i: (bi, i, gblk)),
                  pl.BlockSpec((1, tm, dm), lambda bi, i: (bi, i, gblk + 1)),
                  pl.BlockSpec((1, tm, dm), lambda bi, i: (bi, i, gblk + 2)),
                  pl.BlockSpec((1, tm, dm), row),
                  pl.BlockSpec((1, 1, dm), lambda bi, i: (bi, 0, 0)),
                  _resident((1, POOL_W, POOL_OUT), lyr),
                  _resident((1, 1, dm), lyr),
                  _resident((1, FOUR_W, dm), lyr),
                  _resident((1, D_SSD, dm), lyr),
                  _resident((1, dm, dm), lyr)],
        out_specs=pl.BlockSpec((1, tm, dm), row),
        out_shape=jax.ShapeDtypeStruct((b, l, dm), F32),
        compiler_params=_cparams(("parallel", "parallel")),
        name="merge_out",
    )(d, f, yn, p, p, p, x, g_m, wts["w_pool"], wts["pool_scale"], wts["w_fourier"], wts["w_ssd_out"],
      wts["w_out"])


def _ffn_kernel(x_ref, nw_ref, sc_ref, sh_ref, wg_ref, wu_ref, wd_ref, gate_ref, fnw_ref,
                o_ref, h_ref, acc_ref, *, final_norm):
    j = pl.program_id(2)

    @pl.when(j == 0)
    def _():
        h_ref[...] = _norm_mod(x_ref[0], nw_ref[...], sc_ref[0], sh_ref[0]).astype(BF16)
        acc_ref[...] = jnp.zeros_like(acc_ref)

    h = h_ref[...]
    a = _silu(_dot(h, wg_ref[0])) * _dot(h, wu_ref[0])
    acc_ref[...] += _dot(a.astype(BF16), wd_ref[0])

    @pl.when(j == pl.num_programs(2) - 1)
    def _():
        o = x_ref[0] + gate_ref[0] * acc_ref[...]
        if final_norm:
            ms = jnp.mean(o * o, axis=-1, keepdims=True)
            o = (o * lax.rsqrt(ms + EPS)) * fnw_ref[...]
        o_ref[0] = o


def _ffn(x, nw, sc, sh, wg, wu, wd, layer, gate, fnw, final_norm, tm, tf):
    b, l, d = x.shape
    ff = wg.shape[2]
    tm = min(tm, l)
    vec = lambda bi, i, j: (bi, 0, 0)
    return pl.pallas_call(
        functools.partial(_ffn_kernel, final_norm=final_norm),
        grid=(b, l // tm, ff // tf),
        in_specs=[pl.BlockSpec((1, tm, d), lambda bi, i, j: (bi, i, 0)),
                  pl.BlockSpec((1, d), lambda bi, i, j: (0, 0)),
                  pl.BlockSpec((1, 1, d), vec),
                  pl.BlockSpec((1, 1, d), vec),
                  pl.BlockSpec((1, d, tf), lambda bi, i, j: (layer, 0, j)),
                  pl.BlockSpec((1, d, tf), lambda bi, i, j: (layer, 0, j)),
                  pl.BlockSpec((1, tf, d), lambda bi, i, j: (layer, j, 0)),
                  pl.BlockSpec((1, 1, d), vec),
                  pl.BlockSpec((1, d), lambda bi, i, j: (0, 0))],
        out_specs=pl.BlockSpec((1, tm, d), lambda bi, i, j: (bi, i, 0)),
        out_shape=jax.ShapeDtypeStruct((b, l, d), F32),
        scratch_shapes=[pltpu.VMEM((tm, d), BF16), pltpu.VMEM((tm, d), F32)],
        compiler_params=_cparams(("parallel", "parallel", "arbitrary")),
        name="ffn",
    )(x, nw.reshape(1, d), sc, sh, wg, wu, wd, gate, fnw.reshape(1, d))


def _box_matrix(n, w):
    idx = np.arange(n)
    lo = np.clip(idx - w // 2, 0, n)
    hi = np.clip(idx + (w - w // 2), 0, n)
    m = ((idx[None, :] >= lo[:, None]) & (idx[None, :] < hi[:, None])).astype(np.float64)
    return m / (hi - lo)[:, None]


POOL_TB = 256
POOL_PAD = 16


def _pool_kernel(u_ref, mh_ref, ml_ref, o_ref, *scratch, grid_rows):
    g = pl.program_id(1)
    l = u_ref.shape[1]
    mh = mh_ref[0]
    ml = ml_ref[0]
    if grid_rows is None:
        v = u_ref[0]
        o_ref[0] = (_dot(mh, v) + _dot(ml, v) - v.astype(F32)).astype(o_ref.dtype)
        return

    cp_ref, = scratch
    pad = POOL_PAD * GRID_W
    cp_ref[0:pad, :] = jnp.zeros((pad, POOL_GC), F32)
    cp_ref[pad + l:pad + l + pad, :] = jnp.zeros((pad, POOL_GC), F32)
    for i in range(l // POOL_TB):
        v = u_ref[0, i * POOL_TB:(i + 1) * POOL_TB, :]
        cp_ref[pad + i * POOL_TB:pad + (i + 1) * POOL_TB, :] = _dot(mh, v) + _dot(ml, v)

    for gi, w in enumerate(POOL_WINDOWS):
        lo_off, hi_off = -(w // 2), w - w // 2

        @pl.when(g == gi)
        def _(lo_off=lo_off, hi_off=hi_off):
            def slab(row):
                return cp_ref[pl.ds(pl.multiple_of(pad + row * GRID_W, GRID_W), GRID_W), :]

            def body(r, s):
                s = s + slab(r + hi_off - 1) - slab(r + lo_off - 1)
                cnt = jnp.minimum(r + hi_off, grid_rows) - jnp.maximum(r + lo_off, 0)
                cntv = jnp.full((GRID_W, POOL_GC), cnt, jnp.int32).astype(F32)
                t0 = pl.multiple_of(r * GRID_W, GRID_W)
                v = u_ref[0, pl.ds(t0, GRID_W), :].astype(F32)
                o_ref[0, pl.ds(t0, GRID_W), :] = (s / cntv - v).astype(o_ref.dtype)
                return s

            s0 = jnp.zeros((GRID_W, POOL_GC), F32)
            for o in range(lo_off - 1, hi_off - 1):
                s0 = s0 + cp_ref[pad + o * GRID_W:pad + (o + 1) * GRID_W, :]
            lax.fori_loop(0, grid_rows, body, s0)


def _pool_diff(p, grid_rows):
    b, l, _ = p.shape
    mats = []
    for w in POOL_WINDOWS:
        if grid_rows is None:
            mats.append(_box_matrix(l, w))
        else:
            mats.append(np.kron(np.eye(POOL_TB // GRID_W), _box_matrix(GRID_W, w)))
    mh, ml = _np_split2(np.stack(mats))
    tb = mats[0].shape[0]
    scratch = []
    if grid_rows is not None:
        scratch = [pltpu.VMEM((l + 2 * POOL_PAD * GRID_W, POOL_GC), F32)]
    cblk = P_POOL // POOL_GC
    return pl.pallas_call(
        functools.partial(_pool_kernel, grid_rows=grid_rows),
        grid=(b, POOL_GROUPS),
        in_specs=[pl.BlockSpec((1, l, POOL_GC), lambda bi, g: (bi, 0, cblk + g)),
                  pl.BlockSpec((1, tb, tb), lambda bi, g: (g, 0, 0)),
                  pl.BlockSpec((1, tb, tb), lambda bi, g: (g, 0, 0))],
        out_specs=pl.BlockSpec((1, l, POOL_GC), lambda bi, g: (bi, 0, g)),
        out_shape=jax.ShapeDtypeStruct((b, l, POOL_W), BF16),
        scratch_shapes=scratch,
        compiler_params=_cparams(("parallel", "parallel")),
        name="pool_diff",
    )(p, mh, ml)


FLIP_ROWS = 512


def _flip_matrix(t):
    m = np.zeros((t, t), np.float32)
    m[np.arange(1, t), t - np.arange(1, t)] = 1.0
    return jnp.asarray(m).astype(BF16)


def _chdft_fold_kernel(u_ref, ua_ref, ub_ref, w_ref, flip_ref, o_ref, nyq_ref):
    i = pl.program_id(2)
    c = FOUR_GC
    tm = u_ref.shape[1]
    w = w_ref[...]
    urev = _dot(flip_ref[...], ua_ref[0])
    row = lax.broadcasted_iota(jnp.int32, (tm, c), 0)
    urev = jnp.where(row == 0, jnp.broadcast_to(ub_ref[0, 0:1, :].astype(F32), (tm, c)), urev).astype(BF16)
    z = _dot(u_ref[0], w)
    zrev = _dot(urev, w)
    first = (row + jnp.minimum(i, 1)) == 0
    o_ref[0, 0, :, :c] = (z[:, :c] + jnp.where(first, 0.0, zrev[:, :c])).astype(o_ref.dtype)
    o_ref[0, 0, :, c:] = (z[:, c:] - zrev[:, c:]).astype(o_ref.dtype)

    @pl.when(i == 0)
    def _():
        nyq_ref[0, 0] = zrev[0:8, :c]


def _chdft_fold(p):
    b, l, _ = p.shape
    half = l // 2
    tm = min(FLIP_ROWS, half)
    nb = l // tm
    k = np.arange(FOUR_GC)
    ang = -2.0 * np.pi * ((k[:, None] * k[None, :]) % FOUR_GC) / FOUR_GC
    w = jnp.asarray(np.concatenate([np.cos(ang), np.sin(ang)], axis=1).astype(np.float32)).astype(BF16)
    cblk = P_FOUR // FOUR_GC
    return pl.pallas_call(
        _chdft_fold_kernel,
        grid=(b, FOUR_GROUPS, half // tm),
        in_specs=[pl.BlockSpec((1, tm, FOUR_GC), lambda bi, g, i: (bi, i, cblk + g)),
                  pl.BlockSpec((1, tm, FOUR_GC), lambda bi, g, i: (bi, nb - 1 - i, cblk + g)),
                  pl.BlockSpec((1, tm, FOUR_GC), lambda bi, g, i: (bi, jnp.where(i == 0, nb // 2, nb - i), cblk + g)),
                  _resident((FOUR_GC, 2 * FOUR_GC), lambda bi, g, i: (0, 0)),
                  _resident((tm, tm), lambda bi, g, i: (0, 0))],
        out_specs=[pl.BlockSpec((1, 1, tm, 2 * FOUR_GC), lambda bi, g, i: (bi, g, i, 0)),
                   pl.BlockSpec((1, 1, 8, FOUR_GC), lambda bi, g, i: (bi, g, 0, 0))],
        out_shape=[jax.ShapeDtypeStruct((b, FOUR_GROUPS, half, 2 * FOUR_GC), BF16),
                   jax.ShapeDtypeStruct((b, FOUR_GROUPS, 8, FOUR_GC), F32)],
        compiler_params=_cparams(("parallel", "parallel", "arbitrary")),
        name="chdft",
    )(p, p, p, w, _flip_matrix(tm))


def _seq_dft_tables(l):
    half = l // 2
    r = int(round(math.sqrt(l)))
    assert r * r == l and r % 2 == 0
    col = np.arange(half)
    hi = 2.0 * np.pi * ((np.arange(r // 2)[:, None] * r * col[None, :]) % l) / l
    lo = 2.0 * np.pi * ((np.arange(r)[:, None] * col[None, :]) % l) / l
    ch, sh = [jnp.asarray(f(hi).astype(np.float32))[:, None, :] for f in (np.cos, np.sin)]
    cl, sl = [jnp.asarray(f(lo).astype(np.float32))[None, :, :] for f in (np.cos, np.sin)]
    cos = (ch * cl - sh * sl).reshape(half, half)
    sin = (sh * cl + ch * sl).reshape(half, half)
    return cos.astype(BF16), sin.astype(BF16)


def _seqdft_kernel(c_ref, s_ref, alt_ref, flip_ref, e_ref, nyq_ref, o_ref, *, scale):
    c = FOUR_GC
    half = e_ref.shape[2]
    ft = flip_ref.shape[0]
    nblk = half // ft
    e = e_ref[0, 0]
    p = _dot(c_ref[...], e[:, :c])
    q = _dot(s_ref[...], e[:, c:])
    zn = nyq_ref[0, 0, 0:1, :]
    k = lax.broadcasted_iota(jnp.int32, (half, c), 0)
    n = jnp.where((k & 1) == 0, zn, -zn)
    o_ref[0, 0:half, :] = ((p + q + n) * scale).astype(o_ref.dtype)
    mir = (p - q + n) * scale
    mir_b = mir.astype(BF16)
    y_half = (_dot(alt_ref[...], e[:, :c])[0:1] + zn) * scale
    row = lax.broadcasted_iota(jnp.int32, (ft, c), 0)
    for jb in range(nblk):
        src = (nblk - 1 - jb) * ft
        hi = _dot(flip_ref[...], mir_b[src:src + ft])
        first = y_half if jb == 0 else mir[src + ft:src + ft + 1]
        hi = jnp.where(row == 0, jnp.broadcast_to(first, (ft, c)), hi)
        o_ref[0, half + jb * ft:half + (jb + 1) * ft, :] = hi.astype(o_ref.dtype)


def _fourier(p):
    b, l, _ = p.shape
    half = l // 2
    ft = min(FLIP_ROWS, half)
    e, nyq = _chdft_fold(p)
    cos, sin = _seq_dft_tables(l)
    alt = jnp.asarray(np.broadcast_to(1.0 - 2.0 * (np.arange(half) % 2), (16, half)).astype(np.float32)).astype(BF16)
    scale = 1.0 / math.sqrt(l * FOUR_GC)
    const2 = lambda bi, g: (0, 0)
    return pl.pallas_call(
        functools.partial(_seqdft_kernel, scale=scale),
        grid=(b, FOUR_GROUPS),
        in_specs=[_resident((half, half), const2),
                  _resident((half, half), const2),
                  _resident((16, half), const2),
                  _resident((ft, ft), const2),
                  pl.BlockSpec((1, 1, half, 2 * FOUR_GC), lambda bi, g: (bi, g, 0, 0)),
                  pl.BlockSpec((1, 1, 8, FOUR_GC), lambda bi, g: (bi, g, 0, 0))],
        out_specs=pl.BlockSpec((1, l, FOUR_GC), lambda bi, g: (bi, 0, g)),
        out_shape=jax.ShapeDtypeStruct((b, l, FOUR_W), BF16),
        compiler_params=_cparams(("parallel", "parallel")),
        name="seqdft",
    )(cos, sin, alt, _flip_matrix(ft), e, nyq)


CONV_CB = 512
_SHIFT_TAPS = tuple(k for k in range(CONV_W) if k != CONV_W // 2)


def _shift_matrix():
    m = np.zeros((len(_SHIFT_TAPS) * CHUNK, CHUNK + 2 * HALO), np.float32)
    t = np.arange(CHUNK)
    for q, k in enumerate(_SHIFT_TAPS):
        m[q * CHUNK + t, HALO - CONV_W // 2 + k + t] = 1.0
    return jnp.asarray(m).astype(BF16)


def _ssd_kernel(*refs, fwd, nc, epilogue):
    if fwd:
        xc_ref, dt_ref, dtb_ref, alog_ref, e_ref, h0_ref = refs[:6]
        rest = refs[6:]
        if epilogue:
            z_ref, yb_ref, dsk_ref, nw_ref, y_ref, hout_ref, st_ref = rest
        else:
            y_ref, hout_ref, st_ref = rest
    else:
        (xm_ref, xp_ref, xn_ref, dt_ref, cw_ref, cb_ref, shift_ref, dtb_ref, alog_ref, e_ref, h0_ref,
         y_ref, xc_ref, hout_ref, st_ref, ext_ref) = refs
    c = pl.program_id(0)
    cc = c if fwd else nc - 1 - c
    T = CHUNK
    nb = dt_ref.shape[0]

    @pl.when(c == 0)
    def _():
        st_ref[...] = h0_ref[...]

    if not fwd:
        @pl.when(cc > 0)
        def _():
            ext_ref[:, 0:HALO, :] = xp_ref[...]

        @pl.when(cc == 0)
        def _():
            ext_ref[:, 0:HALO, :] = jnp.zeros((nb, HALO, XBC), ext_ref.dtype)

        ext_ref[:, HALO:HALO + T, :] = xm_ref[...]

        @pl.when(cc < nc - 1)
        def _():
            ext_ref[:, HALO + T:HALO + T + HALO, :] = xn_ref[...]

        @pl.when(cc == nc - 1)
        def _():
            ext_ref[:, HALO + T:HALO + T + HALO, :] = jnp.zeros((nb, HALO, XBC), ext_ref.dtype)

        shift = shift_ref[...]
        for j in range(XBC // CONV_CB):
            cs = slice(j * CONV_CB, (j + 1) * CONV_CB)
            for bi in range(nb):
                sh = _dot(shift, ext_ref[bi, :, cs])
                acc = cb_ref[:, cs] + xm_ref[bi, :, cs].astype(F32) * cw_ref[CONV_W // 2:CONV_W // 2 + 1, cs]
                for q, k in enumerate(_SHIFT_TAPS):
                    acc = acc + sh[q * T:(q + 1) * T] * cw_ref[k:k + 1, cs]
                xc_ref[bi, :, cs] = _silu(acc).astype(xc_ref.dtype)

    row = lax.broadcasted_iota(jnp.int32, (T, T), 0)
    col = lax.broadcasted_iota(jnp.int32, (T, T), 1)
    causal = (col <= row) if fwd else (col >= row)
    tri = jnp.where(causal, 1.0, 0.0).astype(BF16)
    lane = lax.broadcasted_iota(jnp.int32, (T, 2 * HEAD_DIM), 1)
    hbase = 0 if fwd else SSD_HEADS
    e = e_ref[...]
    a = -jnp.exp(alog_ref[...])

    def head_factors(bi):
        xdt = dt_ref[bi] + dtb_ref[...]
        dt = jnp.maximum(xdt, 0.0) + jnp.log1p(jnp.exp(-jnp.abs(xdt)))
        da = dt * a
        d1 = da.astype(BF16)
        r1 = da - d1.astype(F32)
        d2 = r1.astype(BF16)
        d3 = (r1 - d2.astype(F32)).astype(BF16)
        acs = _dot(tri, d1) + _dot(tri, d2) + _dot(tri, d3)
        tot = acs[T - 1:T, :] if fwd else acs[0:1, :]
        eh, el = _split2(jnp.broadcast_to(jnp.exp(tot), (16, T)))
        ex = _dot(jnp.concatenate([jnp.exp(acs).astype(BF16), (dt * jnp.exp(tot - acs)).astype(BF16), eh, el],
                                  axis=0), e)
        return dict(acs=acs, arow_t=(acs - jnp.log(dt)).T, eacs_x=ex[0:T], w2=ex[T:2 * T],
                    etot_x=ex[2 * T:2 * T + 1] + ex[2 * T + 16:2 * T + 17])

    hf = [head_factors(bi) for bi in range(nb)]

    for g in range(SSD_GROUPS):
        gs = slice(g * GROUP_W, (g + 1) * GROUP_W)
        bs = slice(D_SSD + g * D_STATE, D_SSD + (g + 1) * D_STATE)
        cs_ = slice(D_SSD + (SSD_GROUPS + g) * D_STATE, D_SSD + (SSD_GROUPS + g + 1) * D_STATE)
        for bi in range(nb):
            f = hf[bi]
            bb = xc_ref[bi, :, bs]
            cbf = xc_ref[bi, :, cs_]
            cb = lax.dot_general(cbf, bb, (((1,), (1,)), ((), ())), preferred_element_type=F32)
            st = st_ref[bi, g]
            yoff = _dot(cbf, st.astype(BF16)) * f["eacs_x"][:, gs]
            xb = xc_ref[bi, :, gs]
            ys = []
            for jp in range(HEADS_PER_GROUP // 2):
                ms = []
                for jj in range(2):
                    hc = hbase + g * HEADS_PER_GROUP + 2 * jp + jj
                    diff = f["acs"][:, hc:hc + 1] - f["arow_t"][hc:hc + 1, :]
                    ms.append((cb * jnp.exp(jnp.where(causal, diff, -1e30))).astype(BF16))
                r = _dot(jnp.concatenate(ms, axis=0), xb[:, jp * 2 * HEAD_DIM:(jp + 1) * 2 * HEAD_DIM])
                ys.append(jnp.where(lane < HEAD_DIM, r[:T], r[T:]))
            yg = jnp.concatenate(ys, axis=1) + yoff

            xg = xb.astype(F32)
            xds = (xg * f["w2"][:, gs]).astype(BF16)
            st_ref[bi, g] = st * f["etot_x"][:, gs] + _dot(bb.astype(F32).T.astype(BF16), xds)

            if epilogue:
                yt = yg + yb_ref[bi, :, gs] + dsk_ref[:, gs] * xg
                v = yt * _silu(z_ref[bi, :, gs].astype(F32))
                ms_ = jnp.mean(v * v, axis=-1, keepdims=True)
                y_ref[bi, :, gs] = ((v * lax.rsqrt(ms_ + EPS)) * nw_ref[:, gs]).astype(y_ref.dtype)
            else:
                y_ref[bi, :, gs] = yg

    @pl.when(c == nc - 1)
    def _():
        hout_ref[...] = st_ref[...]


_STATE_TAIL = (SSD_GROUPS, D_STATE, GROUP_W)


def _ssd_bwd(p, pdt, lw, h0):
    b, l, _ = p.shape
    nc = l // CHUNK
    nh = l // HALO
    per = CHUNK // HALO
    cidx = lambda c: nc - 1 - c
    chunk = lambda c: (0, cidx(c), 0)
    const2 = lambda c: (0, 0)
    state = lambda c: (0, 0, 0, 0)
    state_block = (b,) + _STATE_TAIL
    return pl.pallas_call(
        functools.partial(_ssd_kernel, fwd=False, nc=nc, epilogue=False),
        grid=(nc,),
        in_specs=[pl.BlockSpec((b, CHUNK, XBC), chunk),
                  pl.BlockSpec((b, HALO, XBC), lambda c: (0, jnp.maximum(cidx(c) * per - 1, 0), 0)),
                  pl.BlockSpec((b, HALO, XBC), lambda c: (0, jnp.minimum((cidx(c) + 1) * per, nh - 1), 0)),
                  pl.BlockSpec((b, CHUNK, DT_PAD), chunk),
                  _resident((8, XBC), const2),
                  _resident((1, XBC), const2),
                  _resident((len(_SHIFT_TAPS) * CHUNK, CHUNK + 2 * HALO), const2),
                  _resident((1, DT_PAD), const2),
                  _resident((1, DT_PAD), const2),
                  _resident((DT_PAD, D_SSD), const2),
                  _resident(state_block, state)],
        out_specs=[pl.BlockSpec((b, CHUNK, D_SSD), chunk),
                   pl.BlockSpec((b, CHUNK, XBC), chunk),
                   pl.BlockSpec(state_block, state)],
        out_shape=[jax.ShapeDtypeStruct((b, l, D_SSD), F32),
                   jax.ShapeDtypeStruct((b, l, XBC), BF16),
                   jax.ShapeDtypeStruct(state_block, F32)],
        scratch_shapes=[pltpu.VMEM(state_block, F32),
                        pltpu.VMEM((b, CHUNK + 2 * HALO, XBC), BF16)],
        compiler_params=_cparams(("arbitrary",)),
        name="ssd_bwd",
    )(p, p, p, pdt, lw["conv_w"], lw["conv_b"], _shift_matrix(), lw["dt_bias"], lw["a_log"], lw["expand_b"], h0)


def _ssd_fwd(xc, pdt, lw, h0, p=None, yb=None):
    b, l, _ = xc.shape
    nc = l // CHUNK
    chunk = lambda c: (0, c, 0)
    const2 = lambda c: (0, 0)
    state = lambda c: (0, 0, 0, 0)
    state_block = (b,) + _STATE_TAIL
    epilogue = p is not None
    in_specs = [pl.BlockSpec((b, CHUNK, XBC), chunk),
                pl.BlockSpec((b, CHUNK, DT_PAD), chunk),
                _resident((1, DT_PAD), const2),
                _resident((1, DT_PAD), const2),
                _resident((DT_PAD, D_SSD), const2),
                _resident(state_block, state)]
    args = [xc, pdt, lw["dt_bias"], lw["a_log"], lw["expand_f"], h0]
    if epilogue:
        z_blk = P_Z // D_SSD
        in_specs += [pl.BlockSpec((b, CHUNK, D_SSD), lambda c: (0, c, z_blk)),
                     pl.BlockSpec((b, CHUNK, D_SSD), chunk),
                     _resident((1, D_SSD), const2),
                     _resident((1, D_SSD), const2)]
        args += [p, yb, lw["d_skip"], lw["ssd_norm_w"]]
    return pl.pallas_call(
        functools.partial(_ssd_kernel, fwd=True, nc=nc, epilogue=epilogue),
        grid=(nc,),
        in_specs=in_specs,
        out_specs=[pl.BlockSpec((b, CHUNK, D_SSD), chunk),
                   pl.BlockSpec(state_block, state)],
        out_shape=[jax.ShapeDtypeStruct((b, l, D_SSD), BF16 if epilogue else F32),
                   jax.ShapeDtypeStruct(state_block, F32)],
        scratch_shapes=[pltpu.VMEM(state_block, F32)],
        compiler_params=_cparams(("arbitrary",)),
        name="ssd_fwd",
    )(*args)


def _ssd(p, pdt, lw, h0f, h0b):
    yb, xc, hb = _ssd_bwd(p, pdt, lw, h0b)
    y, hf = _ssd_fwd(xc, pdt, lw, h0f, p, yb)
    return y, hf, hb


def _expand_matrix(offset):
    e = np.zeros((DT_PAD, D_SSD), np.float32)
    for h in range(SSD_HEADS):
        e[offset + h, h * HEAD_DIM:(h + 1) * HEAD_DIM] = 1.0
    return jnp.asarray(e).astype(BF16)


def _layer_weights(l, conv_w, conv_b, a_log, dt_bias, d_skip, ssd_norm_w):
    pad_h = (0, DT_PAD - 2 * SSD_HEADS)
    return {
        "conv_w": jnp.pad(conv_w[l], ((0, 8 - CONV_W), (0, 0))),
        "conv_b": conv_b[l].reshape(1, XBC),
        "dt_bias": jnp.pad(dt_bias[l].reshape(-1), pad_h).reshape(1, DT_PAD),
        "a_log": jnp.pad(a_log[l].reshape(-1), pad_h).reshape(1, DT_PAD),
        "d_skip": jnp.repeat(d_skip[l], HEAD_DIM).reshape(1, D_SSD),
        "ssd_norm_w": ssd_norm_w[l].reshape(1, D_SSD),
        "expand_f": _expand_matrix(0),
        "expand_b": _expand_matrix(SSD_HEADS),
    }


def _mixer(x, nw, sc, sh, g_m, wts, lw, layer, grid_rows, h0f, h0b):
    b, l, dm = x.shape
    tok = (lambda a: a.reshape(1, b * l, a.shape[-1])) if sc.shape[0] == 1 else (lambda a: a)
    seq = lambda a: a.reshape(b, l, a.shape[-1])
    p, pdt = _inproj(tok(x), nw, sc, sh, wts["w_main"], wts["w_dt"], layer, P_TOT, INPROJ_TM, INPROJ_TN)
    y_n, hf, hb = _ssd(seq(p), seq(pdt), lw, h0f, h0b)
    d = _pool_diff(seq(p), grid_rows)
    f = _fourier(seq(p))
    x = seq(_merge_out(tok(d), tok(f), tok(y_n), p, tok(x), g_m, wts, layer, MERGE_TM))
    return x, hf, hb


def kernel(x, c, ctx, c_ctx, w_ada, b_ada, norm_mix_w, norm_ffn_w, w_in, conv_w, conv_b, a_log, dt_bias,
           d_skip, ssd_norm_w, w_ssd_out, w_pool, pool_scale, w_fourier, w_out, w_ffn_gate, w_ffn_up,
           w_ffn_down, final_norm_w):
    b, seq, d = x.shape
    rows = seq // GRID_W
    mod = _adaln(jnp.concatenate([c, c_ctx[None, :]], axis=0), w_ada, b_ada)
    h0 = jnp.zeros((b,) + _STATE_TAIL, F32)
    lc = ctx.shape[1]
    w_main = _winprep(w_in)
    w_dt = jnp.pad(w_in[:, :, XBC:SSD_IN], ((0, 0), (0, 0), (0, DT_PAD - 2 * SSD_HEADS))).astype(BF16)
    wts = {
        "w_main": w_main,
        "w_dt": w_dt,
        "w_pool": _to_bf16(w_pool.reshape(DEPTH, POOL_W, POOL_OUT)),
        "pool_scale": pool_scale.reshape(DEPTH, 1, D_MODEL),
        "w_fourier": _to_bf16(w_fourier),
        "w_ssd_out": _to_bf16(w_ssd_out),
        "w_out": _to_bf16(w_out),
    }
    wg, wu, wd = _to_bf16(w_ffn_gate), _to_bf16(w_ffn_up), _to_bf16(w_ffn_down)
    for l in range(DEPTH):
        last = l == DEPTH - 1
        lw = _layer_weights(l, conv_w, conv_b, a_log, dt_bias, d_skip, ssd_norm_w)
        sh_m, sc_m, g_m, sh_f, sc_f, g_f = [mod[l, :b, i * d:(i + 1) * d].reshape(b, 1, d) for i in range(6)]
        csh_m, csc_m, cg_m, csh_f, csc_f, cg_f = [mod[l, b, i * d:(i + 1) * d].reshape(1, 1, d) for i in range(6)]

        if last:
            pc, pdtc = _inproj(ctx.reshape(1, b * lc, d), norm_mix_w[l], csc_m, csh_m, w_main, w_dt, l,
                               XBC, INPROJ_TM, INPROJ_TN)
            pdtc = pdtc.reshape(b, lc, DT_PAD)
            _, xcc, hb = _ssd_bwd(pc.reshape(b, lc, XBC), pdtc, lw, h0)
            _, hf = _ssd_fwd(xcc, pdtc, lw, h0)
        else:
            ctx, hf, hb = _mixer(ctx, norm_mix_w[l], csc_m, csh_m, cg_m, wts, lw, l, None, h0, h0)
            ctx = _ffn(ctx.reshape(1, b * lc, d), norm_ffn_w[l], csc_f, csh_f, wg, wu, wd, l,
                       cg_f, final_norm_w, False, FFN_TM, FFN_TF).reshape(b, lc, d)

        x, _, _ = _mixer(x, norm_mix_w[l], sc_m, sh_m, g_m, wts, lw, l, rows, hf, hb)
        x = _ffn(x, norm_ffn_w[l], sc_f, sh_f, wg, wu, wd, l, g_f, final_norm_w, last, FFN_TM, FFN_TF)
    return x
```

```python
import functools
import math

import numpy as np
import jax
import jax.numpy as jnp
from jax import lax
from jax.experimental import pallas as pl
from jax.experimental.pallas import tpu as pltpu

F32 = jnp.float32
BF16 = jnp.bfloat16

D_MODEL = 2048
DEPTH = 2
GRID_W = 64
EPS = 1e-6

POOL_GROUPS = 4
POOL_WINDOWS = (2, 4, 8, 16)
POOL_W = D_MODEL // 2
POOL_GC = POOL_W // POOL_GROUPS
POOL_OUT = D_MODEL // POOL_GROUPS

FOUR_GROUPS = 4
FOUR_W = D_MODEL // 2
FOUR_GC = FOUR_W // FOUR_GROUPS

D_SSD = D_MODEL
HEAD_DIM = 64
SSD_HEADS = D_SSD // HEAD_DIM
SSD_GROUPS = 4
HEADS_PER_GROUP = SSD_HEADS // SSD_GROUPS
GROUP_W = HEADS_PER_GROUP * HEAD_DIM
D_STATE = 128
CONV_W = 5
CHUNK = 128

XBC = D_SSD + 2 * SSD_GROUPS * D_STATE
SSD_IN = XBC + 2 * SSD_HEADS
Z_OFF = SSD_IN
POOL_OFF = Z_OFF + D_SSD
FOUR_OFF = POOL_OFF + POOL_W
GATE_OFF = FOUR_OFF + FOUR_W
N_IN = GATE_OFF + 3 * D_MODEL

P_XBC = 0
P_POOL = XBC
P_Z = P_POOL + POOL_W
P_GATE = P_Z + D_SSD
P_FOUR = P_GATE + 3 * D_MODEL
P_TOT = P_FOUR + FOUR_W
DT_PAD = 128

V7X_VMEM_LIMIT = 56 * 1024 * 1024
HALO = 16

INPROJ_TM, INPROJ_TN = 1024, 1024
MERGE_TM = 256
FFN_TM, FFN_TF = 512, 512
ADALN_TN = 1024
FFN_CAST_ROWS = 64


def _cparams(sem):
    return pltpu.CompilerParams(dimension_semantics=sem, vmem_limit_bytes=V7X_VMEM_LIMIT)


def _resident(shape, index_map):
    return pl.BlockSpec(shape, index_map, pipeline_mode=pl.Buffered(1))


def _split2(v):
    hi = v.astype(BF16)
    lo = (v - hi.astype(F32)).astype(BF16)
    return hi, lo


def _dot(a, b):
    return jnp.dot(a, b, preferred_element_type=F32)


def _sigmoid(v):
    return 0.5 * jnp.tanh(0.5 * v) + 0.5


def _silu(v):
    return v * _sigmoid(v)


def _np_split2(m):
    m = jnp.asarray(np.asarray(m, np.float32))
    hi = m.astype(BF16)
    lo = (m - hi.astype(F32)).astype(BF16)
    return hi, lo


CAST_ROWS = 256
WIN_CHUNK = 1024


def _cast_kernel(w_ref, o_ref):
    o_ref[...] = w_ref[...].astype(BF16)


def _to_bf16(w):
    depth, k, n = w.shape
    return pl.pallas_call(
        _cast_kernel,
        grid=(depth, k // CAST_ROWS),
        in_specs=[pl.BlockSpec((1, CAST_ROWS, n), lambda l, i: (l, i, 0))],
        out_specs=pl.BlockSpec((1, CAST_ROWS, n), lambda l, i: (l, i, 0)),
        out_shape=jax.ShapeDtypeStruct((depth, k, n), BF16),
        compiler_params=_cparams(("parallel", "parallel")),
        name="cast_bf16",
    )(w)


_WIN_SEGMENTS = ((P_XBC, 0, XBC), (P_POOL, POOL_OFF, POOL_W), (P_Z, Z_OFF, D_SSD),
                 (P_GATE, GATE_OFF, 3 * D_MODEL), (P_FOUR, FOUR_OFF, FOUR_W))


def _win_src(j):
    src = jnp.int32(0)
    for dst0, src0, width in _WIN_SEGMENTS:
        first = dst0 // WIN_CHUNK
        inside = (j >= first) & (j < first + width // WIN_CHUNK)
        src = jnp.where(inside, src0 + (j - first) * WIN_CHUNK, src)
    return pl.multiple_of(src, 2 * SSD_HEADS)


def _winprep_kernel(wt_ref, o_ref):
    o_ref[0] = wt_ref[0].T.astype(BF16)


def _winprep(w_in):
    depth, d, n = w_in.shape
    wt = jnp.swapaxes(w_in, 1, 2)
    return pl.pallas_call(
        _winprep_kernel,
        grid=(depth, P_TOT // WIN_CHUNK),
        in_specs=[pl.BlockSpec((pl.Element(1), pl.Element(WIN_CHUNK), pl.Element(d)),
                               lambda l, j: (l, _win_src(j), 0))],
        out_specs=pl.BlockSpec((1, d, WIN_CHUNK), lambda l, j: (l, 0, j)),
        out_shape=jax.ShapeDtypeStruct((depth, d, P_TOT), BF16),
        compiler_params=_cparams(("parallel", "parallel")),
        name="winprep",
    )(wt)


ADA_ROWS = 8


def _adaln_kernel(ct_ref, w_ref, b_ref, o_ref, *, n_rows):
    s = _silu(ct_ref[...])
    w = w_ref[0]
    tn = w.shape[1]
    rows = [jnp.sum(w * s[:, r:r + 1], axis=0, keepdims=True) for r in range(n_rows)]
    rows.append(jnp.zeros((ADA_ROWS - n_rows, tn), F32))
    o_ref[0] = jnp.concatenate(rows, axis=0) + b_ref[0]


def _adaln(cc, w_ada, b_ada):
    depth, d, n = w_ada.shape
    tn = ADALN_TN
    n_rows = cc.shape[0]
    ct = jnp.pad(cc, ((0, ADA_ROWS - n_rows), (0, 0))).T
    return pl.pallas_call(
        functools.partial(_adaln_kernel, n_rows=n_rows),
        grid=(depth, n // tn),
        in_specs=[pl.BlockSpec((d, ADA_ROWS), lambda l, j: (0, 0)),
                  pl.BlockSpec((1, d, tn), lambda l, j: (l, 0, j)),
                  pl.BlockSpec((1, 1, tn), lambda l, j: (l, 0, j))],
        out_specs=pl.BlockSpec((1, ADA_ROWS, tn), lambda l, j: (l, 0, j)),
        out_shape=jax.ShapeDtypeStruct((depth, ADA_ROWS, n), F32),
        compiler_params=_cparams(("parallel", "parallel")),
        name="adaln",
    )(ct, w_ada, b_ada.reshape(depth, 1, n))


def _norm_mod(x, nw, sc, sh):
    ms = jnp.mean(x * x, axis=-1, keepdims=True)
    return (x * lax.rsqrt(ms + EPS)) * nw * (1.0 + sc) + sh


def _inproj_kernel(x_ref, nw_ref, sc_ref, sh_ref, w_ref, wdt_ref, *rest, n_side):
    side_in = rest[:n_side]
    o_ref, dt_ref = rest[n_side:n_side + 2]
    side_out = rest[n_side + 2:2 * n_side + 2]
    h_ref = rest[-1]

    @pl.when(pl.program_id(2) == 0)
    def _():
        h = _norm_mod(x_ref[0], nw_ref[...], sc_ref[0], sh_ref[0]).astype(BF16)
        h_ref[...] = h
        dt_ref[0] = _dot(h, wdt_ref[0])

    o_ref[0] = _dot(h_ref[...], w_ref[0]).astype(o_ref.dtype)

    for wi_ref, wo_ref in zip(side_in, side_out):
        wo_ref[...] = wi_ref[...].astype(BF16)


def _inproj(x, nw, sc, sh, w, wdt, layer, n_out, tm, tn, side=()):
    b, l, d = x.shape
    tm = min(tm, l)
    n_i, n_j = l // tm, n_out // tn
    side_specs, side_shapes = [], []
    for arr, rows in side:
        nblk = arr.shape[0] // rows
        assert arr.shape[0] % rows == 0 and nblk <= b * n_i * n_j
        imap = lambda bi, i, j, nblk=nblk: (jnp.minimum((bi * n_i + i) * n_j + j, nblk - 1), 0)
        side_specs.append(pl.BlockSpec((rows, arr.shape[1]), imap))
        side_shapes.append(jax.ShapeDtypeStruct(arr.shape, BF16))
    sem = ("arbitrary",) * 3 if side else ("parallel", "parallel", "arbitrary")
    return pl.pallas_call(
        functools.partial(_inproj_kernel, n_side=len(side)),
        grid=(b, n_i, n_j),
        in_specs=[pl.BlockSpec((1, tm, d), lambda bi, i, j: (bi, i, 0)),
                  pl.BlockSpec((1, d), lambda bi, i, j: (0, 0)),
                  pl.BlockSpec((1, 1, d), lambda bi, i, j: (bi, 0, 0)),
                  pl.BlockSpec((1, 1, d), lambda bi, i, j: (bi, 0, 0)),
                  pl.BlockSpec((1, d, tn), lambda bi, i, j: (layer, 0, j)),
                  _resident((1, d, DT_PAD), lambda bi, i, j: (layer, 0, 0))] + side_specs,
        out_specs=[pl.BlockSpec((1, tm, tn), lambda bi, i, j: (bi, i, j)),
                   pl.BlockSpec((1, tm, DT_PAD), lambda bi, i, j: (bi, i, 0))] + side_specs,
        out_shape=[jax.ShapeDtypeStruct((b, l, n_out), BF16),
                   jax.ShapeDtypeStruct((b, l, DT_PAD), F32)] + side_shapes,
        scratch_shapes=[pltpu.VMEM((tm, d), BF16)],
        compiler_params=_cparams(sem),
        name="inproj",
    )(x, nw.reshape(1, d), sc, sh, w, wdt, *[arr for arr, _ in side])


def _merge_kernel(d_ref, f_ref, yn_ref, gp_ref, gf_ref, gs_ref, x_ref, gm_ref,
                  wp_ref, ps_ref, wf_ref, ws_ref, wo_ref, o_ref):
    d = d_ref[0]
    y_pool = jnp.concatenate(
        [_dot(d[:, g * POOL_GC:(g + 1) * POOL_GC], wp_ref[0, g * POOL_GC:(g + 1) * POOL_GC, :])
         for g in range(POOL_GROUPS)], axis=1)
    m = _sigmoid(gp_ref[0].astype(F32)) * (y_pool * ps_ref[0])
    m = m + _sigmoid(gf_ref[0].astype(F32)) * _dot(f_ref[0], wf_ref[0])
    m = m + _sigmoid(gs_ref[0].astype(F32)) * _dot(yn_ref[0], ws_ref[0])
    o_ref[0] = x_ref[0] + gm_ref[0] * _dot(m.astype(BF16), wo_ref[0])


def _merge_out(d, f, yn, p, x, g_m, wts, layer, tm):
    b, l, dm = x.shape
    tm = min(tm, l)
    gblk = P_GATE // dm
    row = lambda bi, i: (bi, i, 0)
    lyr = lambda bi, i: (layer, 0, 0)
    return pl.pallas_call(
        _merge_kernel,
        grid=(b, l // tm),
        in_specs=[pl.BlockSpec((1, tm, POOL_W), row),
                  pl.BlockSpec((1, tm, FOUR_W), row),
                  pl.BlockSpec((1, tm, D_SSD), row),
                  pl.BlockSpec((1, tm, dm), lambda bi, i: (bi, i, gblk)),
                  pl.BlockSpec((1, tm, dm), lambda bi, i: (bi, i, gblk + 1)),
                  pl.BlockSpec((1, tm, dm), lambda bi, i: (bi, i, gblk + 2)),
                  pl.BlockSpec((1, tm, dm), row),
                  pl.BlockSpec((1, 1, dm), lambda bi, i: (bi, 0, 0)),
                  _resident((1, POOL_W, POOL_OUT), lyr),
                  _resident((1, 1, dm), lyr),
                  _resident((1, FOUR_W, dm), lyr),
                  _resident((1, D_SSD, dm), lyr),
                  _resident((1, dm, dm), lyr)],
        out_specs=pl.BlockSpec((1, tm, dm), row),
        out_shape=jax.ShapeDtypeStruct((b, l, dm), F32),
        compiler_params=_cparams(("parallel", "parallel")),
        name="merge_out",
    )(d, f, yn, p, p, p, x, g_m, wts["w_pool"], wts["pool_scale"], wts["w_fourier"], wts["w_ssd_out"],
      wts["w_out"])


def _ffn_kernel(x_ref, nw_ref, sc_ref, sh_ref, wg_ref, wu_ref, wd_ref, gate_ref, fnw_ref,
                o_ref, h_ref, acc_ref, *, final_norm):
    j = pl.program_id(2)

    @pl.when(j == 0)
    def _():
        h_ref[...] = _norm_mod(x_ref[0], nw_ref[...], sc_ref[0], sh_ref[0]).astype(BF16)
        acc_ref[...] = jnp.zeros_like(acc_ref)

    h = h_ref[...]
    a = _silu(_dot(h, wg_ref[0])) * _dot(h, wu_ref[0])
    acc_ref[...] += _dot(a.astype(BF16), wd_ref[0])

    @pl.when(j == pl.num_programs(2) - 1)
    def _():
        o = x_ref[0] + gate_ref[0] * acc_ref[...]
        if final_norm:
            ms = jnp.mean(o * o, axis=-1, keepdims=True)
            o = (o * lax.rsqrt(ms + EPS)) * fnw_ref[...]
        o_ref[0] = o


def _ffn(x, nw, sc, sh, wg, wu, wd, layer, gate, fnw, final_norm, tm, tf):
    b, l, d = x.shape
    ff = wg.shape[2]
    tm = min(tm, l)
    vec = lambda bi, i, j: (bi, 0, 0)
    return pl.pallas_call(
        functools.partial(_ffn_kernel, final_norm=final_norm),
        grid=(b, l // tm, ff // tf),
        in_specs=[pl.BlockSpec((1, tm, d), lambda bi, i, j: (bi, i, 0)),
                  pl.BlockSpec((1, d), lambda bi, i, j: (0, 0)),
                  pl.BlockSpec((1, 1, d), vec),
                  pl.BlockSpec((1, 1, d), vec),
                  pl.BlockSpec((1, d, tf), lambda bi, i, j: (layer, 0, j)),
                  pl.BlockSpec((1, d, tf), lambda bi, i, j: (layer, 0, j)),
                  pl.BlockSpec((1, tf, d), lambda bi, i, j: (layer, j, 0)),
                  pl.BlockSpec((1, 1, d), vec),
                  pl.BlockSpec((1, d), lambda bi, i, j: (0, 0))],
        out_specs=pl.BlockSpec((1, tm, d), lambda bi, i, j: (bi, i, 0)),
        out_shape=jax.ShapeDtypeStruct((b, l, d), F32),
        scratch_shapes=[pltpu.VMEM((tm, d), BF16), pltpu.VMEM((tm, d), F32)],
        compiler_params=_cparams(("parallel", "parallel", "arbitrary")),
        name="ffn",
    )(x, nw.reshape(1, d), sc, sh, wg, wu, wd, gate, fnw.reshape(1, d))


def _box_matrix(n, w):
    idx = np.arange(n)
    lo = np.clip(idx - w // 2, 0, n)
    hi = np.clip(idx + (w - w // 2), 0, n)
    m = ((idx[None, :] >= lo[:, None]) & (idx[None, :] < hi[:, None])).astype(np.float64)
    return m / (hi - lo)[:, None]


POOL_TB = 256
POOL_PAD = 16


def _pool_kernel(u_ref, mh_ref, ml_ref, o_ref, *scratch, grid_rows):
    g = pl.program_id(1)
    l = u_ref.shape[1]
    mh = mh_ref[0]
    ml = ml_ref[0]
    if grid_rows is None:
        v = u_ref[0]
        o_ref[0] = (_dot(mh, v) + _dot(ml, v) - v.astype(F32)).astype(o_ref.dtype)
        return

    cp_ref, = scratch
    pad = POOL_PAD * GRID_W
    cp_ref[0:pad, :] = jnp.zeros((pad, POOL_GC), F32)
    cp_ref[pad + l:pad + l + pad, :] = jnp.zeros((pad, POOL_GC), F32)
    for i in range(l // POOL_TB):
        v = u_ref[0, i * POOL_TB:(i + 1) * POOL_TB, :]
        cp_ref[pad + i * POOL_TB:pad + (i + 1) * POOL_TB, :] = _dot(mh, v) + _dot(ml, v)

    for gi, w in enumerate(POOL_WINDOWS):
        lo_off, hi_off = -(w // 2), w - w // 2

        @pl.when(g == gi)
        def _(lo_off=lo_off, hi_off=hi_off):
            def slab(row):
                return cp_ref[pl.ds(pl.multiple_of(pad + row * GRID_W, GRID_W), GRID_W), :]

            def body(r, s):
                s = s + slab(r + hi_off - 1) - slab(r + lo_off - 1)
                cnt = jnp.minimum(r + hi_off, grid_rows) - jnp.maximum(r + lo_off, 0)
                cntv = jnp.full((GRID_W, POOL_GC), cnt, jnp.int32).astype(F32)
                t0 = pl.multiple_of(r * GRID_W, GRID_W)
                v = u_ref[0, pl.ds(t0, GRID_W), :].astype(F32)
                o_ref[0, pl.ds(t0, GRID_W), :] = (s / cntv - v).astype(o_ref.dtype)
                return s

            s0 = jnp.zeros((GRID_W, POOL_GC), F32)
            for o in range(lo_off - 1, hi_off - 1):
                s0 = s0 + cp_ref[pad + o * GRID_W:pad + (o + 1) * GRID_W, :]
            lax.fori_loop(0, grid_rows, body, s0)


def _pool_diff(p, grid_rows):
    b, l, _ = p.shape
    mats = []
    for w in POOL_WINDOWS:
        if grid_rows is None:
            mats.append(_box_matrix(l, w))
        else:
            mats.append(np.kron(np.eye(POOL_TB // GRID_W), _box_matrix(GRID_W, w)))
    mh, ml = _np_split2(np.stack(mats))
    tb = mats[0].shape[0]
    scratch = []
    if grid_rows is not None:
        scratch = [pltpu.VMEM((l + 2 * POOL_PAD * GRID_W, POOL_GC), F32)]
    cblk = P_POOL // POOL_GC
    return pl.pallas_call(
        functools.partial(_pool_kernel, grid_rows=grid_rows),
        grid=(b, POOL_GROUPS),
        in_specs=[pl.BlockSpec((1, l, POOL_GC), lambda bi, g: (bi, 0, cblk + g)),
                  pl.BlockSpec((1, tb, tb), lambda bi, g: (g, 0, 0)),
                  pl.BlockSpec((1, tb, tb), lambda bi, g: (g, 0, 0))],
        out_specs=pl.BlockSpec((1, l, POOL_GC), lambda bi, g: (bi, 0, g)),
        out_shape=jax.ShapeDtypeStruct((b, l, POOL_W), BF16),
        scratch_shapes=scratch,
        compiler_params=_cparams(("parallel", "parallel")),
        name="pool_diff",
    )(p, mh, ml)


FLIP_ROWS = 512


def _flip_matrix(t):
    m = np.zeros((t, t), np.float32)
    m[np.arange(1, t), t - np.arange(1, t)] = 1.0
    return jnp.asarray(m).astype(BF16)


def _chdft_fold_kernel(u_ref, ua_ref, ub_ref, w_ref, flip_ref, o_ref, nyq_ref):
    i = pl.program_id(2)
    c = FOUR_GC
    tm = u_ref.shape[1]
    w = w_ref[...]
    urev = _dot(flip_ref[...], ua_ref[0])
    row = lax.broadcasted_iota(jnp.int32, (tm, c), 0)
    urev = jnp.where(row == 0, jnp.broadcast_to(ub_ref[0, 0:1, :].astype(F32), (tm, c)), urev).astype(BF16)
    z = _dot(u_ref[0], w)
    zrev = _dot(urev, w)
    first = (row + jnp.minimum(i, 1)) == 0
    o_ref[0, 0, :, :c] = (z[:, :c] + jnp.where(first, 0.0, zrev[:, :c])).astype(o_ref.dtype)
    o_ref[0, 0, :, c:] = (z[:, c:] - zrev[:, c:]).astype(o_ref.dtype)

    @pl.when(i == 0)
    def _():
        nyq_ref[0, 0] = zrev[0:8, :c]


def _chdft_fold(p):
    b, l, _ = p.shape
    half = l // 2
    tm = min(FLIP_ROWS, half)
    nb = l // tm
    k = np.arange(FOUR_GC)
    ang = -2.0 * np.pi * ((k[:, None] * k[None, :]) % FOUR_GC) / FOUR_GC
    w = jnp.asarray(np.concatenate([np.cos(ang), np.sin(ang)], axis=1).astype(np.float32)).astype(BF16)
    cblk = P_FOUR // FOUR_GC
    return pl.pallas_call(
        _chdft_fold_kernel,
        grid=(b, FOUR_GROUPS, half // tm),
        in_specs=[pl.BlockSpec((1, tm, FOUR_GC), lambda bi, g, i: (bi, i, cblk + g)),
                  pl.BlockSpec((1, tm, FOUR_GC), lambda bi, g, i: (bi, nb - 1 - i, cblk + g)),
                  pl.BlockSpec((1, tm, FOUR_GC), lambda bi, g, i: (bi, jnp.where(i == 0, nb // 2, nb - i), cblk + g)),
                  _resident((FOUR_GC, 2 * FOUR_GC), lambda bi, g, i: (0, 0)),
                  _resident((tm, tm), lambda bi, g, i: (0, 0))],
        out_specs=[pl.BlockSpec((1, 1, tm, 2 * FOUR_GC), lambda bi, g, i: (bi, g, i, 0)),
                   pl.BlockSpec((1, 1, 8, FOUR_GC), lambda bi, g, i: (bi, g, 0, 0))],
        out_shape=[jax.ShapeDtypeStruct((b, FOUR_GROUPS, half, 2 * FOUR_GC), BF16),
                   jax.ShapeDtypeStruct((b, FOUR_GROUPS, 8, FOUR_GC), F32)],
        compiler_params=_cparams(("parallel", "parallel", "arbitrary")),
        name="chdft",
    )(p, p, p, w, _flip_matrix(tm))


def _seq_dft_tables(l):
    half = l // 2
    r = int(round(math.sqrt(l)))
    assert r * r == l and r % 2 == 0
    col = np.arange(half)
    hi = 2.0 * np.pi * ((np.arange(r // 2)[:, None] * r * col[None, :]) % l) / l
    lo = 2.0 * np.pi * ((np.arange(r)[:, None] * col[None, :]) % l) / l
    ch, sh = [jnp.asarray(f(hi).astype(np.float32))[:, None, :] for f in (np.cos, np.sin)]
    cl, sl = [jnp.asarray(f(lo).astype(np.float32))[None, :, :] for f in (np.cos, np.sin)]
    cos = (ch * cl - sh * sl).reshape(half, half)
    sin = (sh * cl + ch * sl).reshape(half, half)
    return cos.astype(BF16), sin.astype(BF16)


def _seqdft_kernel(c_ref, s_ref, alt_ref, flip_ref, e_ref, nyq_ref, o_ref, *, scale):
    c = FOUR_GC
    half = e_ref.shape[2]
    ft = flip_ref.shape[0]
    nblk = half // ft
    e = e_ref[0, 0]
    p = _dot(c_ref[...], e[:, :c])
    q = _dot(s_ref[...], e[:, c:])
    zn = nyq_ref[0, 0, 0:1, :]
    k = lax.broadcasted_iota(jnp.int32, (half, c), 0)
    n = jnp.where((k & 1) == 0, zn, -zn)
    o_ref[0, 0:half, :] = ((p + q + n) * scale).astype(o_ref.dtype)
    mir = (p - q + n) * scale
    mir_b = mir.astype(BF16)
    y_half = (_dot(alt_ref[...], e[:, :c])[0:1] + zn) * scale
    row = lax.broadcasted_iota(jnp.int32, (ft, c), 0)
    for jb in range(nblk):
        src = (nblk - 1 - jb) * ft
        hi = _dot(flip_ref[...], mir_b[src:src + ft])
        first = y_half if jb == 0 else mir[src + ft:src + ft + 1]
        hi = jnp.where(row == 0, jnp.broadcast_to(first, (ft, c)), hi)
        o_ref[0, half + jb * ft:half + (jb + 1) * ft, :] = hi.astype(o_ref.dtype)


def _fourier(p):
    b, l, _ = p.shape
    half = l // 2
    ft = min(FLIP_ROWS, half)
    e, nyq = _chdft_fold(p)
    cos, sin = _seq_dft_tables(l)
    alt = jnp.asarray(np.broadcast_to(1.0 - 2.0 * (np.arange(half) % 2), (16, half)).astype(np.float32)).astype(BF16)
    scale = 1.0 / math.sqrt(l * FOUR_GC)
    const2 = lambda bi, g: (0, 0)
    return pl.pallas_call(
        functools.partial(_seqdft_kernel, scale=scale),
        grid=(b, FOUR_GROUPS),
        in_specs=[_resident((half, half), const2),
                  _resident((half, half), const2),
                  _resident((16, half), const2),
                  _resident((ft, ft), const2),
                  pl.BlockSpec((1, 1, half, 2 * FOUR_GC), lambda bi, g: (bi, g, 0, 0)),
                  pl.BlockSpec((1, 1, 8, FOUR_GC), lambda bi, g: (bi, g, 0, 0))],
        out_specs=pl.BlockSpec((1, l, FOUR_GC), lambda bi, g: (bi, 0, g)),
        out_shape=jax.ShapeDtypeStruct((b, l, FOUR_W), BF16),
        compiler_params=_cparams(("parallel", "parallel")),
        name="seqdft",
    )(cos, sin, alt, _flip_matrix(ft), e, nyq)


CONV_CB = 512
_SHIFT_TAPS = tuple(k for k in range(CONV_W) if k != CONV_W // 2)


def _shift_matrix():
    m = np.zeros((len(_SHIFT_TAPS) * CHUNK, CHUNK + 2 * HALO), np.float32)
    t = np.arange(CHUNK)
    for q, k in enumerate(_SHIFT_TAPS):
        m[q * CHUNK + t, HALO - CONV_W // 2 + k + t] = 1.0
    return jnp.asarray(m).astype(BF16)


def _ssd_kernel(*refs, fwd, nc, epilogue):
    if fwd:
        xc_ref, dt_ref, dtb_ref, alog_ref, e_ref, h0_ref = refs[:6]
        rest = refs[6:]
        if epilogue:
            z_ref, yb_ref, dsk_ref, nw_ref, y_ref, hout_ref, st_ref = rest
        else:
            y_ref, hout_ref, st_ref = rest
    else:
        (xm_ref, xp_ref, xn_ref, dt_ref, cw_ref, cb_ref, shift_ref, dtb_ref, alog_ref, e_ref, h0_ref,
         y_ref, xc_ref, hout_ref, st_ref, ext_ref) = refs
    c = pl.program_id(0)
    cc = c if fwd else nc - 1 - c
    T = CHUNK
    nb = dt_ref.shape[0]

    @pl.when(c == 0)
    def _():
        st_ref[...] = h0_ref[...]

    if not fwd:
        @pl.when(cc > 0)
        def _():
            ext_ref[:, 0:HALO, :] = xp_ref[...]

        @pl.when(cc == 0)
        def _():
            ext_ref[:, 0:HALO, :] = jnp.zeros((nb, HALO, XBC), ext_ref.dtype)

        ext_ref[:, HALO:HALO + T, :] = xm_ref[...]

        @pl.when(cc < nc - 1)
        def _():
            ext_ref[:, HALO + T:HALO + T + HALO, :] = xn_ref[...]

        @pl.when(cc == nc - 1)
        def _():
            ext_ref[:, HALO + T:HALO + T + HALO, :] = jnp.zeros((nb, HALO, XBC), ext_ref.dtype)

        shift = shift_ref[...]
        for j in range(XBC // CONV_CB):
            cs = slice(j * CONV_CB, (j + 1) * CONV_CB)
            for bi in range(nb):
                sh = _dot(shift, ext_ref[bi, :, cs])
                acc = cb_ref[:, cs] + xm_ref[bi, :, cs].astype(F32) * cw_ref[CONV_W // 2:CONV_W // 2 + 1, cs]
                for q, k in enumerate(_SHIFT_TAPS):
                    acc = acc + sh[q * T:(q + 1) * T] * cw_ref[k:k + 1, cs]
                xc_ref[bi, :, cs] = _silu(acc).astype(xc_ref.dtype)

    row = lax.broadcasted_iota(jnp.int32, (T, T), 0)
    col = lax.broadcasted_iota(jnp.int32, (T, T), 1)
    causal = (col <= row) if fwd else (col >= row)
    tri = jnp.where(causal, 1.0, 0.0).astype(BF16)
    lane = lax.broadcasted_iota(jnp.int32, (T, 2 * HEAD_DIM), 1)
    hbase = 0 if fwd else SSD_HEADS
    e = e_ref[...]
    a = -jnp.exp(alog_ref[...])

    def head_factors(bi):
        xdt = dt_ref[bi] + dtb_ref[...]
        dt = jnp.maximum(xdt, 0.0) + jnp.log1p(jnp.exp(-jnp.abs(xdt)))
        da = dt * a
        d1 = da.astype(BF16)
        r1 = da - d1.astype(F32)
        d2 = r1.astype(BF16)
        d3 = (r1 - d2.astype(F32)).astype(BF16)
        acs = _dot(tri, d1) + _dot(tri, d2) + _dot(tri, d3)
        tot = acs[T - 1:T, :] if fwd else acs[0:1, :]
        eh, el = _split2(jnp.broadcast_to(jnp.exp(tot), (16, T)))
        ex = _dot(jnp.concatenate([jnp.exp(acs).astype(BF16), (dt * jnp.exp(tot - acs)).astype(BF16), eh, el],
                                  axis=0), e)
        return dict(acs=acs, arow_t=(acs - jnp.log(dt)).T, eacs_x=ex[0:T], w2=ex[T:2 * T],
                    etot_x=ex[2 * T:2 * T + 1] + ex[2 * T + 16:2 * T + 17])

    hf = [head_factors(bi) for bi in range(nb)]

    for g in range(SSD_GROUPS):
        gs = slice(g * GROUP_W, (g + 1) * GROUP_W)
        bs = slice(D_SSD + g * D_STATE, D_SSD + (g + 1) * D_STATE)
        cs_ = slice(D_SSD + (SSD_GROUPS + g) * D_STATE, D_SSD + (SSD_GROUPS + g + 1) * D_STATE)
        for bi in range(nb):
            f = hf[bi]
            bb = xc_ref[bi, :, bs]
            cbf = xc_ref[bi, :, cs_]
            cb = lax.dot_general(cbf, bb, (((1,), (1,)), ((), ())), preferred_element_type=F32)
            st = st_ref[bi, g]
            yoff = _dot(cbf, st.astype(BF16)) * f["eacs_x"][:, gs]
            xb = xc_ref[bi, :, gs]
            ys = []
            for jp in range(HEADS_PER_GROUP // 2):
                ms = []
                for jj in range(2):
                    hc = hbase + g * HEADS_PER_GROUP + 2 * jp + jj
                    diff = f["acs"][:, hc:hc + 1] - f["arow_t"][hc:hc + 1, :]
                    ms.append((cb * jnp.exp(jnp.where(causal, diff, -1e30))).astype(BF16))
                r = _dot(jnp.concatenate(ms, axis=0), xb[:, jp * 2 * HEAD_DIM:(jp + 1) * 2 * HEAD_DIM])
                ys.append(jnp.where(lane < HEAD_DIM, r[:T], r[T:]))
            yg = jnp.concatenate(ys, axis=1) + yoff

            xg = xb.astype(F32)
            xds = (xg * f["w2"][:, gs]).astype(BF16)
            st_ref[bi, g] = st * f["etot_x"][:, gs] + _dot(bb.astype(F32).T.astype(BF16), xds)

            if epilogue:
                yt = yg + yb_ref[bi, :, gs] + dsk_ref[:, gs] * xg
                v = yt * _silu(z_ref[bi, :, gs].astype(F32))
                ms_ = jnp.mean(v * v, axis=-1, keepdims=True)
                y_ref[bi, :, gs] = ((v * lax.rsqrt(ms_ + EPS)) * nw_ref[:, gs]).astype(y_ref.dtype)
            else:
                y_ref[bi, :, gs] = yg

    @pl.when(c == nc - 1)
    def _():
        hout_ref[...] = st_ref[...]


_STATE_TAIL = (SSD_GROUPS, D_STATE, GROUP_W)


def _ssd_bwd(p, pdt, lw, h0):
    b, l, _ = p.shape
    nc = l // CHUNK
    nh = l // HALO
    per = CHUNK // HALO
    cidx = lambda c: nc - 1 - c
    chunk = lambda c: (0, cidx(c), 0)
    const2 = lambda c: (0, 0)
    state = lambda c: (0, 0, 0, 0)
    state_block = (b,) + _STATE_TAIL
    return pl.pallas_call(
        functools.partial(_ssd_kernel, fwd=False, nc=nc, epilogue=False),
        grid=(nc,),
        in_specs=[pl.BlockSpec((b, CHUNK, XBC), chunk),
                  pl.BlockSpec((b, HALO, XBC), lambda c: (0, jnp.maximum(cidx(c) * per - 1, 0), 0)),
                  pl.BlockSpec((b, HALO, XBC), lambda c: (0, jnp.minimum((cidx(c) + 1) * per, nh - 1), 0)),
                  pl.BlockSpec((b, CHUNK, DT_PAD), chunk),
                  _resident((8, XBC), const2),
                  _resident((1, XBC), const2),
                  _resident((len(_SHIFT_TAPS) * CHUNK, CHUNK + 2 * HALO), const2),
                  _resident((1, DT_PAD), const2),
                  _resident((1, DT_PAD), const2),
                  _resident((DT_PAD, D_SSD), const2),
                  _resident(state_block, state)],
        out_specs=[pl.BlockSpec((b, CHUNK, D_SSD), chunk),
                   pl.BlockSpec((b, CHUNK, XBC), chunk),
                   pl.BlockSpec(state_block, state)],
        out_shape=[jax.ShapeDtypeStruct((b, l, D_SSD), F32),
                   jax.ShapeDtypeStruct((b, l, XBC), BF16),
                   jax.ShapeDtypeStruct(state_block, F32)],
        scratch_shapes=[pltpu.VMEM(state_block, F32),
                        pltpu.VMEM((b, CHUNK + 2 * HALO, XBC), BF16)],
        compiler_params=_cparams(("arbitrary",)),
        name="ssd_bwd",
    )(p, p, p, pdt, lw["conv_w"], lw["conv_b"], _shift_matrix(), lw["dt_bias"], lw["a_log"], lw["expand_b"], h0)


def _ssd_fwd(xc, pdt, lw, h0, p=None, yb=None):
    b, l, _ = xc.shape
    nc = l // CHUNK
    chunk = lambda c: (0, c, 0)
    const2 = lambda c: (0, 0)
    state = lambda c: (0, 0, 0, 0)
    state_block = (b,) + _STATE_TAIL
    epilogue = p is not None
    in_specs = [pl.BlockSpec((b, CHUNK, XBC), chunk),
                pl.BlockSpec((b, CHUNK, DT_PAD), chunk),
                _resident((1, DT_PAD), const2),
                _resident((1, DT_PAD), const2),
                _resident((DT_PAD, D_SSD), const2),
                _resident(state_block, state)]
    args = [xc, pdt, lw["dt_bias"], lw["a_log"], lw["expand_f"], h0]
    if epilogue:
        z_blk = P_Z // D_SSD
        in_specs += [pl.BlockSpec((b, CHUNK, D_SSD), lambda c: (0, c, z_blk)),
                     pl.BlockSpec((b, CHUNK, D_SSD), chunk),
                     _resident((1, D_SSD), const2),
                     _resident((1, D_SSD), const2)]
        args += [p, yb, lw["d_skip"], lw["ssd_norm_w"]]
    return pl.pallas_call(
        functools.partial(_ssd_kernel, fwd=True, nc=nc, epilogue=epilogue),
        grid=(nc,),
        in_specs=in_specs,
        out_specs=[pl.BlockSpec((b, CHUNK, D_SSD), chunk),
                   pl.BlockSpec(state_block, state)],
        out_shape=[jax.ShapeDtypeStruct((b, l, D_SSD), BF16 if epilogue else F32),
                   jax.ShapeDtypeStruct(state_block, F32)],
        scratch_shapes=[pltpu.VMEM(state_block, F32)],
        compiler_params=_cparams(("arbitrary",)),
        name="ssd_fwd",
    )(*args)


def _ssd(p, pdt, lw, h0f, h0b):
    yb, xc, hb = _ssd_bwd(p, pdt, lw, h0b)
    y, hf = _ssd_fwd(xc, pdt, lw, h0f, p, yb)
    return y, hf, hb


def _expand_matrix(offset):
    e = np.zeros((DT_PAD, D_SSD), np.float32)
    for h in range(SSD_HEADS):
        e[offset + h, h * HEAD_DIM:(h + 1) * HEAD_DIM] = 1.0
    return jnp.asarray(e).astype(BF16)


def _layer_weights(l, conv_w, conv_b, a_log, dt_bias, d_skip, ssd_norm_w):
    pad_h = (0, DT_PAD - 2 * SSD_HEADS)
    return {
        "conv_w": jnp.pad(conv_w[l], ((0, 8 - CONV_W), (0, 0))),
        "conv_b": conv_b[l].reshape(1, XBC),
        "dt_bias": jnp.pad(dt_bias[l].reshape(-1), pad_h).reshape(1, DT_PAD),
        "a_log": jnp.pad(a_log[l].reshape(-1), pad_h).reshape(1, DT_PAD),
        "d_skip": jnp.repeat(d_skip[l], HEAD_DIM).reshape(1, D_SSD),
        "ssd_norm_w": ssd_norm_w[l].reshape(1, D_SSD),
        "expand_f": _expand_matrix(0),
        "expand_b": _expand_matrix(SSD_HEADS),
    }


def _mixer(x, nw, sc, sh, g_m, wts, lw, layer, grid_rows, h0f, h0b, proj=None):
    b, l, dm = x.shape
    tok = (lambda a: a.reshape(1, b * l, a.shape[-1])) if sc.shape[0] == 1 else (lambda a: a)
    seq = lambda a: a.reshape(b, l, a.shape[-1])
    p, pdt = proj or _inproj(tok(x), nw, sc, sh, wts["w_main"], wts["w_dt"], layer, P_TOT, INPROJ_TM, INPROJ_TN)
    y_n, hf, hb = _ssd(seq(p), seq(pdt), lw, h0f, h0b)
    d = _pool_diff(seq(p), grid_rows)
    f = _fourier(seq(p))
    x = seq(_merge_out(tok(d), tok(f), tok(y_n), p, tok(x), g_m, wts, layer, MERGE_TM))
    return x, hf, hb


def kernel(x, c, ctx, c_ctx, w_ada, b_ada, norm_mix_w, norm_ffn_w, w_in, conv_w, conv_b, a_log, dt_bias,
           d_skip, ssd_norm_w, w_ssd_out, w_pool, pool_scale, w_fourier, w_out, w_ffn_gate, w_ffn_up,
           w_ffn_down, final_norm_w):
    b, seq, d = x.shape
    rows = seq // GRID_W
    mod = _adaln(jnp.concatenate([c, c_ctx[None, :]], axis=0), w_ada, b_ada)
    h0 = jnp.zeros((b,) + _STATE_TAIL, F32)
    lc = ctx.shape[1]
    w_main = _winprep(w_in)
    w_dt = jnp.pad(w_in[:, :, XBC:SSD_IN], ((0, 0), (0, 0), (0, DT_PAD - 2 * SSD_HEADS))).astype(BF16)
    wts = {
        "w_main": w_main,
        "w_dt": w_dt,
        "w_pool": _to_bf16(w_pool.reshape(DEPTH, POOL_W, POOL_OUT)),
        "pool_scale": pool_scale.reshape(DEPTH, 1, D_MODEL),
        "w_fourier": _to_bf16(w_fourier),
        "w_ssd_out": _to_bf16(w_ssd_out),
        "w_out": _to_bf16(w_out),
    }
    ff = w_ffn_gate.shape[2]
    wg = wu = wd = None
    for l in range(DEPTH):
        last = l == DEPTH - 1
        lw = _layer_weights(l, conv_w, conv_b, a_log, dt_bias, d_skip, ssd_norm_w)
        sh_m, sc_m, g_m, sh_f, sc_f, g_f = [mod[l, :b, i * d:(i + 1) * d].reshape(b, 1, d) for i in range(6)]
        csh_m, csc_m, cg_m, csh_f, csc_f, cg_f = [mod[l, b, i * d:(i + 1) * d].reshape(1, 1, d) for i in range(6)]

        proj = None
        if l == 0:
            side = [(w_ffn_gate.reshape(DEPTH * d, ff), FFN_CAST_ROWS), (w_ffn_up.reshape(DEPTH * d, ff), FFN_CAST_ROWS),
                    (w_ffn_down.reshape(DEPTH * ff, d), 2 * FFN_CAST_ROWS)]
            p0, pdt0, wg, wu, wd = _inproj(x, norm_mix_w[l], sc_m, sh_m, w_main, w_dt, l, P_TOT, INPROJ_TM,
                                           INPROJ_TN, side)
            proj = (p0, pdt0)
            wg, wu, wd = wg.reshape(DEPTH, d, ff), wu.reshape(DEPTH, d, ff), wd.reshape(DEPTH, ff, d)

        if last:
            pc, pdtc = _inproj(ctx.reshape(1, b * lc, d), norm_mix_w[l], csc_m, csh_m, w_main, w_dt, l,
                               XBC, INPROJ_TM, INPROJ_TN)
            pdtc = pdtc.reshape(b, lc, DT_PAD)
            _, xcc, hb = _ssd_bwd(pc.reshape(b, lc, XBC), pdtc, lw, h0)
            _, hf = _ssd_fwd(xcc, pdtc, lw, h0)
        else:
            ctx, hf, hb = _mixer(ctx, norm_mix_w[l], csc_m, csh_m, cg_m, wts, lw, l, None, h0, h0)
            ctx = _ffn(ctx.reshape(1, b * lc, d), norm_ffn_w[l], csc_f, csh_f, wg, wu, wd, l,
                       cg_f, final_norm_w, False, FFN_TM, FFN_TF).reshape(b, lc, d)

        x, _, _ = _mixer(x, norm_mix_w[l], sc_m, sh_m, g_m, wts, lw, l, rows, hf, hb, proj)
        x = _ffn(x, norm_ffn_w[l], sc_f, sh_f, wg, wu, wd, l, g_f, final_norm_w, last, FFN_TM, FFN_TF)
    return x
```

```python
import functools
import math

import numpy as np
import jax
import jax.numpy as jnp
from jax import lax
from jax.experimental import pallas as pl
from jax.experimental.pallas import tpu as pltpu

F32 = jnp.float32
BF16 = jnp.bfloat16

D_MODEL = 2048
DEPTH = 2
GRID_W = 64
EPS = 1e-6

POOL_GROUPS = 4
POOL_WINDOWS = (2, 4, 8, 16)
POOL_W = D_MODEL // 2
POOL_GC = POOL_W // POOL_GROUPS
POOL_OUT = D_MODEL // POOL_GROUPS

FOUR_GROUPS = 4
FOUR_W = D_MODEL // 2
FOUR_GC = FOUR_W // FOUR_GROUPS

D_SSD = D_MODEL
HEAD_DIM = 64
SSD_HEADS = D_SSD // HEAD_DIM
SSD_GROUPS = 4
HEADS_PER_GROUP = SSD_HEADS // SSD_GROUPS
GROUP_W = HEADS_PER_GROUP * HEAD_DIM
D_STATE = 128
CONV_W = 5
CHUNK = 128

XBC = D_SSD + 2 * SSD_GROUPS * D_STATE
SSD_IN = XBC + 2 * SSD_HEADS
Z_OFF = SSD_IN
POOL_OFF = Z_OFF + D_SSD
FOUR_OFF = POOL_OFF + POOL_W
GATE_OFF = FOUR_OFF + FOUR_W
N_IN = GATE_OFF + 3 * D_MODEL

P_XBC = 0
P_POOL = XBC
P_Z = P_POOL + POOL_W
P_GATE = P_Z + D_SSD
P_FOUR = P_GATE + 3 * D_MODEL
P_TOT = P_FOUR + FOUR_W
DT_PAD = 128

V7X_VMEM_LIMIT = 56 * 1024 * 1024
HALO = 16

INPROJ_TM, INPROJ_TN = 1024, 1024
MERGE_TM = 256
FFN_TM, FFN_TF = 512, 512
ADALN_TN = 1024
FFN_CAST_ROWS = 64


def _cparams(sem):
    return pltpu.CompilerParams(dimension_semantics=sem, vmem_limit_bytes=V7X_VMEM_LIMIT)


def _resident(shape, index_map):
    return pl.BlockSpec(shape, index_map, pipeline_mode=pl.Buffered(1))


def _split2(v):
    hi = v.astype(BF16)
    lo = (v - hi.astype(F32)).astype(BF16)
    return hi, lo


def _dot(a, b):
    return jnp.dot(a, b, preferred_element_type=F32)


def _sigmoid(v):
    return 0.5 * jnp.tanh(0.5 * v) + 0.5


def _silu(v):
    return v * _sigmoid(v)


def _np_split2(m):
    m = jnp.asarray(np.asarray(m, np.float32))
    hi = m.astype(BF16)
    lo = (m - hi.astype(F32)).astype(BF16)
    return hi, lo


CAST_ROWS = 256
WIN_CHUNK = 1024


def _cast_kernel(w_ref, o_ref):
    o_ref[...] = w_ref[...].astype(BF16)


def _to_bf16(w):
    depth, k, n = w.shape
    return pl.pallas_call(
        _cast_kernel,
        grid=(depth, k // CAST_ROWS),
        in_specs=[pl.BlockSpec((1, CAST_ROWS, n), lambda l, i: (l, i, 0))],
        out_specs=pl.BlockSpec((1, CAST_ROWS, n), lambda l, i: (l, i, 0)),
        out_shape=jax.ShapeDtypeStruct((depth, k, n), BF16),
        compiler_params=_cparams(("parallel", "parallel")),
        name="cast_bf16",
    )(w)


_WIN_SEGMENTS = ((P_XBC, 0, XBC), (P_POOL, POOL_OFF, POOL_W), (P_Z, Z_OFF, D_SSD),
                 (P_GATE, GATE_OFF, 3 * D_MODEL), (P_FOUR, FOUR_OFF, FOUR_W))


def _win_src(j):
    src = jnp.int32(0)
    for dst0, src0, width in _WIN_SEGMENTS:
        first = dst0 // WIN_CHUNK
        inside = (j >= first) & (j < first + width // WIN_CHUNK)
        src = jnp.where(inside, src0 + (j - first) * WIN_CHUNK, src)
    return pl.multiple_of(src, 2 * SSD_HEADS)


def _winprep_kernel(wt_ref, o_ref):
    o_ref[0] = wt_ref[0].T.astype(BF16)


def _winprep(w_in):
    depth, d, n = w_in.shape
    wt = jnp.swapaxes(w_in, 1, 2)
    return pl.pallas_call(
        _winprep_kernel,
        grid=(depth, P_TOT // WIN_CHUNK),
        in_specs=[pl.BlockSpec((pl.Element(1), pl.Element(WIN_CHUNK), pl.Element(d)),
                               lambda l, j: (l, _win_src(j), 0))],
        out_specs=pl.BlockSpec((1, d, WIN_CHUNK), lambda l, j: (l, 0, j)),
        out_shape=jax.ShapeDtypeStruct((depth, d, P_TOT), BF16),
        compiler_params=_cparams(("parallel", "parallel")),
        name="winprep",
    )(wt)


ADA_ROWS = 8


def _adaln_kernel(ct_ref, w_ref, b_ref, o_ref, *, n_rows):
    s = _silu(ct_ref[...])
    w = w_ref[0]
    tn = w.shape[1]
    rows = [jnp.sum(w * s[:, r:r + 1], axis=0, keepdims=True) for r in range(n_rows)]
    rows.append(jnp.zeros((ADA_ROWS - n_rows, tn), F32))
    o_ref[0] = jnp.concatenate(rows, axis=0) + b_ref[0]


def _adaln(cc, w_ada, b_ada):
    depth, d, n = w_ada.shape
    tn = ADALN_TN
    n_rows = cc.shape[0]
    ct = jnp.pad(cc, ((0, ADA_ROWS - n_rows), (0, 0))).T
    return pl.pallas_call(
        functools.partial(_adaln_kernel, n_rows=n_rows),
        grid=(depth, n // tn),
        in_specs=[pl.BlockSpec((d, ADA_ROWS), lambda l, j: (0, 0)),
                  pl.BlockSpec((1, d, tn), lambda l, j: (l, 0, j)),
                  pl.BlockSpec((1, 1, tn), lambda l, j: (l, 0, j))],
        out_specs=pl.BlockSpec((1, ADA_ROWS, tn), lambda l, j: (l, 0, j)),
        out_shape=jax.ShapeDtypeStruct((depth, ADA_ROWS, n), F32),
        compiler_params=_cparams(("parallel", "parallel")),
        name="adaln",
    )(ct, w_ada, b_ada.reshape(depth, 1, n))


def _norm_mod(x, nw, sc, sh):
    ms = jnp.mean(x * x, axis=-1, keepdims=True)
    return (x * lax.rsqrt(ms + EPS)) * nw * (1.0 + sc) + sh


def _inproj_kernel(x_ref, nw_ref, sc_ref, sh_ref, w_ref, wdt_ref, *rest, n_side):
    side_in = rest[:n_side]
    o_ref, dt_ref = rest[n_side:n_side + 2]
    side_out = rest[n_side + 2:2 * n_side + 2]
    h_ref = rest[-1]

    @pl.when(pl.program_id(2) == 0)
    def _():
        h = _norm_mod(x_ref[0], nw_ref[...], sc_ref[0], sh_ref[0]).astype(BF16)
        h_ref[...] = h
        dt_ref[0] = _dot(h, wdt_ref[0])

    o_ref[0] = _dot(h_ref[...], w_ref[0]).astype(o_ref.dtype)

    for wi_ref, wo_ref in zip(side_in, side_out):
        wo_ref[...] = wi_ref[...].astype(BF16)


def _inproj(x, nw, sc, sh, w, wdt, layer, n_out, tm, tn, side=()):
    b, l, d = x.shape
    tm = min(tm, l)
    n_i, n_j = l // tm, n_out // tn
    side_specs, side_shapes = [], []
    for arr, rows in side:
        nblk = arr.shape[0] // rows
        assert arr.shape[0] % rows == 0 and nblk <= b * n_i * n_j
        imap = lambda bi, i, j, nblk=nblk: (jnp.minimum((bi * n_i + i) * n_j + j, nblk - 1), 0)
        side_specs.append(pl.BlockSpec((rows, arr.shape[1]), imap))
        side_shapes.append(jax.ShapeDtypeStruct(arr.shape, BF16))
    sem = ("arbitrary",) * 3 if side else ("parallel", "parallel", "arbitrary")
    return pl.pallas_call(
        functools.partial(_inproj_kernel, n_side=len(side)),
        grid=(b, n_i, n_j),
        in_specs=[pl.BlockSpec((1, tm, d), lambda bi, i, j: (bi, i, 0)),
                  pl.BlockSpec((1, d), lambda bi, i, j: (0, 0)),
                  pl.BlockSpec((1, 1, d), lambda bi, i, j: (bi, 0, 0)),
                  pl.BlockSpec((1, 1, d), lambda bi, i, j: (bi, 0, 0)),
                  pl.BlockSpec((1, d, tn), lambda bi, i, j: (layer, 0, j)),
                  _resident((1, d, DT_PAD), lambda bi, i, j: (layer, 0, 0))] + side_specs,
        out_specs=[pl.BlockSpec((1, tm, tn), lambda bi, i, j: (bi, i, j)),
                   pl.BlockSpec((1, tm, DT_PAD), lambda bi, i, j: (bi, i, 0))] + side_specs,
        out_shape=[jax.ShapeDtypeStruct((b, l, n_out), BF16),
                   jax.ShapeDtypeStruct((b, l, DT_PAD), F32)] + side_shapes,
        scratch_shapes=[pltpu.VMEM((tm, d), BF16)],
        compiler_params=_cparams(sem),
        name="inproj",
    )(x, nw.reshape(1, d), sc, sh, w, wdt, *[arr for arr, _ in side])


def _merge_kernel(d_ref, f_ref, yn_ref, gp_ref, gf_ref, gs_ref, x_ref, gm_ref,
                  wp_ref, ps_ref, wf_ref, ws_ref, wo_ref, o_ref):
    d = d_ref[0]
    y_pool = jnp.concatenate(
        [_dot(d[:, g * POOL_GC:(g + 1) * POOL_GC], wp_ref[0, g * POOL_GC:(g + 1) * POOL_GC, :])
         for g in range(POOL_GROUPS)], axis=1)
    m = _sigmoid(gp_ref[0].astype(F32)) * (y_pool * ps_ref[0])
    m = m + _sigmoid(gf_ref[0].astype(F32)) * _dot(f_ref[0], wf_ref[0])
    m = m + _sigmoid(gs_ref[0].astype(F32)) * _dot(yn_ref[0], ws_ref[0])
    o_ref[0] = x_ref[0] + gm_ref[0] * _dot(m.astype(BF16), wo_ref[0])


def _merge_out(d, f, yn, p, x, g_m, wts, layer, tm):
    b, l, dm = x.shape
    tm = min(tm, l)
    gblk = P_GATE // dm
    row = lambda bi, i: (bi, i, 0)
    lyr = lambda bi, i: (layer, 0, 0)
    return pl.pallas_call(
        _merge_kernel,
        grid=(b, l // tm),
        in_specs=[pl.BlockSpec((1, tm, POOL_W), row),
                  pl.BlockSpec((1, tm, FOUR_W), row),
                  pl.BlockSpec((1, tm, D_SSD), row),
                  pl.BlockSpec((1, tm, dm), lambda bi, i: (bi, i, gblk)),
                  pl.BlockSpec((1, tm, dm), lambda bi, i: (bi, i, gblk + 1)),
                  pl.BlockSpec((1, tm, dm), lambda bi, i: (bi, i, gblk + 2)),
                  pl.BlockSpec((1, tm, dm), row),
                  pl.BlockSpec((1, 1, dm), lambda bi, i: (bi, 0, 0)),
                  _resident((1, POOL_W, POOL_OUT), lyr),
                  _resident((1, 1, dm), lyr),
                  _resident((1, FOUR_W, dm), lyr),
                  _resident((1, D_SSD, dm), lyr),
                  _resident((1, dm, dm), lyr)],
        out_specs=pl.BlockSpec((1, tm, dm), row),
        out_shape=jax.ShapeDtypeStruct((b, l, dm), F32),
        compiler_params=_cparams(("parallel", "parallel")),
        name="merge_out",
    )(d, f, yn, p, p, p, x, g_m, wts["w_pool"], wts["pool_scale"], wts["w_fourier"], wts["w_ssd_out"],
      wts["w_out"])


def _ffn_kernel(x_ref, nw_ref, sc_ref, sh_ref, wg_ref, wu_ref, wd_ref, gate_ref, fnw_ref,
                o_ref, h_ref, acc_ref, *, final_norm):
    j = pl.program_id(2)

    @pl.when(j == 0)
    def _():
        h_ref[...] = _norm_mod(x_ref[0], nw_ref[...], sc_ref[0], sh_ref[0]).astype(BF16)
        acc_ref[...] = jnp.zeros_like(acc_ref)

    h = h_ref[...]
    a = _silu(_dot(h, wg_ref[0])) * _dot(h, wu_ref[0])
    acc_ref[...] += _dot(a.astype(BF16), wd_ref[0])

    @pl.when(j == pl.num_programs(2) - 1)
    def _():
        o = x_ref[0] + gate_ref[0] * acc_ref[...]
        if final_norm:
            ms = jnp.mean(o * o, axis=-1, keepdims=True)
            o = (o * lax.rsqrt(ms + EPS)) * fnw_ref[...]
        o_ref[0] = o


def _ffn(x, nw, sc, sh, wg, wu, wd, layer, gate, fnw, final_norm, tm, tf):
    b, l, d = x.shape
    ff = wg.shape[2]
    tm = min(tm, l)
    vec = lambda bi, i, j: (bi, 0, 0)
    return pl.pallas_call(
        functools.partial(_ffn_kernel, final_norm=final_norm),
        grid=(b, l // tm, ff // tf),
        in_specs=[pl.BlockSpec((1, tm, d), lambda bi, i, j: (bi, i, 0)),
                  pl.BlockSpec((1, d), lambda bi, i, j: (0, 0)),
                  pl.BlockSpec((1, 1, d), vec),
                  pl.BlockSpec((1, 1, d), vec),
                  pl.BlockSpec((1, d, tf), lambda bi, i, j: (layer, 0, j)),
                  pl.BlockSpec((1, d, tf), lambda bi, i, j: (layer, 0, j)),
                  pl.BlockSpec((1, tf, d), lambda bi, i, j: (layer, j, 0)),
                  pl.BlockSpec((1, 1, d), vec),
                  pl.BlockSpec((1, d), lambda bi, i, j: (0, 0))],
        out_specs=pl.BlockSpec((1, tm, d), lambda bi, i, j: (bi, i, 0)),
        out_shape=jax.ShapeDtypeStruct((b, l, d), F32),
        scratch_shapes=[pltpu.VMEM((tm, d), BF16), pltpu.VMEM((tm, d), F32)],
        compiler_params=_cparams(("parallel", "parallel", "arbitrary")),
        name="ffn",
    )(x, nw.reshape(1, d), sc, sh, wg, wu, wd, gate, fnw.reshape(1, d))


def _box_matrix(n, w):
    idx = np.arange(n)
    lo = np.clip(idx - w // 2, 0, n)
    hi = np.clip(idx + (w - w // 2), 0, n)
    m = ((idx[None, :] >= lo[:, None]) & (idx[None, :] < hi[:, None])).astype(np.float64)
    return m / (hi - lo)[:, None]


POOL_TB = 256
POOL_PAD = 16


def _pool_kernel(u_ref, mh_ref, ml_ref, o_ref, *scratch, grid_rows):
    g = pl.program_id(1)
    l = u_ref.shape[1]
    mh = mh_ref[0]
    ml = ml_ref[0]
    if grid_rows is None:
        v = u_ref[0]
        o_ref[0] = (_dot(mh, v) + _dot(ml, v) - v.astype(F32)).astype(o_ref.dtype)
        return

    cp_ref, = scratch
    pad = POOL_PAD * GRID_W
    cp_ref[0:pad, :] = jnp.zeros((pad, POOL_GC), F32)
    cp_ref[pad + l:pad + l + pad, :] = jnp.zeros((pad, POOL_GC), F32)
    for i in range(l // POOL_TB):
        v = u_ref[0, i * POOL_TB:(i + 1) * POOL_TB, :]
        cp_ref[pad + i * POOL_TB:pad + (i + 1) * POOL_TB, :] = _dot(mh, v) + _dot(ml, v)

    for gi, w in enumerate(POOL_WINDOWS):
        lo_off, hi_off = -(w // 2), w - w // 2

        @pl.when(g == gi)
        def _(lo_off=lo_off, hi_off=hi_off):
            def slab(row):
                return cp_ref[pl.ds(pl.multiple_of(pad + row * GRID_W, GRID_W), GRID_W), :]

            def body(r, s):
                s = s + slab(r + hi_off - 1) - slab(r + lo_off - 1)
                cnt = jnp.minimum(r + hi_off, grid_rows) - jnp.maximum(r + lo_off, 0)
                cntv = jnp.full((GRID_W, POOL_GC), cnt, jnp.int32).astype(F32)
                t0 = pl.multiple_of(r * GRID_W, GRID_W)
                v = u_ref[0, pl.ds(t0, GRID_W), :].astype(F32)
                o_ref[0, pl.ds(t0, GRID_W), :] = (s / cntv - v).astype(o_ref.dtype)
                return s

            s0 = jnp.zeros((GRID_W, POOL_GC), F32)
            for o in range(lo_off - 1, hi_off - 1):
                s0 = s0 + cp_ref[pad + o * GRID_W:pad + (o + 1) * GRID_W, :]
            lax.fori_loop(0, grid_rows, body, s0)


def _pool_diff(p, grid_rows):
    b, l, _ = p.shape
    mats = []
    for w in POOL_WINDOWS:
        if grid_rows is None:
            mats.append(_box_matrix(l, w))
        else:
            mats.append(np.kron(np.eye(POOL_TB // GRID_W), _box_matrix(GRID_W, w)))
    mh, ml = _np_split2(np.stack(mats))
    tb = mats[0].shape[0]
    scratch = []
    if grid_rows is not None:
        scratch = [pltpu.VMEM((l + 2 * POOL_PAD * GRID_W, POOL_GC), F32)]
    cblk = P_POOL // POOL_GC
    return pl.pallas_call(
        functools.partial(_pool_kernel, grid_rows=grid_rows),
        grid=(b, POOL_GROUPS),
        in_specs=[pl.BlockSpec((1, l, POOL_GC), lambda bi, g: (bi, 0, cblk + g)),
                  pl.BlockSpec((1, tb, tb), lambda bi, g: (g, 0, 0)),
                  pl.BlockSpec((1, tb, tb), lambda bi, g: (g, 0, 0))],
        out_specs=pl.BlockSpec((1, l, POOL_GC), lambda bi, g: (bi, 0, g)),
        out_shape=jax.ShapeDtypeStruct((b, l, POOL_W), BF16),
        scratch_shapes=scratch,
        compiler_params=_cparams(("parallel", "parallel")),
        name="pool_diff",
    )(p, mh, ml)


FLIP_ROWS = 512


def _flip_matrix(t):
    m = np.zeros((t, t), np.float32)
    m[np.arange(1, t), t - np.arange(1, t)] = 1.0
    return jnp.asarray(m).astype(BF16)


def _chdft_fold_kernel(u_ref, ua_ref, ub_ref, w_ref, flip_ref, o_ref, nyq_ref):
    i = pl.program_id(2)
    c = FOUR_GC
    tm = u_ref.shape[1]
    w = w_ref[...]
    urev = _dot(flip_ref[...], ua_ref[0])
    row = lax.broadcasted_iota(jnp.int32, (tm, c), 0)
    urev = jnp.where(row == 0, jnp.broadcast_to(ub_ref[0, 0:1, :].astype(F32), (tm, c)), urev).astype(BF16)
    z = _dot(u_ref[0], w)
    zrev = _dot(urev, w)
    first = (row + jnp.minimum(i, 1)) == 0
    o_ref[0, 0, :, :c] = (z[:, :c] + jnp.where(first, 0.0, zrev[:, :c])).astype(o_ref.dtype)
    o_ref[0, 0, :, c:] = (z[:, c:] - zrev[:, c:]).astype(o_ref.dtype)

    @pl.when(i == 0)
    def _():
        nyq_ref[0, 0] = zrev[0:8, :c]


def _chdft_fold(p):
    b, l, _ = p.shape
    half = l // 2
    tm = min(FLIP_ROWS, half)
    nb = l // tm
    k = np.arange(FOUR_GC)
    ang = -2.0 * np.pi * ((k[:, None] * k[None, :]) % FOUR_GC) / FOUR_GC
    w = jnp.asarray(np.concatenate([np.cos(ang), np.sin(ang)], axis=1).astype(np.float32)).astype(BF16)
    cblk = P_FOUR // FOUR_GC
    return pl.pallas_call(
        _chdft_fold_kernel,
        grid=(b, FOUR_GROUPS, half // tm),
        in_specs=[pl.BlockSpec((1, tm, FOUR_GC), lambda bi, g, i: (bi, i, cblk + g)),
                  pl.BlockSpec((1, tm, FOUR_GC), lambda bi, g, i: (bi, nb - 1 - i, cblk + g)),
                  pl.BlockSpec((1, tm, FOUR_GC), lambda bi, g, i: (bi, jnp.where(i == 0, nb // 2, nb - i), cblk + g)),
                  _resident((FOUR_GC, 2 * FOUR_GC), lambda bi, g, i: (0, 0)),
                  _resident((tm, tm), lambda bi, g, i: (0, 0))],
        out_specs=[pl.BlockSpec((1, 1, tm, 2 * FOUR_GC), lambda bi, g, i: (bi, g, i, 0)),
                   pl.BlockSpec((1, 1, 8, FOUR_GC), lambda bi, g, i: (bi, g, 0, 0))],
        out_shape=[jax.ShapeDtypeStruct((b, FOUR_GROUPS, half, 2 * FOUR_GC), BF16),
                   jax.ShapeDtypeStruct((b, FOUR_GROUPS, 8, FOUR_GC), F32)],
        compiler_params=_cparams(("parallel", "parallel", "arbitrary")),
        name="chdft",
    )(p, p, p, w, _flip_matrix(tm))


def _seq_dft_tables(l):
    half = l // 2
    r = int(round(math.sqrt(l)))
    assert r * r == l and r % 2 == 0
    col = np.arange(half)
    hi = 2.0 * np.pi * ((np.arange(r // 2)[:, None] * r * col[None, :]) % l) / l
    lo = 2.0 * np.pi * ((np.arange(r)[:, None] * col[None, :]) % l) / l
    ch, sh = [jnp.asarray(f(hi).astype(np.float32))[:, None, :] for f in (np.cos, np.sin)]
    cl, sl = [jnp.asarray(f(lo).astype(np.float32))[None, :, :] for f in (np.cos, np.sin)]
    cos = (ch * cl - sh * sl).reshape(half, half)
    sin = (sh * cl + ch * sl).reshape(half, half)
    return cos.astype(BF16), sin.astype(BF16)


def _seqdft_kernel(c_ref, s_ref, alt_ref, flip_ref, e_ref, nyq_ref, o_ref, *, scale):
    c = FOUR_GC
    half = e_ref.shape[2]
    ft = flip_ref.shape[0]
    nblk = half // ft
    e = e_ref[0, 0]
    p = _dot(c_ref[...], e[:, :c])
    q = _dot(s_ref[...], e[:, c:])
    zn = nyq_ref[0, 0, 0:1, :]
    k = lax.broadcasted_iota(jnp.int32, (half, c), 0)
    n = jnp.where((k & 1) == 0, zn, -zn)
    o_ref[0, 0:half, :] = ((p + q + n) * scale).astype(o_ref.dtype)
    mir = (p - q + n) * scale
    mir_b = mir.astype(BF16)
    y_half = (_dot(alt_ref[...], e[:, :c])[0:1] + zn) * scale
    row = lax.broadcasted_iota(jnp.int32, (ft, c), 0)
    for jb in range(nblk):
        src = (nblk - 1 - jb) * ft
        hi = _dot(flip_ref[...], mir_b[src:src + ft])
        first = y_half if jb == 0 else mir[src + ft:src + ft + 1]
        hi = jnp.where(row == 0, jnp.broadcast_to(first, (ft, c)), hi)
        o_ref[0, half + jb * ft:half + (jb + 1) * ft, :] = hi.astype(o_ref.dtype)


def _fourier(p):
    b, l, _ = p.shape
    half = l // 2
    ft = min(FLIP_ROWS, half)
    e, nyq = _chdft_fold(p)
    cos, sin = _seq_dft_tables(l)
    alt = jnp.asarray(np.broadcast_to(1.0 - 2.0 * (np.arange(half) % 2), (16, half)).astype(np.float32)).astype(BF16)
    scale = 1.0 / math.sqrt(l * FOUR_GC)
    const2 = lambda bi, g: (0, 0)
    return pl.pallas_call(
        functools.partial(_seqdft_kernel, scale=scale),
        grid=(b, FOUR_GROUPS),
        in_specs=[_resident((half, half), const2),
                  _resident((half, half), const2),
                  _resident((16, half), const2),
                  _resident((ft, ft), const2),
                  pl.BlockSpec((1, 1, half, 2 * FOUR_GC), lambda bi, g: (bi, g, 0, 0)),
                  pl.BlockSpec((1, 1, 8, FOUR_GC), lambda bi, g: (bi, g, 0, 0))],
        out_specs=pl.BlockSpec((1, l, FOUR_GC), lambda bi, g: (bi, 0, g)),
        out_shape=jax.ShapeDtypeStruct((b, l, FOUR_W), BF16),
        compiler_params=_cparams(("parallel", "parallel")),
        name="seqdft",
    )(cos, sin, alt, _flip_matrix(ft), e, nyq)


CONV_CB = 512
_SHIFT_TAPS = tuple(k for k in range(CONV_W) if k != CONV_W // 2)


def _shift_matrix():
    m = np.zeros((len(_SHIFT_TAPS) * CHUNK, CHUNK + 2 * HALO), np.float32)
    t = np.arange(CHUNK)
    for q, k in enumerate(_SHIFT_TAPS):
        m[q * CHUNK + t, HALO - CONV_W // 2 + k + t] = 1.0
    return jnp.asarray(m).astype(BF16)


def _ssd_kernel(*refs, fwd, nc, epilogue):
    if fwd:
        xc_ref, dt_ref, dtb_ref, alog_ref, e_ref, h0_ref = refs[:6]
        rest = refs[6:]
        if epilogue:
            z_ref, yb_ref, dsk_ref, nw_ref, y_ref, hout_ref, st_ref = rest
        else:
            y_ref, hout_ref, st_ref = rest
    else:
        (xm_ref, xp_ref, xn_ref, dt_ref, cw_ref, cb_ref, shift_ref, dtb_ref, alog_ref, e_ref, h0_ref,
         y_ref, xc_ref, hout_ref, st_ref, ext_ref) = refs
    c = pl.program_id(0)
    cc = c if fwd else nc - 1 - c
    T = CHUNK
    nb = dt_ref.shape[0]

    @pl.when(c == 0)
    def _():
        st_ref[...] = h0_ref[...]

    if not fwd:
        @pl.when(cc > 0)
        def _():
            ext_ref[:, 0:HALO, :] = xp_ref[...]

        @pl.when(cc == 0)
        def _():
            ext_ref[:, 0:HALO, :] = jnp.zeros((nb, HALO, XBC), ext_ref.dtype)

        ext_ref[:, HALO:HALO + T, :] = xm_ref[...]

        @pl.when(cc < nc - 1)
        def _():
            ext_ref[:, HALO + T:HALO + T + HALO, :] = xn_ref[...]

        @pl.when(cc == nc - 1)
        def _():
            ext_ref[:, HALO + T:HALO + T + HALO, :] = jnp.zeros((nb, HALO, XBC), ext_ref.dtype)

        shift = shift_ref[...]
        for j in range(XBC // CONV_CB):
            cs = slice(j * CONV_CB, (j + 1) * CONV_CB)
            for bi in range(nb):
                sh = _dot(shift, ext_ref[bi, :, cs])
                acc = cb_ref[:, cs] + xm_ref[bi, :, cs].astype(F32) * cw_ref[CONV_W // 2:CONV_W // 2 + 1, cs]
                for q, k in enumerate(_SHIFT_TAPS):
                    acc = acc + sh[q * T:(q + 1) * T] * cw_ref[k:k + 1, cs]
                xc_ref[bi, :, cs] = _silu(acc).astype(xc_ref.dtype)

    row = lax.broadcasted_iota(jnp.int32, (T, T), 0)
    col = lax.broadcasted_iota(jnp.int32, (T, T), 1)
    causal = (col <= row) if fwd else (col >= row)
    tri = jnp.where(causal, 1.0, 0.0).astype(BF16)
    lane = lax.broadcasted_iota(jnp.int32, (T, 2 * HEAD_DIM), 1)
    hbase = 0 if fwd else SSD_HEADS
    e = e_ref[...]
    a = -jnp.exp(alog_ref[...])

    def head_factors(bi):
        xdt = dt_ref[bi] + dtb_ref[...]
        dt = jnp.maximum(xdt, 0.0) + jnp.log1p(jnp.exp(-jnp.abs(xdt)))
        da = dt * a
        d1 = da.astype(BF16)
        r1 = da - d1.astype(F32)
        d2 = r1.astype(BF16)
        d3 = (r1 - d2.astype(F32)).astype(BF16)
        acs = _dot(tri, d1) + _dot(tri, d2) + _dot(tri, d3)
        tot = acs[T - 1:T, :] if fwd else acs[0:1, :]
        eh, el = _split2(jnp.broadcast_to(jnp.exp(tot), (16, T)))
        ex = _dot(jnp.concatenate([jnp.exp(acs).astype(BF16), (dt * jnp.exp(tot - acs)).astype(BF16), eh, el],
                                  axis=0), e)
        return dict(acs=acs, arow_t=(acs - jnp.log(dt)).T, eacs_x=ex[0:T], w2=ex[T:2 * T],
                    etot_x=ex[2 * T:2 * T + 1] + ex[2 * T + 16:2 * T + 17])

    hf = [head_factors(bi) for bi in range(nb)]

    for g in range(SSD_GROUPS):
        gs = slice(g * GROUP_W, (g + 1) * GROUP_W)
        bs = slice(D_SSD + g * D_STATE, D_SSD + (g + 1) * D_STATE)
        cs_ = slice(D_SSD + (SSD_GROUPS + g) * D_STATE, D_SSD + (SSD_GROUPS + g + 1) * D_STATE)
        for bi in range(nb):
            f = hf[bi]
            bb = xc_ref[bi, :, bs]
            cbf = xc_ref[bi, :, cs_]
            cb = lax.dot_general(cbf, bb, (((1,), (1,)), ((), ())), preferred_element_type=F32)
            st = st_ref[bi, g]
            yoff = _dot(cbf, st.astype(BF16)) * f["eacs_x"][:, gs]
            xb = xc_ref[bi, :, gs]
            ys = []
            for jp in range(HEADS_PER_GROUP // 2):
                ms = []
                for jj in range(2):
                    hc = hbase + g * HEADS_PER_GROUP + 2 * jp + jj
                    diff = f["acs"][:, hc:hc + 1] - f["arow_t"][hc:hc + 1, :]
                    ms.append((cb * jnp.exp(jnp.where(causal, diff, -1e30))).astype(BF16))
                r = _dot(jnp.concatenate(ms, axis=0), xb[:, jp * 2 * HEAD_DIM:(jp + 1) * 2 * HEAD_DIM])
                ys.append(jnp.where(lane < HEAD_DIM, r[:T], r[T:]))
            yg = jnp.concatenate(ys, axis=1) + yoff

            xg = xb.astype(F32)
            xds = (xg * f["w2"][:, gs]).astype(BF16)
            st_ref[bi, g] = st * f["etot_x"][:, gs] + _dot(bb.astype(F32).T.astype(BF16), xds)

            if epilogue:
                yt = yg + yb_ref[bi, :, gs] + dsk_ref[:, gs] * xg
                v = yt * _silu(z_ref[bi, :, gs].astype(F32))
                ms_ = jnp.mean(v * v, axis=-1, keepdims=True)
                y_ref[bi, :, gs] = ((v * lax.rsqrt(ms_ + EPS)) * nw_ref[:, gs]).astype(y_ref.dtype)
            else:
                y_ref[bi, :, gs] = yg

    @pl.when(c == nc - 1)
    def _():
        hout_ref[...] = st_ref[...]


_STATE_TAIL = (SSD_GROUPS, D_STATE, GROUP_W)


def _ssd_bwd(p, pdt, lw, h0):
    b, l, _ = p.shape
    nc = l // CHUNK
    nh = l // HALO
    per = CHUNK // HALO
    cidx = lambda c: nc - 1 - c
    chunk = lambda c: (0, cidx(c), 0)
    const2 = lambda c: (0, 0)
    state = lambda c: (0, 0, 0, 0)
    state_block = (b,) + _STATE_TAIL
    return pl.pallas_call(
        functools.partial(_ssd_kernel, fwd=False, nc=nc, epilogue=False),
        grid=(nc,),
        in_specs=[pl.BlockSpec((b, CHUNK, XBC), chunk),
                  pl.BlockSpec((b, HALO, XBC), lambda c: (0, jnp.maximum(cidx(c) * per - 1, 0), 0)),
                  pl.BlockSpec((b, HALO, XBC), lambda c: (0, jnp.minimum((cidx(c) + 1) * per, nh - 1), 0)),
                  pl.BlockSpec((b, CHUNK, DT_PAD), chunk),
                  _resident((8, XBC), const2),
                  _resident((1, XBC), const2),
                  _resident((len(_SHIFT_TAPS) * CHUNK, CHUNK + 2 * HALO), const2),
                  _resident((1, DT_PAD), const2),
                  _resident((1, DT_PAD), const2),
                  _resident((DT_PAD, D_SSD), const2),
                  _resident(state_block, state)],
        out_specs=[pl.BlockSpec((b, CHUNK, D_SSD), chunk),
                   pl.BlockSpec((b, CHUNK, XBC), chunk),
                   pl.BlockSpec(state_block, state)],
        out_shape=[jax.ShapeDtypeStruct((b, l, D_SSD), F32),
                   jax.ShapeDtypeStruct((b, l, XBC), BF16),
                   jax.ShapeDtypeStruct(state_block, F32)],
        scratch_shapes=[pltpu.VMEM(state_block, F32),
                        pltpu.VMEM((b, CHUNK + 2 * HALO, XBC), BF16)],
        compiler_params=_cparams(("arbitrary",)),
        name="ssd_bwd",
    )(p, p, p, pdt, lw["conv_w"], lw["conv_b"], _shift_matrix(), lw["dt_bias"], lw["a_log"], lw["expand_b"], h0)


def _ssd_fwd(xc, pdt, lw, h0, p=None, yb=None):
    b, l, _ = xc.shape
    nc = l // CHUNK
    chunk = lambda c: (0, c, 0)
    const2 = lambda c: (0, 0)
    state = lambda c: (0, 0, 0, 0)
    state_block = (b,) + _STATE_TAIL
    epilogue = p is not None
    in_specs = [pl.BlockSpec((b, CHUNK, XBC), chunk),
                pl.BlockSpec((b, CHUNK, DT_PAD), chunk),
                _resident((1, DT_PAD), const2),
                _resident((1, DT_PAD), const2),
                _resident((DT_PAD, D_SSD), const2),
                _resident(state_block, state)]
    args = [xc, pdt, lw["dt_bias"], lw["a_log"], lw["expand_f"], h0]
    if epilogue:
        z_blk = P_Z // D_SSD
        in_specs += [pl.BlockSpec((b, CHUNK, D_SSD), lambda c: (0, c, z_blk)),
                     pl.BlockSpec((b, CHUNK, D_SSD), chunk),
                     _resident((1, D_SSD), const2),
                     _resident((1, D_SSD), const2)]
        args += [p, yb, lw["d_skip"], lw["ssd_norm_w"]]
    return pl.pallas_call(
        functools.partial(_ssd_kernel, fwd=True, nc=nc, epilogue=epilogue),
        grid=(nc,),
        in_specs=in_specs,
        out_specs=[pl.BlockSpec((b, CHUNK, D_SSD), chunk),
                   pl.BlockSpec(state_block, state)],
        out_shape=[jax.ShapeDtypeStruct((b, l, D_SSD), BF16 if epilogue else F32),
                   jax.ShapeDtypeStruct(state_block, F32)],
        scratch_shapes=[pltpu.VMEM(state_block, F32)],
        compiler_params=_cparams(("arbitrary",)),
        name="ssd_fwd",
    )(*args)


def _ssd(p, pdt, lw, h0f, h0b):
    yb, xc, hb = _ssd_bwd(p, pdt, lw, h0b)
    y, hf = _ssd_fwd(xc, pdt, lw, h0f, p, yb)
    return y, hf, hb


def _expand_matrix(offset):
    e = np.zeros((DT_PAD, D_SSD), np.float32)
    for h in range(SSD_HEADS):
        e[offset + h, h * HEAD_DIM:(h + 1) * HEAD_DIM] = 1.0
    return jnp.asarray(e).astype(BF16)


def _layer_weights(l, conv_w, conv_b, a_log, dt_bias, d_skip, ssd_norm_w):
    pad_h = (0, DT_PAD - 2 * SSD_HEADS)
    return {
        "conv_w": jnp.pad(conv_w[l], ((0, 8 - CONV_W), (0, 0))),
        "conv_b": conv_b[l].reshape(1, XBC),
        "dt_bias": jnp.pad(dt_bias[l].reshape(-1), pad_h).reshape(1, DT_PAD),
        "a_log": jnp.pad(a_log[l].reshape(-1), pad_h).reshape(1, DT_PAD),
        "d_skip": jnp.repeat(d_skip[l], HEAD_DIM).reshape(1, D_SSD),
        "ssd_norm_w": ssd_norm_w[l].reshape(1, D_SSD),
        "expand_f": _expand_matrix(0),
        "expand_b": _expand_matrix(SSD_HEADS),
    }


def _mixer(x, nw, sc, sh, g_m, wts, lw, layer, grid_rows, h0f, h0b, proj=None):
    b, l, dm = x.shape
    tok = (lambda a: a.reshape(1, b * l, a.shape[-1])) if sc.shape[0] == 1 else (lambda a: a)
    seq = lambda a: a.reshape(b, l, a.shape[-1])
    p, pdt = proj or _inproj(tok(x), nw, sc, sh, wts["w_main"], wts["w_dt"], layer, P_TOT, INPROJ_TM, INPROJ_TN)
    y_n, hf, hb = _ssd(seq(p), seq(pdt), lw, h0f, h0b)
    d = _pool_diff(seq(p), grid_rows)
    f = _fourier(seq(p))
    x = seq(_merge_out(tok(d), tok(f), tok(y_n), p, tok(x), g_m, wts, layer, MERGE_TM))
    return x, hf, hb


def kernel(x, c, ctx, c_ctx, w_ada, b_ada, norm_mix_w, norm_ffn_w, w_in, conv_w, conv_b, a_log, dt_bias,
           d_skip, ssd_norm_w, w_ssd_out, w_pool, pool_scale, w_fourier, w_out, w_ffn_gate, w_ffn_up,
           w_ffn_down, final_norm_w):
    b, seq, d = x.shape
    rows = seq // GRID_W
    mod = _adaln(jnp.concatenate([c, c_ctx[None, :]], axis=0), w_ada, b_ada)
    h0 = jnp.zeros((b,) + _STATE_TAIL, F32)
    lc = ctx.shape[1]
    w_main = _winprep(w_in)
    w_dt = jnp.pad(w_in[:, :, XBC:SSD_IN], ((0, 0), (0, 0), (0, DT_PAD - 2 * SSD_HEADS))).astype(BF16)
    wts = {"w_main": w_main, "w_dt": w_dt, "pool_scale": pool_scale.reshape(DEPTH, 1, D_MODEL)}
    ff = w_ffn_gate.shape[2]
    wg = wu = wd = None
    for l in range(DEPTH):
        last = l == DEPTH - 1
        lw = _layer_weights(l, conv_w, conv_b, a_log, dt_bias, d_skip, ssd_norm_w)
        sh_m, sc_m, g_m, sh_f, sc_f, g_f = [mod[l, :b, i * d:(i + 1) * d].reshape(b, 1, d) for i in range(6)]
        csh_m, csc_m, cg_m, csh_f, csc_f, cg_f = [mod[l, b, i * d:(i + 1) * d].reshape(1, 1, d) for i in range(6)]

        proj = None
        if l == 0:
            r = FFN_CAST_ROWS
            side = [(w_ffn_gate.reshape(DEPTH * d, ff), r), (w_ffn_up.reshape(DEPTH * d, ff), r),
                    (w_ffn_down.reshape(DEPTH * ff, d), 2 * r),
                    (w_out.reshape(DEPTH * d, d), r), (w_ssd_out.reshape(DEPTH * D_SSD, d), r),
                    (w_fourier.reshape(DEPTH * FOUR_W, d), r // 2), (w_pool.reshape(DEPTH * POOL_W, POOL_OUT), r // 2)]
            p0, pdt0, wg, wu, wd, wo, ws, wf, wp = _inproj(x, norm_mix_w[l], sc_m, sh_m, w_main, w_dt, l, P_TOT,
                                                           INPROJ_TM, INPROJ_TN, side)
            proj = (p0, pdt0)
            wg, wu, wd = wg.reshape(DEPTH, d, ff), wu.reshape(DEPTH, d, ff), wd.reshape(DEPTH, ff, d)
            wts.update(w_out=wo.reshape(DEPTH, d, d), w_ssd_out=ws.reshape(DEPTH, D_SSD, d),
                       w_fourier=wf.reshape(DEPTH, FOUR_W, d), w_pool=wp.reshape(DEPTH, POOL_W, POOL_OUT))

        if last:
            pc, pdtc = _inproj(ctx.reshape(1, b * lc, d), norm_mix_w[l], csc_m, csh_m, w_main, w_dt, l,
                               XBC, INPROJ_TM, INPROJ_TN)
            pdtc = pdtc.reshape(b, lc, DT_PAD)
            _, xcc, hb = _ssd_bwd(pc.reshape(b, lc, XBC), pdtc, lw, h0)
            _, hf = _ssd_fwd(xcc, pdtc, lw, h0)
        else:
            ctx, hf, hb = _mixer(ctx, norm_mix_w[l], csc_m, csh_m, cg_m, wts, lw, l, None, h0, h0)
            ctx = _ffn(ctx.reshape(1, b * lc, d), norm_ffn_w[l], csc_f, csh_f, wg, wu, wd, l,
                       cg_f, final_norm_w, False, FFN_TM, FFN_TF).reshape(b, lc, d)

        x, _, _ = _mixer(x, norm_mix_w[l], sc_m, sh_m, g_m, wts, lw, l, rows, hf, hb, proj)
        x = _ffn(x, norm_ffn_w[l], sc_f, sh_f, wg, wu, wd, l, g_f, final_norm_w, last, FFN_TM, FFN_TF)
    return x
```

```python
import functools
import math

import numpy as np
import jax
import jax.numpy as jnp
from jax import lax
from jax.experimental import pallas as pl
from jax.experimental.pallas import tpu as pltpu

F32 = jnp.float32
BF16 = jnp.bfloat16

D_MODEL = 2048
DEPTH = 2
GRID_W = 64
EPS = 1e-6

POOL_GROUPS = 4
POOL_WINDOWS = (2, 4, 8, 16)
POOL_W = D_MODEL // 2
POOL_GC = POOL_W // POOL_GROUPS
POOL_OUT = D_MODEL // POOL_GROUPS

FOUR_GROUPS = 4
FOUR_W = D_MODEL // 2
FOUR_GC = FOUR_W // FOUR_GROUPS

D_SSD = D_MODEL
HEAD_DIM = 64
SSD_HEADS = D_SSD // HEAD_DIM
SSD_GROUPS = 4
HEADS_PER_GROUP = SSD_HEADS // SSD_GROUPS
GROUP_W = HEADS_PER_GROUP * HEAD_DIM
D_STATE = 128
CONV_W = 5
CHUNK = 128

XBC = D_SSD + 2 * SSD_GROUPS * D_STATE
SSD_IN = XBC + 2 * SSD_HEADS
Z_OFF = SSD_IN
POOL_OFF = Z_OFF + D_SSD
FOUR_OFF = POOL_OFF + POOL_W
GATE_OFF = FOUR_OFF + FOUR_W
N_IN = GATE_OFF + 3 * D_MODEL

P_XBC = 0
P_POOL = XBC
P_Z = P_POOL + POOL_W
P_GATE = P_Z + D_SSD
P_FOUR = P_GATE + 3 * D_MODEL
P_TOT = P_FOUR + FOUR_W
DT_PAD = 128

V7X_VMEM_LIMIT = 56 * 1024 * 1024
HALO = 16

INPROJ_TM, INPROJ_TN = 1024, 1024
MERGE_TM = 256
FFN_TM, FFN_TF = 512, 512
ADALN_TN = 1024
FFN_CAST_ROWS = 64


def _cparams(sem):
    return pltpu.CompilerParams(dimension_semantics=sem, vmem_limit_bytes=V7X_VMEM_LIMIT)


def _resident(shape, index_map):
    return pl.BlockSpec(shape, index_map, pipeline_mode=pl.Buffered(1))


def _split2(v):
    hi = v.astype(BF16)
    lo = (v - hi.astype(F32)).astype(BF16)
    return hi, lo


def _dot(a, b):
    return jnp.dot(a, b, preferred_element_type=F32)


def _sigmoid(v):
    return 0.5 * jnp.tanh(0.5 * v) + 0.5


def _silu(v):
    return v * _sigmoid(v)


def _np_split2(m):
    m = jnp.asarray(np.asarray(m, np.float32))
    hi = m.astype(BF16)
    lo = (m - hi.astype(F32)).astype(BF16)
    return hi, lo


WIN_CHUNK = 1024

_WIN_SEGMENTS = ((P_XBC, 0, XBC), (P_POOL, POOL_OFF, POOL_W), (P_Z, Z_OFF, D_SSD),
                 (P_GATE, GATE_OFF, 3 * D_MODEL), (P_FOUR, FOUR_OFF, FOUR_W))


def _win_src(j):
    src = jnp.int32(0)
    for dst0, src0, width in _WIN_SEGMENTS:
        first = dst0 // WIN_CHUNK
        inside = (j >= first) & (j < first + width // WIN_CHUNK)
        src = jnp.where(inside, src0 + (j - first) * WIN_CHUNK, src)
    return pl.multiple_of(src, 2 * SSD_HEADS)


def _winprep_kernel(wt_ref, o_ref):
    o_ref[0] = wt_ref[0].T.astype(BF16)


def _winprep(w_in):
    depth, d, n = w_in.shape
    wt = jnp.swapaxes(w_in, 1, 2)
    return pl.pallas_call(
        _winprep_kernel,
        grid=(depth, P_TOT // WIN_CHUNK),
        in_specs=[pl.BlockSpec((pl.Element(1), pl.Element(WIN_CHUNK), pl.Element(d)),
                               lambda l, j: (l, _win_src(j), 0))],
        out_specs=pl.BlockSpec((1, d, WIN_CHUNK), lambda l, j: (l, 0, j)),
        out_shape=jax.ShapeDtypeStruct((depth, d, P_TOT), BF16),
        compiler_params=_cparams(("parallel", "parallel")),
        name="winprep",
    )(wt)


ADA_ROWS = 8


def _adaln_kernel(ct_ref, w_ref, b_ref, o_ref, *, n_rows):
    s = _silu(ct_ref[...])
    w = w_ref[0]
    tn = w.shape[1]
    rows = [jnp.sum(w * s[:, r:r + 1], axis=0, keepdims=True) for r in range(n_rows)]
    rows.append(jnp.zeros((ADA_ROWS - n_rows, tn), F32))
    o_ref[0] = jnp.concatenate(rows, axis=0) + b_ref[0]


def _adaln(cc, w_ada, b_ada):
    depth, d, n = w_ada.shape
    tn = ADALN_TN
    n_rows = cc.shape[0]
    ct = jnp.pad(cc, ((0, ADA_ROWS - n_rows), (0, 0))).T
    return pl.pallas_call(
        functools.partial(_adaln_kernel, n_rows=n_rows),
        grid=(depth, n // tn),
        in_specs=[pl.BlockSpec((d, ADA_ROWS), lambda l, j: (0, 0)),
                  pl.BlockSpec((1, d, tn), lambda l, j: (l, 0, j)),
                  pl.BlockSpec((1, 1, tn), lambda l, j: (l, 0, j))],
        out_specs=pl.BlockSpec((1, ADA_ROWS, tn), lambda l, j: (l, 0, j)),
        out_shape=jax.ShapeDtypeStruct((depth, ADA_ROWS, n), F32),
        compiler_params=_cparams(("parallel", "parallel")),
        name="adaln",
    )(ct, w_ada, b_ada.reshape(depth, 1, n))


def _norm_mod(x, nw, sc, sh):
    ms = jnp.mean(x * x, axis=-1, keepdims=True)
    return (x * lax.rsqrt(ms + EPS)) * nw * (1.0 + sc) + sh


def _inproj_kernel(x_ref, nw_ref, sc_ref, sh_ref, w_ref, wdt_ref, *rest, n_side):
    side_in = rest[:n_side]
    o_ref, dt_ref = rest[n_side:n_side + 2]
    side_out = rest[n_side + 2:2 * n_side + 2]
    h_ref = rest[-1]

    @pl.when(pl.program_id(2) == 0)
    def _():
        h = _norm_mod(x_ref[0], nw_ref[...], sc_ref[0], sh_ref[0]).astype(BF16)
        h_ref[...] = h
        dt_ref[0] = _dot(h, wdt_ref[0])

    o_ref[0] = _dot(h_ref[...], w_ref[0]).astype(o_ref.dtype)

    for wi_ref, wo_ref in zip(side_in, side_out):
        wo_ref[...] = wi_ref[...].astype(BF16)


def _inproj(x, nw, sc, sh, w, wdt, layer, n_out, tm, tn, side=()):
    b, l, d = x.shape
    tm = min(tm, l)
    n_i, n_j = l // tm, n_out // tn
    side_specs, side_shapes = [], []
    for arr, rows in side:
        nblk = arr.shape[0] // rows
        assert arr.shape[0] % rows == 0 and nblk <= b * n_i * n_j
        imap = lambda bi, i, j, nblk=nblk: (jnp.minimum((bi * n_i + i) * n_j + j, nblk - 1), 0)
        side_specs.append(pl.BlockSpec((rows, arr.shape[1]), imap))
        side_shapes.append(jax.ShapeDtypeStruct(arr.shape, BF16))
    sem = ("arbitrary",) * 3 if side else ("parallel", "parallel", "arbitrary")
    return pl.pallas_call(
        functools.partial(_inproj_kernel, n_side=len(side)),
        grid=(b, n_i, n_j),
        in_specs=[pl.BlockSpec((1, tm, d), lambda bi, i, j: (bi, i, 0)),
                  pl.BlockSpec((1, d), lambda bi, i, j: (0, 0)),
                  pl.BlockSpec((1, 1, d), lambda bi, i, j: (bi, 0, 0)),
                  pl.BlockSpec((1, 1, d), lambda bi, i, j: (bi, 0, 0)),
                  pl.BlockSpec((1, d, tn), lambda bi, i, j: (layer, 0, j)),
                  _resident((1, d, DT_PAD), lambda bi, i, j: (layer, 0, 0))] + side_specs,
        out_specs=[pl.BlockSpec((1, tm, tn), lambda bi, i, j: (bi, i, j)),
                   pl.BlockSpec((1, tm, DT_PAD), lambda bi, i, j: (bi, i, 0))] + side_specs,
        out_shape=[jax.ShapeDtypeStruct((b, l, n_out), BF16),
                   jax.ShapeDtypeStruct((b, l, DT_PAD), F32)] + side_shapes,
        scratch_shapes=[pltpu.VMEM((tm, d), BF16)],
        compiler_params=_cparams(sem),
        name="inproj",
    )(x, nw.reshape(1, d), sc, sh, w, wdt, *[arr for arr, _ in side])


def _merge_kernel(d_ref, f_ref, yn_ref, gp_ref, gf_ref, gs_ref, x_ref, gm_ref,
                  wp_ref, ps_ref, wf_ref, ws_ref, wo_ref, o_ref):
    d = d_ref[0]
    y_pool = jnp.concatenate(
        [_dot(d[:, g * POOL_GC:(g + 1) * POOL_GC], wp_ref[0, g * POOL_GC:(g + 1) * POOL_GC, :])
         for g in range(POOL_GROUPS)], axis=1)
    m = _sigmoid(gp_ref[0].astype(F32)) * (y_pool * ps_ref[0])
    m = m + _sigmoid(gf_ref[0].astype(F32)) * _dot(f_ref[0], wf_ref[0])
    m = m + _sigmoid(gs_ref[0].astype(F32)) * _dot(yn_ref[0], ws_ref[0])
    o_ref[0] = x_ref[0] + gm_ref[0] * _dot(m.astype(BF16), wo_ref[0])


def _merge_out(d, f, yn, p, x, g_m, wts, layer, tm):
    b, l, dm = x.shape
    tm = min(tm, l)
    gblk = P_GATE // dm
    row = lambda bi, i: (bi, i, 0)
    lyr = lambda bi, i: (layer, 0, 0)
    return pl.pallas_call(
        _merge_kernel,
        grid=(b, l // tm),
        in_specs=[pl.BlockSpec((1, tm, POOL_W), row),
                  pl.BlockSpec((1, tm, FOUR_W), row),
                  pl.BlockSpec((1, tm, D_SSD), row),
                  pl.BlockSpec((1, tm, dm), lambda bi, i: (bi, i, gblk)),
                  pl.BlockSpec((1, tm, dm), lambda bi, i: (bi, i, gblk + 1)),
                  pl.BlockSpec((1, tm, dm), lambda bi, i: (bi, i, gblk + 2)),
                  pl.BlockSpec((1, tm, dm), row),
                  pl.BlockSpec((1, 1, dm), lambda bi, i: (bi, 0, 0)),
                  _resident((1, POOL_W, POOL_OUT), lyr),
                  _resident((1, 1, dm), lyr),
                  _resident((1, FOUR_W, dm), lyr),
                  _resident((1, D_SSD, dm), lyr),
                  _resident((1, dm, dm), lyr)],
        out_specs=pl.BlockSpec((1, tm, dm), row),
        out_shape=jax.ShapeDtypeStruct((b, l, dm), F32),
        compiler_params=_cparams(("parallel", "parallel")),
        name="merge_out",
    )(d, f, yn, p, p, p, x, g_m, wts["w_pool"], wts["pool_scale"], wts["w_fourier"], wts["w_ssd_out"],
      wts["w_out"])


def _ffn_kernel(x_ref, nw_ref, sc_ref, sh_ref, wg_ref, wu_ref, wd_ref, gate_ref, fnw_ref,
                o_ref, h_ref, acc_ref, *, final_norm):
    j = pl.program_id(2)

    @pl.when(j == 0)
    def _():
        h_ref[...] = _norm_mod(x_ref[0], nw_ref[...], sc_ref[0], sh_ref[0]).astype(BF16)
        acc_ref[...] = jnp.zeros_like(acc_ref)

    h = h_ref[...]
    a = _silu(_dot(h, wg_ref[0])) * _dot(h, wu_ref[0])
    acc_ref[...] += _dot(a.astype(BF16), wd_ref[0])

    @pl.when(j == pl.num_programs(2) - 1)
    def _():
        o = x_ref[0] + gate_ref[0] * acc_ref[...]
        if final_norm:
            ms = jnp.mean(o * o, axis=-1, keepdims=True)
            o = (o * lax.rsqrt(ms + EPS)) * fnw_ref[...]
        o_ref[0] = o


def _ffn(x, nw, sc, sh, wg, wu, wd, layer, gate, fnw, final_norm, tm, tf):
    b, l, d = x.shape
    ff = wg.shape[2]
    tm = min(tm, l)
    vec = lambda bi, i, j: (bi, 0, 0)
    return pl.pallas_call(
        functools.partial(_ffn_kernel, final_norm=final_norm),
        grid=(b, l // tm, ff // tf),
        in_specs=[pl.BlockSpec((1, tm, d), lambda bi, i, j: (bi, i, 0)),
                  pl.BlockSpec((1, d), lambda bi, i, j: (0, 0)),
                  pl.BlockSpec((1, 1, d), vec),
                  pl.BlockSpec((1, 1, d), vec),
                  pl.BlockSpec((1, d, tf), lambda bi, i, j: (layer, 0, j)),
                  pl.BlockSpec((1, d, tf), lambda bi, i, j: (layer, 0, j)),
                  pl.BlockSpec((1, tf, d), lambda bi, i, j: (layer, j, 0)),
                  pl.BlockSpec((1, 1, d), vec),
                  pl.BlockSpec((1, d), lambda bi, i, j: (0, 0))],
        out_specs=pl.BlockSpec((1, tm, d), lambda bi, i, j: (bi, i, 0)),
        out_shape=jax.ShapeDtypeStruct((b, l, d), F32),
        scratch_shapes=[pltpu.VMEM((tm, d), BF16), pltpu.VMEM((tm, d), F32)],
        compiler_params=_cparams(("parallel", "parallel", "arbitrary")),
        name="ffn",
    )(x, nw.reshape(1, d), sc, sh, wg, wu, wd, gate, fnw.reshape(1, d))


def _box_matrix(n, w):
    idx = np.arange(n)
    lo = np.clip(idx - w // 2, 0, n)
    hi = np.clip(idx + (w - w // 2), 0, n)
    m = ((idx[None, :] >= lo[:, None]) & (idx[None, :] < hi[:, None])).astype(np.float64)
    return m / (hi - lo)[:, None]


POOL_TB = 256
POOL_PAD = 16


def _pool_kernel(u_ref, mh_ref, ml_ref, o_ref, *scratch, grid_rows):
    g = pl.program_id(1)
    l = u_ref.shape[1]
    mh = mh_ref[0]
    ml = ml_ref[0]
    if grid_rows is None:
        v = u_ref[0]
        o_ref[0] = (_dot(mh, v) + _dot(ml, v) - v.astype(F32)).astype(o_ref.dtype)
        return

    cp_ref, = scratch
    pad = POOL_PAD * GRID_W
    cp_ref[0:pad, :] = jnp.zeros((pad, POOL_GC), F32)
    cp_ref[pad + l:pad + l + pad, :] = jnp.zeros((pad, POOL_GC), F32)
    for i in range(l // POOL_TB):
        v = u_ref[0, i * POOL_TB:(i + 1) * POOL_TB, :]
        cp_ref[pad + i * POOL_TB:pad + (i + 1) * POOL_TB, :] = _dot(mh, v) + _dot(ml, v)

    for gi, w in enumerate(POOL_WINDOWS):
        lo_off, hi_off = -(w // 2), w - w // 2

        @pl.when(g == gi)
        def _(lo_off=lo_off, hi_off=hi_off):
            def slab(row):
                return cp_ref[pl.ds(pl.multiple_of(pad + row * GRID_W, GRID_W), GRID_W), :]

            def body(r, s):
                s = s + slab(r + hi_off - 1) - slab(r + lo_off - 1)
                cnt = jnp.minimum(r + hi_off, grid_rows) - jnp.maximum(r + lo_off, 0)
                cntv = jnp.full((GRID_W, POOL_GC), cnt, jnp.int32).astype(F32)
                t0 = pl.multiple_of(r * GRID_W, GRID_W)
                v = u_ref[0, pl.ds(t0, GRID_W), :].astype(F32)
                o_ref[0, pl.ds(t0, GRID_W), :] = (s / cntv - v).astype(o_ref.dtype)
                return s

            s0 = jnp.zeros((GRID_W, POOL_GC), F32)
            for o in range(lo_off - 1, hi_off - 1):
                s0 = s0 + cp_ref[pad + o * GRID_W:pad + (o + 1) * GRID_W, :]
            lax.fori_loop(0, grid_rows, body, s0)


def _pool_diff(p, grid_rows):
    b, l, _ = p.shape
    mats = []
    for w in POOL_WINDOWS:
        if grid_rows is None:
            mats.append(_box_matrix(l, w))
        else:
            mats.append(np.kron(np.eye(POOL_TB // GRID_W), _box_matrix(GRID_W, w)))
    mh, ml = _np_split2(np.stack(mats))
    tb = mats[0].shape[0]
    scratch = []
    if grid_rows is not None:
        scratch = [pltpu.VMEM((l + 2 * POOL_PAD * GRID_W, POOL_GC), F32)]
    cblk = P_POOL // POOL_GC
    return pl.pallas_call(
        functools.partial(_pool_kernel, grid_rows=grid_rows),
        grid=(b, POOL_GROUPS),
        in_specs=[pl.BlockSpec((1, l, POOL_GC), lambda bi, g: (bi, 0, cblk + g)),
                  pl.BlockSpec((1, tb, tb), lambda bi, g: (g, 0, 0)),
                  pl.BlockSpec((1, tb, tb), lambda bi, g: (g, 0, 0))],
        out_specs=pl.BlockSpec((1, l, POOL_GC), lambda bi, g: (bi, 0, g)),
        out_shape=jax.ShapeDtypeStruct((b, l, POOL_W), BF16),
        scratch_shapes=scratch,
        compiler_params=_cparams(("parallel", "parallel")),
        name="pool_diff",
    )(p, mh, ml)


FLIP_ROWS = 512


def _flip_matrix(t):
    m = np.zeros((t, t), np.float32)
    m[np.arange(1, t), t - np.arange(1, t)] = 1.0
    return jnp.asarray(m).astype(BF16)


def _chdft_fold_kernel(u_ref, ua_ref, ub_ref, w_ref, flip_ref, o_ref, nyq_ref):
    i = pl.program_id(2)
    c = FOUR_GC
    tm = u_ref.shape[1]
    w = w_ref[...]
    urev = _dot(flip_ref[...], ua_ref[0])
    row = lax.broadcasted_iota(jnp.int32, (tm, c), 0)
    urev = jnp.where(row == 0, jnp.broadcast_to(ub_ref[0, 0:1, :].astype(F32), (tm, c)), urev).astype(BF16)
    z = _dot(u_ref[0], w)
    zrev = _dot(urev, w)
    first = (row + jnp.minimum(i, 1)) == 0
    o_ref[0, 0, :, :c] = (z[:, :c] + jnp.where(first, 0.0, zrev[:, :c])).astype(o_ref.dtype)
    o_ref[0, 0, :, c:] = (z[:, c:] - zrev[:, c:]).astype(o_ref.dtype)

    @pl.when(i == 0)
    def _():
        nyq_ref[0, 0] = zrev[0:8, :c]


def _chdft_fold(p):
    b, l, _ = p.shape
    half = l // 2
    tm = min(FLIP_ROWS, half)
    nb = l // tm
    k = np.arange(FOUR_GC)
    ang = -2.0 * np.pi * ((k[:, None] * k[None, :]) % FOUR_GC) / FOUR_GC
    w = jnp.asarray(np.concatenate([np.cos(ang), np.sin(ang)], axis=1).astype(np.float32)).astype(BF16)
    cblk = P_FOUR // FOUR_GC
    return pl.pallas_call(
        _chdft_fold_kernel,
        grid=(b, FOUR_GROUPS, half // tm),
        in_specs=[pl.BlockSpec((1, tm, FOUR_GC), lambda bi, g, i: (bi, i, cblk + g)),
                  pl.BlockSpec((1, tm, FOUR_GC), lambda bi, g, i: (bi, nb - 1 - i, cblk + g)),
                  pl.BlockSpec((1, tm, FOUR_GC), lambda bi, g, i: (bi, jnp.where(i == 0, nb // 2, nb - i), cblk + g)),
                  _resident((FOUR_GC, 2 * FOUR_GC), lambda bi, g, i: (0, 0)),
                  _resident((tm, tm), lambda bi, g, i: (0, 0))],
        out_specs=[pl.BlockSpec((1, 1, tm, 2 * FOUR_GC), lambda bi, g, i: (bi, g, i, 0)),
                   pl.BlockSpec((1, 1, 8, FOUR_GC), lambda bi, g, i: (bi, g, 0, 0))],
        out_shape=[jax.ShapeDtypeStruct((b, FOUR_GROUPS, half, 2 * FOUR_GC), BF16),
                   jax.ShapeDtypeStruct((b, FOUR_GROUPS, 8, FOUR_GC), F32)],
        compiler_params=_cparams(("parallel", "parallel", "arbitrary")),
        name="chdft",
    )(p, p, p, w, _flip_matrix(tm))


def _seq_dft_tables(l):
    half = l // 2
    r = int(round(math.sqrt(l)))
    assert r * r == l and r % 2 == 0
    col = np.arange(half)
    hi = 2.0 * np.pi * ((np.arange(r // 2)[:, None] * r * col[None, :]) % l) / l
    lo = 2.0 * np.pi * ((np.arange(r)[:, None] * col[None, :]) % l) / l
    ch, sh = [jnp.asarray(f(hi).astype(np.float32))[:, None, :] for f in (np.cos, np.sin)]
    cl, sl = [jnp.asarray(f(lo).astype(np.float32))[None, :, :] for f in (np.cos, np.sin)]
    cos = (ch * cl - sh * sl).reshape(half, half)
    sin = (sh * cl + ch * sl).reshape(half, half)
    return cos.astype(BF16), sin.astype(BF16)


def _seqdft_kernel(c_ref, s_ref, alt_ref, flip_ref, e_ref, nyq_ref, o_ref, *, scale):
    c = FOUR_GC
    half = e_ref.shape[2]
    ft = flip_ref.shape[0]
    nblk = half // ft
    e = e_ref[0, 0]
    p = _dot(c_ref[...], e[:, :c])
    q = _dot(s_ref[...], e[:, c:])
    zn = nyq_ref[0, 0, 0:1, :]
    k = lax.broadcasted_iota(jnp.int32, (half, c), 0)
    n = jnp.where((k & 1) == 0, zn, -zn)
    o_ref[0, 0:half, :] = ((p + q + n) * scale).astype(o_ref.dtype)
    mir = (p - q + n) * scale
    mir_b = mir.astype(BF16)
    y_half = (_dot(alt_ref[...], e[:, :c])[0:1] + zn) * scale
    row = lax.broadcasted_iota(jnp.int32, (ft, c), 0)
    for jb in range(nblk):
        src = (nblk - 1 - jb) * ft
        hi = _dot(flip_ref[...], mir_b[src:src + ft])
        first = y_half if jb == 0 else mir[src + ft:src + ft + 1]
        hi = jnp.where(row == 0, jnp.broadcast_to(first, (ft, c)), hi)
        o_ref[0, half + jb * ft:half + (jb + 1) * ft, :] = hi.astype(o_ref.dtype)


def _fourier(p):
    b, l, _ = p.shape
    half = l // 2
    ft = min(FLIP_ROWS, half)
    e, nyq = _chdft_fold(p)
    cos, sin = _seq_dft_tables(l)
    alt = jnp.asarray(np.broadcast_to(1.0 - 2.0 * (np.arange(half) % 2), (16, half)).astype(np.float32)).astype(BF16)
    scale = 1.0 / math.sqrt(l * FOUR_GC)
    const2 = lambda bi, g: (0, 0)
    return pl.pallas_call(
        functools.partial(_seqdft_kernel, scale=scale),
        grid=(b, FOUR_GROUPS),
        in_specs=[_resident((half, half), const2),
                  _resident((half, half), const2),
                  _resident((16, half), const2),
                  _resident((ft, ft), const2),
                  pl.BlockSpec((1, 1, half, 2 * FOUR_GC), lambda bi, g: (bi, g, 0, 0)),
                  pl.BlockSpec((1, 1, 8, FOUR_GC), lambda bi, g: (bi, g, 0, 0))],
        out_specs=pl.BlockSpec((1, l, FOUR_GC), lambda bi, g: (bi, 0, g)),
        out_shape=jax.ShapeDtypeStruct((b, l, FOUR_W), BF16),
        compiler_params=_cparams(("parallel", "parallel")),
        name="seqdft",
    )(cos, sin, alt, _flip_matrix(ft), e, nyq)


CONV_CB = 512
_SHIFT_TAPS = tuple(k for k in range(CONV_W) if k != CONV_W // 2)


def _shift_matrix():
    m = np.zeros((len(_SHIFT_TAPS) * CHUNK, CHUNK + 2 * HALO), np.float32)
    t = np.arange(CHUNK)
    for q, k in enumerate(_SHIFT_TAPS):
        m[q * CHUNK + t, HALO - CONV_W // 2 + k + t] = 1.0
    return jnp.asarray(m).astype(BF16)


def _ssd_kernel(*refs, fwd, nc, epilogue):
    if fwd:
        xc_ref, dt_ref, dtb_ref, alog_ref, e_ref, h0_ref = refs[:6]
        rest = refs[6:]
        if epilogue:
            z_ref, yb_ref, dsk_ref, nw_ref, y_ref, hout_ref, st_ref = rest
        else:
            y_ref, hout_ref, st_ref = rest
    else:
        (xm_ref, xp_ref, xn_ref, dt_ref, cw_ref, cb_ref, shift_ref, dtb_ref, alog_ref, e_ref, h0_ref,
         y_ref, xc_ref, hout_ref, st_ref, ext_ref) = refs
    c = pl.program_id(0)
    cc = c if fwd else nc - 1 - c
    T = CHUNK
    nb = dt_ref.shape[0]

    @pl.when(c == 0)
    def _():
        st_ref[...] = h0_ref[...]

    if not fwd:
        @pl.when(cc > 0)
        def _():
            ext_ref[:, 0:HALO, :] = xp_ref[...]

        @pl.when(cc == 0)
        def _():
            ext_ref[:, 0:HALO, :] = jnp.zeros((nb, HALO, XBC), ext_ref.dtype)

        ext_ref[:, HALO:HALO + T, :] = xm_ref[...]

        @pl.when(cc < nc - 1)
        def _():
            ext_ref[:, HALO + T:HALO + T + HALO, :] = xn_ref[...]

        @pl.when(cc == nc - 1)
        def _():
            ext_ref[:, HALO + T:HALO + T + HALO, :] = jnp.zeros((nb, HALO, XBC), ext_ref.dtype)

        shift = shift_ref[...]
        for j in range(XBC // CONV_CB):
            cs = slice(j * CONV_CB, (j + 1) * CONV_CB)
            for bi in range(nb):
                sh = _dot(shift, ext_ref[bi, :, cs])
                acc = cb_ref[:, cs] + xm_ref[bi, :, cs].astype(F32) * cw_ref[CONV_W // 2:CONV_W // 2 + 1, cs]
                for q, k in enumerate(_SHIFT_TAPS):
                    acc = acc + sh[q * T:(q + 1) * T] * cw_ref[k:k + 1, cs]
                xc_ref[bi, :, cs] = _silu(acc).astype(xc_ref.dtype)

    row = lax.broadcasted_iota(jnp.int32, (T, T), 0)
    col = lax.broadcasted_iota(jnp.int32, (T, T), 1)
    causal = (col <= row) if fwd else (col >= row)
    tri = jnp.where(causal, 1.0, 0.0).astype(BF16)
    lane = lax.broadcasted_iota(jnp.int32, (T, 2 * HEAD_DIM), 1)
    hbase = 0 if fwd else SSD_HEADS
    e = e_ref[...]
    a = -jnp.exp(alog_ref[...])

    def head_factors(bi):
        xdt = dt_ref[bi] + dtb_ref[...]
        dt = jnp.maximum(xdt, 0.0) + jnp.log1p(jnp.exp(-jnp.abs(xdt)))
        da = dt * a
        d1 = da.astype(BF16)
        r1 = da - d1.astype(F32)
        d2 = r1.astype(BF16)
        d3 = (r1 - d2.astype(F32)).astype(BF16)
        acs = _dot(tri, d1) + _dot(tri, d2) + _dot(tri, d3)
        tot = acs[T - 1:T, :] if fwd else acs[0:1, :]
        eh, el = _split2(jnp.broadcast_to(jnp.exp(tot), (16, T)))
        ex = _dot(jnp.concatenate([jnp.exp(acs).astype(BF16), (dt * jnp.exp(tot - acs)).astype(BF16), eh, el],
                                  axis=0), e)
        return dict(acs=acs, arow_t=(acs - jnp.log(dt)).T, eacs_x=ex[0:T], w2=ex[T:2 * T],
                    etot_x=ex[2 * T:2 * T + 1] + ex[2 * T + 16:2 * T + 17])

    hf = [head_factors(bi) for bi in range(nb)]

    for g in range(SSD_GROUPS):
        gs = slice(g * GROUP_W, (g + 1) * GROUP_W)
        bs = slice(D_SSD + g * D_STATE, D_SSD + (g + 1) * D_STATE)
        cs_ = slice(D_SSD + (SSD_GROUPS + g) * D_STATE, D_SSD + (SSD_GROUPS + g + 1) * D_STATE)
        for bi in range(nb):
            f = hf[bi]
            bb = xc_ref[bi, :, bs]
            cbf = xc_ref[bi, :, cs_]
            cb = lax.dot_general(cbf, bb, (((1,), (1,)), ((), ())), preferred_element_type=F32)
            st = st_ref[bi, g]
            yoff = _dot(cbf, st.astype(BF16)) * f["eacs_x"][:, gs]
            xb = xc_ref[bi, :, gs]
            ys = []
            for jp in range(HEADS_PER_GROUP // 2):
                ms = []
                for jj in range(2):
                    hc = hbase + g * HEADS_PER_GROUP + 2 * jp + jj
                    diff = f["acs"][:, hc:hc + 1] - f["arow_t"][hc:hc + 1, :]
                    ms.append((cb * jnp.exp(jnp.where(causal, diff, -1e30))).astype(BF16))
                r = _dot(jnp.concatenate(ms, axis=0), xb[:, jp * 2 * HEAD_DIM:(jp + 1) * 2 * HEAD_DIM])
                ys.append(jnp.where(lane < HEAD_DIM, r[:T], r[T:]))
            yg = jnp.concatenate(ys, axis=1) + yoff

            xg = xb.astype(F32)
            xds = (xg * f["w2"][:, gs]).astype(BF16)
            st_ref[bi, g] = st * f["etot_x"][:, gs] + _dot(bb.astype(F32).T.astype(BF16), xds)

            if epilogue:
                yt = yg + yb_ref[bi, :, gs] + dsk_ref[:, gs] * xg
                v = yt * _silu(z_ref[bi, :, gs].astype(F32))
                ms_ = jnp.mean(v * v, axis=-1, keepdims=True)
                y_ref[bi, :, gs] = ((v * lax.rsqrt(ms_ + EPS)) * nw_ref[:, gs]).astype(y_ref.dtype)
            else:
                y_ref[bi, :, gs] = yg

    @pl.when(c == nc - 1)
    def _():
        hout_ref[...] = st_ref[...]


_STATE_TAIL = (SSD_GROUPS, D_STATE, GROUP_W)


def _ssd_bwd(p, pdt, lw, h0):
    b, l, _ = p.shape
    nc = l // CHUNK
    nh = l // HALO
    per = CHUNK // HALO
    cidx = lambda c: nc - 1 - c
    chunk = lambda c: (0, cidx(c), 0)
    const2 = lambda c: (0, 0)
    state = lambda c: (0, 0, 0, 0)
    state_block = (b,) + _STATE_TAIL
    return pl.pallas_call(
        functools.partial(_ssd_kernel, fwd=False, nc=nc, epilogue=False),
        grid=(nc,),
        in_specs=[pl.BlockSpec((b, CHUNK, XBC), chunk),
                  pl.BlockSpec((b, HALO, XBC), lambda c: (0, jnp.maximum(cidx(c) * per - 1, 0), 0)),
                  pl.BlockSpec((b, HALO, XBC), lambda c: (0, jnp.minimum((cidx(c) + 1) * per, nh - 1), 0)),
                  pl.BlockSpec((b, CHUNK, DT_PAD), chunk),
                  _resident((8, XBC), const2),
                  _resident((1, XBC), const2),
                  _resident((len(_SHIFT_TAPS) * CHUNK, CHUNK + 2 * HALO), const2),
                  _resident((1, DT_PAD), const2),
                  _resident((1, DT_PAD), const2),
                  _resident((DT_PAD, D_SSD), const2),
                  _resident(state_block, state)],
        out_specs=[pl.BlockSpec((b, CHUNK, D_SSD), chunk),
                   pl.BlockSpec((b, CHUNK, XBC), chunk),
                   pl.BlockSpec(state_block, state)],
        out_shape=[jax.ShapeDtypeStruct((b, l, D_SSD), F32),
                   jax.ShapeDtypeStruct((b, l, XBC), BF16),
                   jax.ShapeDtypeStruct(state_block, F32)],
        scratch_shapes=[pltpu.VMEM(state_block, F32),
                        pltpu.VMEM((b, CHUNK + 2 * HALO, XBC), BF16)],
        compiler_params=_cparams(("arbitrary",)),
        name="ssd_bwd",
    )(p, p, p, pdt, lw["conv_w"], lw["conv_b"], _shift_matrix(), lw["dt_bias"], lw["a_log"], lw["expand_b"], h0)


def _ssd_fwd(xc, pdt, lw, h0, p=None, yb=None):
    b, l, _ = xc.shape
    nc = l // CHUNK
    chunk = lambda c: (0, c, 0)
    const2 = lambda c: (0, 0)
    state = lambda c: (0, 0, 0, 0)
    state_block = (b,) + _STATE_TAIL
    epilogue = p is not None
    in_specs = [pl.BlockSpec((b, CHUNK, XBC), chunk),
                pl.BlockSpec((b, CHUNK, DT_PAD), chunk),
                _resident((1, DT_PAD), const2),
                _resident((1, DT_PAD), const2),
                _resident((DT_PAD, D_SSD), const2),
                _resident(state_block, state)]
    args = [xc, pdt, lw["dt_bias"], lw["a_log"], lw["expand_f"], h0]
    if epilogue:
        z_blk = P_Z // D_SSD
        in_specs += [pl.BlockSpec((b, CHUNK, D_SSD), lambda c: (0, c, z_blk)),
                     pl.BlockSpec((b, CHUNK, D_SSD), chunk),
                     _resident((1, D_SSD), const2),
                     _resident((1, D_SSD), const2)]
        args += [p, yb, lw["d_skip"], lw["ssd_norm_w"]]
    return pl.pallas_call(
        functools.partial(_ssd_kernel, fwd=True, nc=nc, epilogue=epilogue),
        grid=(nc,),
        in_specs=in_specs,
        out_specs=[pl.BlockSpec((b, CHUNK, D_SSD), chunk),
                   pl.BlockSpec(state_block, state)],
        out_shape=[jax.ShapeDtypeStruct((b, l, D_SSD), BF16 if epilogue else F32),
                   jax.ShapeDtypeStruct(state_block, F32)],
        scratch_shapes=[pltpu.VMEM(state_block, F32)],
        compiler_params=_cparams(("arbitrary",)),
        name="ssd_fwd",
    )(*args)


def _ssd(p, pdt, lw, h0f, h0b):
    yb, xc, hb = _ssd_bwd(p, pdt, lw, h0b)
    y, hf = _ssd_fwd(xc, pdt, lw, h0f, p, yb)
    return y, hf, hb


def _expand_matrix(offset):
    e = np.zeros((DT_PAD, D_SSD), np.float32)
    for h in range(SSD_HEADS):
        e[offset + h, h * HEAD_DIM:(h + 1) * HEAD_DIM] = 1.0
    return jnp.asarray(e).astype(BF16)


def _layer_weights(l, conv_w, conv_b, a_log, dt_bias, d_skip, ssd_norm_w):
    pad_h = (0, DT_PAD - 2 * SSD_HEADS)
    return {
        "conv_w": jnp.pad(conv_w[l], ((0, 8 - CONV_W), (0, 0))),
        "conv_b": conv_b[l].reshape(1, XBC),
        "dt_bias": jnp.pad(dt_bias[l].reshape(-1), pad_h).reshape(1, DT_PAD),
        "a_log": jnp.pad(a_log[l].reshape(-1), pad_h).reshape(1, DT_PAD),
        "d_skip": jnp.repeat(d_skip[l], HEAD_DIM).reshape(1, D_SSD),
        "ssd_norm_w": ssd_norm_w[l].reshape(1, D_SSD),
        "expand_f": _expand_matrix(0),
        "expand_b": _expand_matrix(SSD_HEADS),
    }


def _mixer(x, nw, sc, sh, g_m, wts, lw, layer, grid_rows, h0f, h0b, proj=None):
    b, l, dm = x.shape
    tok = (lambda a: a.reshape(1, b * l, a.shape[-1])) if sc.shape[0] == 1 else (lambda a: a)
    seq = lambda a: a.reshape(b, l, a.shape[-1])
    p, pdt = proj or _inproj(tok(x), nw, sc, sh, wts["w_main"], wts["w_dt"], layer, P_TOT, INPROJ_TM, INPROJ_TN)
    y_n, hf, hb = _ssd(seq(p), seq(pdt), lw, h0f, h0b)
    d = _pool_diff(seq(p), grid_rows)
    f = _fourier(seq(p))
    x = seq(_merge_out(tok(d), tok(f), tok(y_n), p, tok(x), g_m, wts, layer, MERGE_TM))
    return x, hf, hb


def kernel(x, c, ctx, c_ctx, w_ada, b_ada, norm_mix_w, norm_ffn_w, w_in, conv_w, conv_b, a_log, dt_bias,
           d_skip, ssd_norm_w, w_ssd_out, w_pool, pool_scale, w_fourier, w_out, w_ffn_gate, w_ffn_up,
           w_ffn_down, final_norm_w):
    b, seq, d = x.shape
    rows = seq // GRID_W
    mod = _adaln(jnp.concatenate([c, c_ctx[None, :]], axis=0), w_ada, b_ada)
    h0 = jnp.zeros((b,) + _STATE_TAIL, F32)
    lc = ctx.shape[1]
    w_main = _winprep(w_in)
    w_dt = jnp.pad(w_in[:, :, XBC:SSD_IN], ((0, 0), (0, 0), (0, DT_PAD - 2 * SSD_HEADS))).astype(BF16)
    wts = {"w_main": w_main, "w_dt": w_dt, "pool_scale": pool_scale.reshape(DEPTH, 1, D_MODEL)}
    ff = w_ffn_gate.shape[2]
    wg = wu = wd = None
    for l in range(DEPTH):
        last = l == DEPTH - 1
        lw = _layer_weights(l, conv_w, conv_b, a_log, dt_bias, d_skip, ssd_norm_w)
        sh_m, sc_m, g_m, sh_f, sc_f, g_f = [mod[l, :b, i * d:(i + 1) * d].reshape(b, 1, d) for i in range(6)]
        csh_m, csc_m, cg_m, csh_f, csc_f, cg_f = [mod[l, b, i * d:(i + 1) * d].reshape(1, 1, d) for i in range(6)]

        proj = None
        if l == 0:
            r = FFN_CAST_ROWS
            side = [(w_ffn_gate.reshape(DEPTH * d, ff), r), (w_ffn_up.reshape(DEPTH * d, ff), r),
                    (w_ffn_down.reshape(DEPTH * ff, d), 2 * r),
                    (w_out.reshape(DEPTH * d, d), r), (w_ssd_out.reshape(DEPTH * D_SSD, d), r),
                    (w_fourier.reshape(DEPTH * FOUR_W, d), r // 2), (w_pool.reshape(DEPTH * POOL_W, POOL_OUT), r // 2)]
            p0, pdt0, wg, wu, wd, wo, ws, wf, wp = _inproj(x, norm_mix_w[l], sc_m, sh_m, w_main, w_dt, l, P_TOT,
                                                           INPROJ_TM, INPROJ_TN, side)
            proj = (p0, pdt0)
            wg, wu, wd = wg.reshape(DEPTH, d, ff), wu.reshape(DEPTH, d, ff), wd.reshape(DEPTH, ff, d)
            wts.update(w_out=wo.reshape(DEPTH, d, d), w_ssd_out=ws.reshape(DEPTH, D_SSD, d),
                       w_fourier=wf.reshape(DEPTH, FOUR_W, d), w_pool=wp.reshape(DEPTH, POOL_W, POOL_OUT))

        if last:
            pc, pdtc = _inproj(ctx.reshape(1, b * lc, d), norm_mix_w[l], csc_m, csh_m, w_main, w_dt, l,
                               XBC, INPROJ_TM, INPROJ_TN)
            pdtc = pdtc.reshape(b, lc, DT_PAD)
            _, xcc, hb = _ssd_bwd(pc.reshape(b, lc, XBC), pdtc, lw, h0)
            _, hf = _ssd_fwd(xcc, pdtc, lw, h0)
        else:
            ctx, hf, hb = _mixer(ctx, norm_mix_w[l], csc_m, csh_m, cg_m, wts, lw, l, None, h0, h0)
            ctx = _ffn(ctx.reshape(1, b * lc, d), norm_ffn_w[l], csc_f, csh_f, wg, wu, wd, l,
                       cg_f, final_norm_w, False, FFN_TM, FFN_TF).reshape(b, lc, d)

        x, _, _ = _mixer(x, norm_mix_w[l], sc_m, sh_m, g_m, wts, lw, l, rows, hf, hb, proj)
        x = _ffn(x, norm_ffn_w[l], sc_f, sh_f, wg, wu, wd, l, g_f, final_norm_w, last, FFN_TM, FFN_TF)
    return x
```

```python
import functools
import math

import numpy as np
import jax
import jax.numpy as jnp
from jax import lax
from jax.experimental import pallas as pl
from jax.experimental.pallas import tpu as pltpu

F32 = jnp.float32
BF16 = jnp.bfloat16

D_MODEL = 2048
DEPTH = 2
GRID_W = 64
EPS = 1e-6

POOL_GROUPS = 4
POOL_WINDOWS = (2, 4, 8, 16)
POOL_W = D_MODEL // 2
POOL_GC = POOL_W // POOL_GROUPS
POOL_OUT = D_MODEL // POOL_GROUPS

FOUR_GROUPS = 4
FOUR_W = D_MODEL // 2
FOUR_GC = FOUR_W // FOUR_GROUPS

D_SSD = D_MODEL
HEAD_DIM = 64
SSD_HEADS = D_SSD // HEAD_DIM
SSD_GROUPS = 4
HEADS_PER_GROUP = SSD_HEADS // SSD_GROUPS
GROUP_W = HEADS_PER_GROUP * HEAD_DIM
D_STATE = 128
CONV_W = 5
CHUNK = 128

XBC = D_SSD + 2 * SSD_GROUPS * D_STATE
SSD_IN = XBC + 2 * SSD_HEADS
Z_OFF = SSD_IN
POOL_OFF = Z_OFF + D_SSD
FOUR_OFF = POOL_OFF + POOL_W
GATE_OFF = FOUR_OFF + FOUR_W
N_IN = GATE_OFF + 3 * D_MODEL

P_XBC = 0
P_POOL = XBC
P_Z = P_POOL + POOL_W
P_GATE = P_Z + D_SSD
P_FOUR = P_GATE + 3 * D_MODEL
P_TOT = P_FOUR + FOUR_W
DT_PAD = 128

V7X_VMEM_LIMIT = 56 * 1024 * 1024
HALO = 16

INPROJ_TM, INPROJ_TN = 1024, 1024
MERGE_TM = 256
FFN_TM, FFN_TF = 512, 512
ADALN_TN = 1024
FFN_CAST_ROWS = 64


def _cparams(sem):
    return pltpu.CompilerParams(dimension_semantics=sem, vmem_limit_bytes=V7X_VMEM_LIMIT)


def _resident(shape, index_map):
    return pl.BlockSpec(shape, index_map, pipeline_mode=pl.Buffered(1))


def _split2(v):
    hi = v.astype(BF16)
    lo = (v - hi.astype(F32)).astype(BF16)
    return hi, lo


def _dot(a, b):
    return jnp.dot(a, b, preferred_element_type=F32)


def _sigmoid(v):
    return 0.5 * jnp.tanh(0.5 * v) + 0.5


def _silu(v):
    return v * _sigmoid(v)


def _np_split2(m):
    m = jnp.asarray(np.asarray(m, np.float32))
    hi = m.astype(BF16)
    lo = (m - hi.astype(F32)).astype(BF16)
    return hi, lo


WIN_CHUNK = 1024

_WIN_SEGMENTS = ((P_XBC, 0, XBC), (P_POOL, POOL_OFF, POOL_W), (P_Z, Z_OFF, D_SSD),
                 (P_GATE, GATE_OFF, 3 * D_MODEL), (P_FOUR, FOUR_OFF, FOUR_W))


def _win_src(j, chunk=None):
    chunk = chunk or WIN_CHUNK
    src = jnp.int32(0)
    for dst0, src0, width in _WIN_SEGMENTS:
        first = dst0 // chunk
        inside = (j >= first) & (j < first + width // chunk)
        src = jnp.where(inside, src0 + (j - first) * chunk, src)
    return pl.multiple_of(src, 2 * SSD_HEADS)


def _winprep_kernel(wt_ref, o_ref):
    o_ref[0] = wt_ref[0].T.astype(BF16)


def _winprep(w_in, layer=None):
    depth, d, n = w_in.shape
    wt = jnp.swapaxes(w_in, 1, 2)
    layers = depth if layer is None else 1
    first = 0 if layer is None else layer
    return pl.pallas_call(
        _winprep_kernel,
        grid=(layers, P_TOT // WIN_CHUNK),
        in_specs=[pl.BlockSpec((pl.Element(1), pl.Element(WIN_CHUNK), pl.Element(d)),
                               lambda l, j: (first + l, _win_src(j), 0))],
        out_specs=pl.BlockSpec((1, d, WIN_CHUNK), lambda l, j: (l, 0, j)),
        out_shape=jax.ShapeDtypeStruct((layers, d, P_TOT), BF16),
        compiler_params=_cparams(("parallel", "parallel")),
        name="winprep",
    )(wt)


ADA_ROWS = 8


def _adaln_kernel(ct_ref, w_ref, b_ref, o_ref, *, n_rows):
    s = _silu(ct_ref[...])
    w = w_ref[0]
    tn = w.shape[1]
    rows = [jnp.sum(w * s[:, r:r + 1], axis=0, keepdims=True) for r in range(n_rows)]
    rows.append(jnp.zeros((ADA_ROWS - n_rows, tn), F32))
    o_ref[0] = jnp.concatenate(rows, axis=0) + b_ref[0]


def _adaln(cc, w_ada, b_ada):
    depth, d, n = w_ada.shape
    tn = ADALN_TN
    n_rows = cc.shape[0]
    ct = jnp.pad(cc, ((0, ADA_ROWS - n_rows), (0, 0))).T
    return pl.pallas_call(
        functools.partial(_adaln_kernel, n_rows=n_rows),
        grid=(depth, n // tn),
        in_specs=[pl.BlockSpec((d, ADA_ROWS), lambda l, j: (0, 0)),
                  pl.BlockSpec((1, d, tn), lambda l, j: (l, 0, j)),
                  pl.BlockSpec((1, 1, tn), lambda l, j: (l, 0, j))],
        out_specs=pl.BlockSpec((1, ADA_ROWS, tn), lambda l, j: (l, 0, j)),
        out_shape=jax.ShapeDtypeStruct((depth, ADA_ROWS, n), F32),
        compiler_params=_cparams(("parallel", "parallel")),
        name="adaln",
    )(ct, w_ada, b_ada.reshape(depth, 1, n))


def _norm_mod(x, nw, sc, sh):
    ms = jnp.mean(x * x, axis=-1, keepdims=True)
    return (x * lax.rsqrt(ms + EPS)) * nw * (1.0 + sc) + sh


def _inproj_kernel(x_ref, nw_ref, sc_ref, sh_ref, w_ref, wdt_ref, *rest, n_side):
    side_in = rest[:n_side]
    o_ref, dt_ref = rest[n_side:n_side + 2]
    side_out = rest[n_side + 2:2 * n_side + 2]
    h_ref = rest[-1]

    @pl.when(pl.program_id(2) == 0)
    def _():
        h = _norm_mod(x_ref[0], nw_ref[...], sc_ref[0], sh_ref[0]).astype(BF16)
        h_ref[...] = h
        dt_ref[0] = _dot(h, wdt_ref[0])

    o_ref[0] = _dot(h_ref[...], w_ref[0]).astype(o_ref.dtype)

    for wi_ref, wo_ref in zip(side_in, side_out):
        wo_ref[...] = wi_ref[...].astype(BF16)


def _inproj(x, nw, sc, sh, w, wdt, layer, n_out, tm, tn, side=(), w_layer=None):
    b, l, d = x.shape
    tm = min(tm, l)
    n_i, n_j = l // tm, n_out // tn
    w_layer = layer if w_layer is None else w_layer
    side_specs, side_shapes = [], []
    for arr, rows in side:
        nblk = arr.shape[0] // rows
        assert arr.shape[0] % rows == 0 and nblk <= b * n_i * n_j
        imap = lambda bi, i, j, nblk=nblk: (jnp.minimum((bi * n_i + i) * n_j + j, nblk - 1), 0)
        side_specs.append(pl.BlockSpec((rows, arr.shape[1]), imap))
        side_shapes.append(jax.ShapeDtypeStruct(arr.shape, BF16))
    sem = ("arbitrary",) * 3 if side else ("parallel", "parallel", "arbitrary")
    return pl.pallas_call(
        functools.partial(_inproj_kernel, n_side=len(side)),
        grid=(b, n_i, n_j),
        in_specs=[pl.BlockSpec((1, tm, d), lambda bi, i, j: (bi, i, 0)),
                  pl.BlockSpec((1, d), lambda bi, i, j: (0, 0)),
                  pl.BlockSpec((1, 1, d), lambda bi, i, j: (bi, 0, 0)),
                  pl.BlockSpec((1, 1, d), lambda bi, i, j: (bi, 0, 0)),
                  pl.BlockSpec((1, d, tn), lambda bi, i, j: (w_layer, 0, j)),
                  _resident((1, d, DT_PAD), lambda bi, i, j: (layer, 0, 0))] + side_specs,
        out_specs=[pl.BlockSpec((1, tm, tn), lambda bi, i, j: (bi, i, j)),
                   pl.BlockSpec((1, tm, DT_PAD), lambda bi, i, j: (bi, i, 0))] + side_specs,
        out_shape=[jax.ShapeDtypeStruct((b, l, n_out), BF16),
                   jax.ShapeDtypeStruct((b, l, DT_PAD), F32)] + side_shapes,
        scratch_shapes=[pltpu.VMEM((tm, d), BF16)],
        compiler_params=_cparams(sem),
        name="inproj",
    )(x, nw.reshape(1, d), sc, sh, w, wdt, *[arr for arr, _ in side])


def _merge_kernel(d_ref, f_ref, yn_ref, gp_ref, gf_ref, gs_ref, x_ref, gm_ref,
                  wp_ref, ps_ref, wf_ref, ws_ref, wo_ref, o_ref):
    d = d_ref[0]
    y_pool = jnp.concatenate(
        [_dot(d[:, g * POOL_GC:(g + 1) * POOL_GC], wp_ref[0, g * POOL_GC:(g + 1) * POOL_GC, :])
         for g in range(POOL_GROUPS)], axis=1)
    m = _sigmoid(gp_ref[0].astype(F32)) * (y_pool * ps_ref[0])
    m = m + _sigmoid(gf_ref[0].astype(F32)) * _dot(f_ref[0], wf_ref[0])
    m = m + _sigmoid(gs_ref[0].astype(F32)) * _dot(yn_ref[0], ws_ref[0])
    o_ref[0] = x_ref[0] + gm_ref[0] * _dot(m.astype(BF16), wo_ref[0])


def _merge_out(d, f, yn, p, x, g_m, wts, layer, tm):
    b, l, dm = x.shape
    tm = min(tm, l)
    gblk = P_GATE // dm
    row = lambda bi, i: (bi, i, 0)
    lyr = lambda bi, i: (layer, 0, 0)
    return pl.pallas_call(
        _merge_kernel,
        grid=(b, l // tm),
        in_specs=[pl.BlockSpec((1, tm, POOL_W), row),
                  pl.BlockSpec((1, tm, FOUR_W), row),
                  pl.BlockSpec((1, tm, D_SSD), row),
                  pl.BlockSpec((1, tm, dm), lambda bi, i: (bi, i, gblk)),
                  pl.BlockSpec((1, tm, dm), lambda bi, i: (bi, i, gblk + 1)),
                  pl.BlockSpec((1, tm, dm), lambda bi, i: (bi, i, gblk + 2)),
                  pl.BlockSpec((1, tm, dm), row),
                  pl.BlockSpec((1, 1, dm), lambda bi, i: (bi, 0, 0)),
                  _resident((1, POOL_W, POOL_OUT), lyr),
                  _resident((1, 1, dm), lyr),
                  _resident((1, FOUR_W, dm), lyr),
                  _resident((1, D_SSD, dm), lyr),
                  _resident((1, dm, dm), lyr)],
        out_specs=pl.BlockSpec((1, tm, dm), row),
        out_shape=jax.ShapeDtypeStruct((b, l, dm), F32),
        compiler_params=_cparams(("parallel", "parallel")),
        name="merge_out",
    )(d, f, yn, p, p, p, x, g_m, wts["w_pool"], wts["pool_scale"], wts["w_fourier"], wts["w_ssd_out"],
      wts["w_out"])


WIN_SIDE_CHUNK = 128


def _ffn_kernel(x_ref, nw_ref, sc_ref, sh_ref, wg_ref, wu_ref, wd_ref, gate_ref, fnw_ref, *rest, final_norm, side):
    if side:
        wi_ref, o_ref, wo_ref, h_ref, acc_ref = rest
        wo_ref[0] = wi_ref[0].T.astype(BF16)
    else:
        o_ref, h_ref, acc_ref = rest
    j = pl.program_id(2)

    @pl.when(j == 0)
    def _():
        h_ref[...] = _norm_mod(x_ref[0], nw_ref[...], sc_ref[0], sh_ref[0]).astype(BF16)
        acc_ref[...] = jnp.zeros_like(acc_ref)

    h = h_ref[...]
    a = _silu(_dot(h, wg_ref[0])) * _dot(h, wu_ref[0])
    acc_ref[...] += _dot(a.astype(BF16), wd_ref[0])

    @pl.when(j == pl.num_programs(2) - 1)
    def _():
        o = x_ref[0] + gate_ref[0] * acc_ref[...]
        if final_norm:
            ms = jnp.mean(o * o, axis=-1, keepdims=True)
            o = (o * lax.rsqrt(ms + EPS)) * fnw_ref[...]
        o_ref[0] = o


def _ffn(x, nw, sc, sh, wg, wu, wd, layer, gate, fnw, final_norm, tm, tf, side_w=None):
    b, l, d = x.shape
    ff = wg.shape[2]
    tm = min(tm, l)
    n_i, n_j = l // tm, ff // tf
    vec = lambda bi, i, j: (bi, 0, 0)
    side_in, side_out, side_shape, side_args = [], [], [], []
    if side_w is not None:
        nchunk = P_TOT // WIN_SIDE_CHUNK
        assert nchunk <= b * n_i * n_j
        step = lambda bi, i, j: jnp.minimum((bi * n_i + i) * n_j + j, nchunk - 1)
        side_in = [pl.BlockSpec((pl.Element(1), pl.Element(WIN_SIDE_CHUNK), pl.Element(d)),
                                lambda bi, i, j: (side_w[1], _win_src(step(bi, i, j), WIN_SIDE_CHUNK), 0))]
        side_out = [pl.BlockSpec((1, d, WIN_SIDE_CHUNK), lambda bi, i, j: (0, 0, step(bi, i, j)))]
        side_shape = [jax.ShapeDtypeStruct((1, d, P_TOT), BF16)]
        side_args = [jnp.swapaxes(side_w[0], 1, 2)]
    sem = ("arbitrary",) * 3 if side_w is not None else ("parallel", "parallel", "arbitrary")
    return pl.pallas_call(
        functools.partial(_ffn_kernel, final_norm=final_norm, side=side_w is not None),
        grid=(b, n_i, n_j),
        in_specs=[pl.BlockSpec((1, tm, d), lambda bi, i, j: (bi, i, 0)),
                  pl.BlockSpec((1, d), lambda bi, i, j: (0, 0)),
                  pl.BlockSpec((1, 1, d), vec),
                  pl.BlockSpec((1, 1, d), vec),
                  pl.BlockSpec((1, d, tf), lambda bi, i, j: (layer, 0, j)),
                  pl.BlockSpec((1, d, tf), lambda bi, i, j: (layer, 0, j)),
                  pl.BlockSpec((1, tf, d), lambda bi, i, j: (layer, j, 0)),
                  pl.BlockSpec((1, 1, d), vec),
                  pl.BlockSpec((1, d), lambda bi, i, j: (0, 0))] + side_in,
        out_specs=[pl.BlockSpec((1, tm, d), lambda bi, i, j: (bi, i, 0))] + side_out,
        out_shape=[jax.ShapeDtypeStruct((b, l, d), F32)] + side_shape,
        scratch_shapes=[pltpu.VMEM((tm, d), BF16), pltpu.VMEM((tm, d), F32)],
        compiler_params=_cparams(sem),
        name="ffn",
    )(x, nw.reshape(1, d), sc, sh, wg, wu, wd, gate, fnw.reshape(1, d), *side_args)


def _box_matrix(n, w):
    idx = np.arange(n)
    lo = np.clip(idx - w // 2, 0, n)
    hi = np.clip(idx + (w - w // 2), 0, n)
    m = ((idx[None, :] >= lo[:, None]) & (idx[None, :] < hi[:, None])).astype(np.float64)
    return m / (hi - lo)[:, None]


POOL_TB = 256
POOL_PAD = 16


def _pool_kernel(u_ref, mh_ref, ml_ref, o_ref, *scratch, grid_rows):
    g = pl.program_id(1)
    l = u_ref.shape[1]
    mh = mh_ref[0]
    ml = ml_ref[0]
    if grid_rows is None:
        v = u_ref[0]
        o_ref[0] = (_dot(mh, v) + _dot(ml, v) - v.astype(F32)).astype(o_ref.dtype)
        return

    cp_ref, = scratch
    pad = POOL_PAD * GRID_W
    cp_ref[0:pad, :] = jnp.zeros((pad, POOL_GC), F32)
    cp_ref[pad + l:pad + l + pad, :] = jnp.zeros((pad, POOL_GC), F32)
    for i in range(l // POOL_TB):
        v = u_ref[0, i * POOL_TB:(i + 1) * POOL_TB, :]
        cp_ref[pad + i * POOL_TB:pad + (i + 1) * POOL_TB, :] = _dot(mh, v) + _dot(ml, v)

    for gi, w in enumerate(POOL_WINDOWS):
        lo_off, hi_off = -(w // 2), w - w // 2

        @pl.when(g == gi)
        def _(lo_off=lo_off, hi_off=hi_off):
            def slab(row):
                return cp_ref[pl.ds(pl.multiple_of(pad + row * GRID_W, GRID_W), GRID_W), :]

            def body(r, s):
                s = s + slab(r + hi_off - 1) - slab(r + lo_off - 1)
                cnt = jnp.minimum(r + hi_off, grid_rows) - jnp.maximum(r + lo_off, 0)
                cntv = jnp.full((GRID_W, POOL_GC), cnt, jnp.int32).astype(F32)
                t0 = pl.multiple_of(r * GRID_W, GRID_W)
                v = u_ref[0, pl.ds(t0, GRID_W), :].astype(F32)
                o_ref[0, pl.ds(t0, GRID_W), :] = (s / cntv - v).astype(o_ref.dtype)
                return s

            s0 = jnp.zeros((GRID_W, POOL_GC), F32)
            for o in range(lo_off - 1, hi_off - 1):
                s0 = s0 + cp_ref[pad + o * GRID_W:pad + (o + 1) * GRID_W, :]
            lax.fori_loop(0, grid_rows, body, s0)


def _pool_diff(p, grid_rows):
    b, l, _ = p.shape
    mats = []
    for w in POOL_WINDOWS:
        if grid_rows is None:
            mats.append(_box_matrix(l, w))
        else:
            mats.append(np.kron(np.eye(POOL_TB // GRID_W), _box_matrix(GRID_W, w)))
    mh, ml = _np_split2(np.stack(mats))
    tb = mats[0].shape[0]
    scratch = []
    if grid_rows is not None:
        scratch = [pltpu.VMEM((l + 2 * POOL_PAD * GRID_W, POOL_GC), F32)]
    cblk = P_POOL // POOL_GC
    return pl.pallas_call(
        functools.partial(_pool_kernel, grid_rows=grid_rows),
        grid=(b, POOL_GROUPS),
        in_specs=[pl.BlockSpec((1, l, POOL_GC), lambda bi, g: (bi, 0, cblk + g)),
                  pl.BlockSpec((1, tb, tb), lambda bi, g: (g, 0, 0)),
                  pl.BlockSpec((1, tb, tb), lambda bi, g: (g, 0, 0))],
        out_specs=pl.BlockSpec((1, l, POOL_GC), lambda bi, g: (bi, 0, g)),
        out_shape=jax.ShapeDtypeStruct((b, l, POOL_W), BF16),
        scratch_shapes=scratch,
        compiler_params=_cparams(("parallel", "parallel")),
        name="pool_diff",
    )(p, mh, ml)


FLIP_ROWS = 512


def _flip_matrix(t):
    m = np.zeros((t, t), np.float32)
    m[np.arange(1, t), t - np.arange(1, t)] = 1.0
    return jnp.asarray(m).astype(BF16)


def _chdft_fold_kernel(u_ref, ua_ref, ub_ref, w_ref, flip_ref, o_ref, nyq_ref):
    i = pl.program_id(2)
    c = FOUR_GC
    tm = u_ref.shape[1]
    w = w_ref[...]
    urev = _dot(flip_ref[...], ua_ref[0])
    row = lax.broadcasted_iota(jnp.int32, (tm, c), 0)
    urev = jnp.where(row == 0, jnp.broadcast_to(ub_ref[0, 0:1, :].astype(F32), (tm, c)), urev).astype(BF16)
    z = _dot(u_ref[0], w)
    zrev = _dot(urev, w)
    first = (row + jnp.minimum(i, 1)) == 0
    o_ref[0, 0, :, :c] = (z[:, :c] + jnp.where(first, 0.0, zrev[:, :c])).astype(o_ref.dtype)
    o_ref[0, 0, :, c:] = (z[:, c:] - zrev[:, c:]).astype(o_ref.dtype)

    @pl.when(i == 0)
    def _():
        nyq_ref[0, 0] = zrev[0:8, :c]


def _chdft_fold(p):
    b, l, _ = p.shape
    half = l // 2
    tm = min(FLIP_ROWS, half)
    nb = l // tm
    k = np.arange(FOUR_GC)
    ang = -2.0 * np.pi * ((k[:, None] * k[None, :]) % FOUR_GC) / FOUR_GC
    w = jnp.asarray(np.concatenate([np.cos(ang), np.sin(ang)], axis=1).astype(np.float32)).astype(BF16)
    cblk = P_FOUR // FOUR_GC
    return pl.pallas_call(
        _chdft_fold_kernel,
        grid=(b, FOUR_GROUPS, half // tm),
        in_specs=[pl.BlockSpec((1, tm, FOUR_GC), lambda bi, g, i: (bi, i, cblk + g)),
                  pl.BlockSpec((1, tm, FOUR_GC), lambda bi, g, i: (bi, nb - 1 - i, cblk + g)),
                  pl.BlockSpec((1, tm, FOUR_GC), lambda bi, g, i: (bi, jnp.where(i == 0, nb // 2, nb - i), cblk + g)),
                  _resident((FOUR_GC, 2 * FOUR_GC), lambda bi, g, i: (0, 0)),
                  _resident((tm, tm), lambda bi, g, i: (0, 0))],
        out_specs=[pl.BlockSpec((1, 1, tm, 2 * FOUR_GC), lambda bi, g, i: (bi, g, i, 0)),
                   pl.BlockSpec((1, 1, 8, FOUR_GC), lambda bi, g, i: (bi, g, 0, 0))],
        out_shape=[jax.ShapeDtypeStruct((b, FOUR_GROUPS, half, 2 * FOUR_GC), BF16),
                   jax.ShapeDtypeStruct((b, FOUR_GROUPS, 8, FOUR_GC), F32)],
        compiler_params=_cparams(("parallel", "parallel", "arbitrary")),
        name="chdft",
    )(p, p, p, w, _flip_matrix(tm))


def _seq_dft_tables(l):
    half = l // 2
    r = int(round(math.sqrt(l)))
    assert r * r == l and r % 2 == 0
    col = np.arange(half)
    hi = 2.0 * np.pi * ((np.arange(r // 2)[:, None] * r * col[None, :]) % l) / l
    lo = 2.0 * np.pi * ((np.arange(r)[:, None] * col[None, :]) % l) / l
    ch, sh = [jnp.asarray(f(hi).astype(np.float32))[:, None, :] for f in (np.cos, np.sin)]
    cl, sl = [jnp.asarray(f(lo).astype(np.float32))[None, :, :] for f in (np.cos, np.sin)]
    cos = (ch * cl - sh * sl).reshape(half, half)
    sin = (sh * cl + ch * sl).reshape(half, half)
    return cos.astype(BF16), sin.astype(BF16)


def _seqdft_kernel(c_ref, s_ref, alt_ref, flip_ref, e_ref, nyq_ref, o_ref, *, scale):
    c = FOUR_GC
    half = e_ref.shape[2]
    ft = flip_ref.shape[0]
    nblk = half // ft
    e = e_ref[0, 0]
    p = _dot(c_ref[...], e[:, :c])
    q = _dot(s_ref[...], e[:, c:])
    zn = nyq_ref[0, 0, 0:1, :]
    k = lax.broadcasted_iota(jnp.int32, (half, c), 0)
    n = jnp.where((k & 1) == 0, zn, -zn)
    o_ref[0, 0:half, :] = ((p + q + n) * scale).astype(o_ref.dtype)
    mir = (p - q + n) * scale
    mir_b = mir.astype(BF16)
    y_half = (_dot(alt_ref[...], e[:, :c])[0:1] + zn) * scale
    row = lax.broadcasted_iota(jnp.int32, (ft, c), 0)
    for jb in range(nblk):
        src = (nblk - 1 - jb) * ft
        hi = _dot(flip_ref[...], mir_b[src:src + ft])
        first = y_half if jb == 0 else mir[src + ft:src + ft + 1]
        hi = jnp.where(row == 0, jnp.broadcast_to(first, (ft, c)), hi)
        o_ref[0, half + jb * ft:half + (jb + 1) * ft, :] = hi.astype(o_ref.dtype)


def _fourier(p):
    b, l, _ = p.shape
    half = l // 2
    ft = min(FLIP_ROWS, half)
    e, nyq = _chdft_fold(p)
    cos, sin = _seq_dft_tables(l)
    alt = jnp.asarray(np.broadcast_to(1.0 - 2.0 * (np.arange(half) % 2), (16, half)).astype(np.float32)).astype(BF16)
    scale = 1.0 / math.sqrt(l * FOUR_GC)
    const2 = lambda bi, g: (0, 0)
    return pl.pallas_call(
        functools.partial(_seqdft_kernel, scale=scale),
        grid=(b, FOUR_GROUPS),
        in_specs=[_resident((half, half), const2),
                  _resident((half, half), const2),
                  _resident((16, half), const2),
                  _resident((ft, ft), const2),
                  pl.BlockSpec((1, 1, half, 2 * FOUR_GC), lambda bi, g: (bi, g, 0, 0)),
                  pl.BlockSpec((1, 1, 8, FOUR_GC), lambda bi, g: (bi, g, 0, 0))],
        out_specs=pl.BlockSpec((1, l, FOUR_GC), lambda bi, g: (bi, 0, g)),
        out_shape=jax.ShapeDtypeStruct((b, l, FOUR_W), BF16),
        compiler_params=_cparams(("parallel", "parallel")),
        name="seqdft",
    )(cos, sin, alt, _flip_matrix(ft), e, nyq)


CONV_CB = 512
_SHIFT_TAPS = tuple(k for k in range(CONV_W) if k != CONV_W // 2)


def _shift_matrix():
    m = np.zeros((len(_SHIFT_TAPS) * CHUNK, CHUNK + 2 * HALO), np.float32)
    t = np.arange(CHUNK)
    for q, k in enumerate(_SHIFT_TAPS):
        m[q * CHUNK + t, HALO - CONV_W // 2 + k + t] = 1.0
    return jnp.asarray(m).astype(BF16)


def _ssd_kernel(*refs, fwd, nc, epilogue):
    if fwd:
        xc_ref, dt_ref, dtb_ref, alog_ref, e_ref, h0_ref = refs[:6]
        rest = refs[6:]
        if epilogue:
            z_ref, yb_ref, dsk_ref, nw_ref, y_ref, hout_ref, st_ref = rest
        else:
            y_ref, hout_ref, st_ref = rest
    else:
        (xm_ref, xp_ref, xn_ref, dt_ref, cw_ref, cb_ref, shift_ref, dtb_ref, alog_ref, e_ref, h0_ref,
         y_ref, xc_ref, hout_ref, st_ref, ext_ref) = refs
    c = pl.program_id(0)
    cc = c if fwd else nc - 1 - c
    T = CHUNK
    nb = dt_ref.shape[0]

    @pl.when(c == 0)
    def _():
        st_ref[...] = h0_ref[...]

    if not fwd:
        @pl.when(cc > 0)
        def _():
            ext_ref[:, 0:HALO, :] = xp_ref[...]

        @pl.when(cc == 0)
        def _():
            ext_ref[:, 0:HALO, :] = jnp.zeros((nb, HALO, XBC), ext_ref.dtype)

        ext_ref[:, HALO:HALO + T, :] = xm_ref[...]

        @pl.when(cc < nc - 1)
        def _():
            ext_ref[:, HALO + T:HALO + T + HALO, :] = xn_ref[...]

        @pl.when(cc == nc - 1)
        def _():
            ext_ref[:, HALO + T:HALO + T + HALO, :] = jnp.zeros((nb, HALO, XBC), ext_ref.dtype)

        shift = shift_ref[...]
        for j in range(XBC // CONV_CB):
            cs = slice(j * CONV_CB, (j + 1) * CONV_CB)
            for bi in range(nb):
                sh = _dot(shift, ext_ref[bi, :, cs])
                acc = cb_ref[:, cs] + xm_ref[bi, :, cs].astype(F32) * cw_ref[CONV_W // 2:CONV_W // 2 + 1, cs]
                for q, k in enumerate(_SHIFT_TAPS):
                    acc = acc + sh[q * T:(q + 1) * T] * cw_ref[k:k + 1, cs]
                xc_ref[bi, :, cs] = _silu(acc).astype(xc_ref.dtype)

    row = lax.broadcasted_iota(jnp.int32, (T, T), 0)
    col = lax.broadcasted_iota(jnp.int32, (T, T), 1)
    causal = (col <= row) if fwd else (col >= row)
    tri = jnp.where(causal, 1.0, 0.0).astype(BF16)
    lane = lax.broadcasted_iota(jnp.int32, (T, 2 * HEAD_DIM), 1)
    hbase = 0 if fwd else SSD_HEADS
    e = e_ref[...]
    a = -jnp.exp(alog_ref[...])

    def head_factors(bi):
        xdt = dt_ref[bi] + dtb_ref[...]
        dt = jnp.maximum(xdt, 0.0) + jnp.log1p(jnp.exp(-jnp.abs(xdt)))
        da = dt * a
        d1 = da.astype(BF16)
        r1 = da - d1.astype(F32)
        d2 = r1.astype(BF16)
        d3 = (r1 - d2.astype(F32)).astype(BF16)
        acs = _dot(tri, d1) + _dot(tri, d2) + _dot(tri, d3)
        tot = acs[T - 1:T, :] if fwd else acs[0:1, :]
        eh, el = _split2(jnp.broadcast_to(jnp.exp(tot), (16, T)))
        ex = _dot(jnp.concatenate([jnp.exp(acs).astype(BF16), (dt * jnp.exp(tot - acs)).astype(BF16), eh, el],
                                  axis=0), e)
        return dict(acs=acs, arow_t=(acs - jnp.log(dt)).T, eacs_x=ex[0:T], w2=ex[T:2 * T],
                    etot_x=ex[2 * T:2 * T + 1] + ex[2 * T + 16:2 * T + 17])

    hf = [head_factors(bi) for bi in range(nb)]

    for g in range(SSD_GROUPS):
        gs = slice(g * GROUP_W, (g + 1) * GROUP_W)
        bs = slice(D_SSD + g * D_STATE, D_SSD + (g + 1) * D_STATE)
        cs_ = slice(D_SSD + (SSD_GROUPS + g) * D_STATE, D_SSD + (SSD_GROUPS + g + 1) * D_STATE)
        for bi in range(nb):
            f = hf[bi]
            bb = xc_ref[bi, :, bs]
            cbf = xc_ref[bi, :, cs_]
            cb = lax.dot_general(cbf, bb, (((1,), (1,)), ((), ())), preferred_element_type=F32)
            st = st_ref[bi, g]
            yoff = _dot(cbf, st.astype(BF16)) * f["eacs_x"][:, gs]
            xb = xc_ref[bi, :, gs]
            ys = []
            for jp in range(HEADS_PER_GROUP // 2):
                ms = []
                for jj in range(2):
                    hc = hbase + g * HEADS_PER_GROUP + 2 * jp + jj
                    diff = f["acs"][:, hc:hc + 1] - f["arow_t"][hc:hc + 1, :]
                    ms.append((cb * jnp.exp(jnp.where(causal, diff, -1e30))).astype(BF16))
                r = _dot(jnp.concatenate(ms, axis=0), xb[:, jp * 2 * HEAD_DIM:(jp + 1) * 2 * HEAD_DIM])
                ys.append(jnp.where(lane < HEAD_DIM, r[:T], r[T:]))
            yg = jnp.concatenate(ys, axis=1) + yoff

            xg = xb.astype(F32)
            xds = (xg * f["w2"][:, gs]).astype(BF16)
            st_ref[bi, g] = st * f["etot_x"][:, gs] + _dot(bb.astype(F32).T.astype(BF16), xds)

            if epilogue:
                yt = yg + yb_ref[bi, :, gs] + dsk_ref[:, gs] * xg
                v = yt * _silu(z_ref[bi, :, gs].astype(F32))
                ms_ = jnp.mean(v * v, axis=-1, keepdims=True)
                y_ref[bi, :, gs] = ((v * lax.rsqrt(ms_ + EPS)) * nw_ref[:, gs]).astype(y_ref.dtype)
            else:
                y_ref[bi, :, gs] = yg

    @pl.when(c == nc - 1)
    def _():
        hout_ref[...] = st_ref[...]


_STATE_TAIL = (SSD_GROUPS, D_STATE, GROUP_W)


def _ssd_bwd(p, pdt, lw, h0):
    b, l, _ = p.shape
    nc = l // CHUNK
    nh = l // HALO
    per = CHUNK // HALO
    cidx = lambda c: nc - 1 - c
    chunk = lambda c: (0, cidx(c), 0)
    const2 = lambda c: (0, 0)
    state = lambda c: (0, 0, 0, 0)
    state_block = (b,) + _STATE_TAIL
    return pl.pallas_call(
        functools.partial(_ssd_kernel, fwd=False, nc=nc, epilogue=False),
        grid=(nc,),
        in_specs=[pl.BlockSpec((b, CHUNK, XBC), chunk),
                  pl.BlockSpec((b, HALO, XBC), lambda c: (0, jnp.maximum(cidx(c) * per - 1, 0), 0)),
                  pl.BlockSpec((b, HALO, XBC), lambda c: (0, jnp.minimum((cidx(c) + 1) * per, nh - 1), 0)),
                  pl.BlockSpec((b, CHUNK, DT_PAD), chunk),
                  _resident((8, XBC), const2),
                  _resident((1, XBC), const2),
                  _resident((len(_SHIFT_TAPS) * CHUNK, CHUNK + 2 * HALO), const2),
                  _resident((1, DT_PAD), const2),
                  _resident((1, DT_PAD), const2),
                  _resident((DT_PAD, D_SSD), const2),
                  _resident(state_block, state)],
        out_specs=[pl.BlockSpec((b, CHUNK, D_SSD), chunk),
                   pl.BlockSpec((b, CHUNK, XBC), chunk),
                   pl.BlockSpec(state_block, state)],
        out_shape=[jax.ShapeDtypeStruct((b, l, D_SSD), F32),
                   jax.ShapeDtypeStruct((b, l, XBC), BF16),
                   jax.ShapeDtypeStruct(state_block, F32)],
        scratch_shapes=[pltpu.VMEM(state_block, F32),
                        pltpu.VMEM((b, CHUNK + 2 * HALO, XBC), BF16)],
        compiler_params=_cparams(("arbitrary",)),
        name="ssd_bwd",
    )(p, p, p, pdt, lw["conv_w"], lw["conv_b"], _shift_matrix(), lw["dt_bias"], lw["a_log"], lw["expand_b"], h0)


def _ssd_fwd(xc, pdt, lw, h0, p=None, yb=None):
    b, l, _ = xc.shape
    nc = l // CHUNK
    chunk = lambda c: (0, c, 0)
    const2 = lambda c: (0, 0)
    state = lambda c: (0, 0, 0, 0)
    state_block = (b,) + _STATE_TAIL
    epilogue = p is not None
    in_specs = [pl.BlockSpec((b, CHUNK, XBC), chunk),
                pl.BlockSpec((b, CHUNK, DT_PAD), chunk),
                _resident((1, DT_PAD), const2),
                _resident((1, DT_PAD), const2),
                _resident((DT_PAD, D_SSD), const2),
                _resident(state_block, state)]
    args = [xc, pdt, lw["dt_bias"], lw["a_log"], lw["expand_f"], h0]
    if epilogue:
        z_blk = P_Z // D_SSD
        in_specs += [pl.BlockSpec((b, CHUNK, D_SSD), lambda c: (0, c, z_blk)),
                     pl.BlockSpec((b, CHUNK, D_SSD), chunk),
                     _resident((1, D_SSD), const2),
                     _resident((1, D_SSD), const2)]
        args += [p, yb, lw["d_skip"], lw["ssd_norm_w"]]
    return pl.pallas_call(
        functools.partial(_ssd_kernel, fwd=True, nc=nc, epilogue=epilogue),
        grid=(nc,),
        in_specs=in_specs,
        out_specs=[pl.BlockSpec((b, CHUNK, D_SSD), chunk),
                   pl.BlockSpec(state_block, state)],
        out_shape=[jax.ShapeDtypeStruct((b, l, D_SSD), BF16 if epilogue else F32),
                   jax.ShapeDtypeStruct(state_block, F32)],
        scratch_shapes=[pltpu.VMEM(state_block, F32)],
        compiler_params=_cparams(("arbitrary",)),
        name="ssd_fwd",
    )(*args)


def _ssd(p, pdt, lw, h0f, h0b):
    yb, xc, hb = _ssd_bwd(p, pdt, lw, h0b)
    y, hf = _ssd_fwd(xc, pdt, lw, h0f, p, yb)
    return y, hf, hb


def _expand_matrix(offset):
    e = np.zeros((DT_PAD, D_SSD), np.float32)
    for h in range(SSD_HEADS):
        e[offset + h, h * HEAD_DIM:(h + 1) * HEAD_DIM] = 1.0
    return jnp.asarray(e).astype(BF16)


def _layer_weights(l, conv_w, conv_b, a_log, dt_bias, d_skip, ssd_norm_w):
    pad_h = (0, DT_PAD - 2 * SSD_HEADS)
    return {
        "conv_w": jnp.pad(conv_w[l], ((0, 8 - CONV_W), (0, 0))),
        "conv_b": conv_b[l].reshape(1, XBC),
        "dt_bias": jnp.pad(dt_bias[l].reshape(-1), pad_h).reshape(1, DT_PAD),
        "a_log": jnp.pad(a_log[l].reshape(-1), pad_h).reshape(1, DT_PAD),
        "d_skip": jnp.repeat(d_skip[l], HEAD_DIM).reshape(1, D_SSD),
        "ssd_norm_w": ssd_norm_w[l].reshape(1, D_SSD),
        "expand_f": _expand_matrix(0),
        "expand_b": _expand_matrix(SSD_HEADS),
    }


def _mixer(x, nw, sc, sh, g_m, wts, lw, layer, grid_rows, h0f, h0b, proj=None):
    b, l, dm = x.shape
    tok = (lambda a: a.reshape(1, b * l, a.shape[-1])) if sc.shape[0] == 1 else (lambda a: a)
    seq = lambda a: a.reshape(b, l, a.shape[-1])
    p, pdt = proj or _inproj(tok(x), nw, sc, sh, wts["w_main"], wts["w_dt"], layer, P_TOT, INPROJ_TM, INPROJ_TN,
                             w_layer=0)
    y_n, hf, hb = _ssd(seq(p), seq(pdt), lw, h0f, h0b)
    d = _pool_diff(seq(p), grid_rows)
    f = _fourier(seq(p))
    x = seq(_merge_out(tok(d), tok(f), tok(y_n), p, tok(x), g_m, wts, layer, MERGE_TM))
    return x, hf, hb


def kernel(x, c, ctx, c_ctx, w_ada, b_ada, norm_mix_w, norm_ffn_w, w_in, conv_w, conv_b, a_log, dt_bias,
           d_skip, ssd_norm_w, w_ssd_out, w_pool, pool_scale, w_fourier, w_out, w_ffn_gate, w_ffn_up,
           w_ffn_down, final_norm_w):
    b, seq, d = x.shape
    rows = seq // GRID_W
    mod = _adaln(jnp.concatenate([c, c_ctx[None, :]], axis=0), w_ada, b_ada)
    h0 = jnp.zeros((b,) + _STATE_TAIL, F32)
    lc = ctx.shape[1]
    w_main = _winprep(w_in, 0)
    w_dt = jnp.pad(w_in[:, :, XBC:SSD_IN], ((0, 0), (0, 0), (0, DT_PAD - 2 * SSD_HEADS))).astype(BF16)
    wts = {"w_main": w_main, "w_dt": w_dt, "pool_scale": pool_scale.reshape(DEPTH, 1, D_MODEL)}
    ff = w_ffn_gate.shape[2]
    wg = wu = wd = None
    for l in range(DEPTH):
        last = l == DEPTH - 1
        lw = _layer_weights(l, conv_w, conv_b, a_log, dt_bias, d_skip, ssd_norm_w)
        sh_m, sc_m, g_m, sh_f, sc_f, g_f = [mod[l, :b, i * d:(i + 1) * d].reshape(b, 1, d) for i in range(6)]
        csh_m, csc_m, cg_m, csh_f, csc_f, cg_f = [mod[l, b, i * d:(i + 1) * d].reshape(1, 1, d) for i in range(6)]

        proj = None
        if l == 0:
            r = FFN_CAST_ROWS
            side = [(w_ffn_gate.reshape(DEPTH * d, ff), r), (w_ffn_up.reshape(DEPTH * d, ff), r),
                    (w_ffn_down.reshape(DEPTH * ff, d), 2 * r),
                    (w_out.reshape(DEPTH * d, d), r), (w_ssd_out.reshape(DEPTH * D_SSD, d), r),
                    (w_fourier.reshape(DEPTH * FOUR_W, d), r // 2), (w_pool.reshape(DEPTH * POOL_W, POOL_OUT), r // 2)]
            p0, pdt0, wg, wu, wd, wo, ws, wf, wp = _inproj(x, norm_mix_w[l], sc_m, sh_m, w_main, w_dt, l, P_TOT,
                                                           INPROJ_TM, INPROJ_TN, side, w_layer=0)
            proj = (p0, pdt0)
            wg, wu, wd = wg.reshape(DEPTH, d, ff), wu.reshape(DEPTH, d, ff), wd.reshape(DEPTH, ff, d)
            wts.update(w_out=wo.reshape(DEPTH, d, d), w_ssd_out=ws.reshape(DEPTH, D_SSD, d),
                       w_fourier=wf.reshape(DEPTH, FOUR_W, d), w_pool=wp.reshape(DEPTH, POOL_W, POOL_OUT))

        if last:
            pc, pdtc = _inproj(ctx.reshape(1, b * lc, d), norm_mix_w[l], csc_m, csh_m, wts["w_main"], w_dt, l,
                               XBC, INPROJ_TM, INPROJ_TN, w_layer=0)
            pdtc = pdtc.reshape(b, lc, DT_PAD)
            _, xcc, hb = _ssd_bwd(pc.reshape(b, lc, XBC), pdtc, lw, h0)
            _, hf = _ssd_fwd(xcc, pdtc, lw, h0)
        else:
            ctx, hf, hb = _mixer(ctx, norm_mix_w[l], csc_m, csh_m, cg_m, wts, lw, l, None, h0, h0)
            ctx = _ffn(ctx.reshape(1, b * lc, d), norm_ffn_w[l], csc_f, csh_f, wg, wu, wd, l,
                       cg_f, final_norm_w, False, FFN_TM, FFN_TF)[0].reshape(b, lc, d)

        x, _, _ = _mixer(x, norm_mix_w[l], sc_m, sh_m, g_m, wts, lw, l, rows, hf, hb, proj)
        if last:
            x, = _ffn(x, norm_ffn_w[l], sc_f, sh_f, wg, wu, wd, l, g_f, final_norm_w, True, FFN_TM, FFN_TF)
        else:
            x, wts["w_main"] = _ffn(x, norm_ffn_w[l], sc_f, sh_f, wg, wu, wd, l, g_f, final_norm_w, False,
                                    FFN_TM, FFN_TF, side_w=(w_in, l + 1))
    return x
```

```python
import functools
import math

import numpy as np
import jax
import jax.numpy as jnp
from jax import lax
from jax.experimental import pallas as pl
from jax.experimental.pallas import tpu as pltpu

F32 = jnp.float32
BF16 = jnp.bfloat16

D_MODEL = 2048
DEPTH = 2
GRID_W = 64
EPS = 1e-6

POOL_GROUPS = 4
POOL_WINDOWS = (2, 4, 8, 16)
POOL_W = D_MODEL // 2
POOL_GC = POOL_W // POOL_GROUPS
POOL_OUT = D_MODEL // POOL_GROUPS

FOUR_GROUPS = 4
FOUR_W = D_MODEL // 2
FOUR_GC = FOUR_W // FOUR_GROUPS

D_SSD = D_MODEL
HEAD_DIM = 64
SSD_HEADS = D_SSD // HEAD_DIM
SSD_GROUPS = 4
HEADS_PER_GROUP = SSD_HEADS // SSD_GROUPS
GROUP_W = HEADS_PER_GROUP * HEAD_DIM
D_STATE = 128
CONV_W = 5
CHUNK = 128

XBC = D_SSD + 2 * SSD_GROUPS * D_STATE
SSD_IN = XBC + 2 * SSD_HEADS
Z_OFF = SSD_IN
POOL_OFF = Z_OFF + D_SSD
FOUR_OFF = POOL_OFF + POOL_W
GATE_OFF = FOUR_OFF + FOUR_W
N_IN = GATE_OFF + 3 * D_MODEL

P_XBC = 0
P_POOL = XBC
P_Z = P_POOL + POOL_W
P_GATE = P_Z + D_SSD
P_FOUR = P_GATE + 3 * D_MODEL
P_TOT = P_FOUR + FOUR_W
DT_PAD = 128

V7X_VMEM_LIMIT = 56 * 1024 * 1024
HALO = 16

INPROJ_TM, INPROJ_TN = 1024, 1024
MERGE_TM = 256
FFN_TM, FFN_TF = 512, 512
ADALN_TN = 1024
FFN_CAST_ROWS = 64


def _cparams(sem):
    return pltpu.CompilerParams(dimension_semantics=sem, vmem_limit_bytes=V7X_VMEM_LIMIT)


def _resident(shape, index_map):
    return pl.BlockSpec(shape, index_map, pipeline_mode=pl.Buffered(1))


def _split2(v):
    hi = v.astype(BF16)
    lo = (v - hi.astype(F32)).astype(BF16)
    return hi, lo


def _dot(a, b):
    return jnp.dot(a, b, preferred_element_type=F32)


def _sigmoid(v):
    return 0.5 * jnp.tanh(0.5 * v) + 0.5


def _silu(v):
    return v * _sigmoid(v)


def _np_split2(m):
    m = jnp.asarray(np.asarray(m, np.float32))
    hi = m.astype(BF16)
    lo = (m - hi.astype(F32)).astype(BF16)
    return hi, lo


WIN_CHUNK = 1024

_WIN_SEGMENTS = ((P_XBC, 0, XBC), (P_POOL, POOL_OFF, POOL_W), (P_Z, Z_OFF, D_SSD),
                 (P_GATE, GATE_OFF, 3 * D_MODEL), (P_FOUR, FOUR_OFF, FOUR_W))


def _win_src(j, chunk=None):
    chunk = chunk or WIN_CHUNK
    src = jnp.int32(0)
    for dst0, src0, width in _WIN_SEGMENTS:
        first = dst0 // chunk
        inside = (j >= first) & (j < first + width // chunk)
        src = jnp.where(inside, src0 + (j - first) * chunk, src)
    return pl.multiple_of(src, 2 * SSD_HEADS)


def _winprep_kernel(wt_ref, o_ref):
    o_ref[0] = wt_ref[0].T.astype(BF16)


def _winprep(w_in, layer=None):
    depth, d, n = w_in.shape
    wt = jnp.swapaxes(w_in, 1, 2)
    layers = depth if layer is None else 1
    first = 0 if layer is None else layer
    return pl.pallas_call(
        _winprep_kernel,
        grid=(layers, P_TOT // WIN_CHUNK),
        in_specs=[pl.BlockSpec((pl.Element(1), pl.Element(WIN_CHUNK), pl.Element(d)),
                               lambda l, j: (first + l, _win_src(j), 0))],
        out_specs=pl.BlockSpec((1, d, WIN_CHUNK), lambda l, j: (l, 0, j)),
        out_shape=jax.ShapeDtypeStruct((layers, d, P_TOT), BF16),
        compiler_params=_cparams(("parallel", "parallel")),
        name="winprep",
    )(wt)


ADA_ROWS = 8


def _adaln_kernel(ct_ref, w_ref, b_ref, o_ref, *, n_rows):
    s = _silu(ct_ref[...])
    w = w_ref[0]
    tn = w.shape[1]
    rows = [jnp.sum(w * s[:, r:r + 1], axis=0, keepdims=True) for r in range(n_rows)]
    rows.append(jnp.zeros((ADA_ROWS - n_rows, tn), F32))
    o_ref[0] = jnp.concatenate(rows, axis=0) + b_ref[0]


def _adaln(cc, w_ada, b_ada):
    depth, d, n = w_ada.shape
    tn = ADALN_TN
    n_rows = cc.shape[0]
    ct = jnp.pad(cc, ((0, ADA_ROWS - n_rows), (0, 0))).T
    return pl.pallas_call(
        functools.partial(_adaln_kernel, n_rows=n_rows),
        grid=(depth, n // tn),
        in_specs=[pl.BlockSpec((d, ADA_ROWS), lambda l, j: (0, 0)),
                  pl.BlockSpec((1, d, tn), lambda l, j: (l, 0, j)),
                  pl.BlockSpec((1, 1, tn), lambda l, j: (l, 0, j))],
        out_specs=pl.BlockSpec((1, ADA_ROWS, tn), lambda l, j: (l, 0, j)),
        out_shape=jax.ShapeDtypeStruct((depth, ADA_ROWS, n), F32),
        compiler_params=_cparams(("parallel", "parallel")),
        name="adaln",
    )(ct, w_ada, b_ada.reshape(depth, 1, n))


def _norm_mod(x, nw, sc, sh):
    ms = jnp.mean(x * x, axis=-1, keepdims=True)
    return (x * lax.rsqrt(ms + EPS)) * nw * (1.0 + sc) + sh


def _inproj_kernel(x_ref, nw_ref, sc_ref, sh_ref, w_ref, wdt_ref, *rest, n_side):
    side_in = rest[:n_side]
    o_ref, dt_ref = rest[n_side:n_side + 2]
    side_out = rest[n_side + 2:2 * n_side + 2]
    h_ref = rest[-1]

    @pl.when(pl.program_id(2) == 0)
    def _():
        h = _norm_mod(x_ref[0], nw_ref[...], sc_ref[0], sh_ref[0]).astype(BF16)
        h_ref[...] = h
        dt_ref[0] = _dot(h, wdt_ref[0])

    o_ref[0] = _dot(h_ref[...], w_ref[0]).astype(o_ref.dtype)

    for wi_ref, wo_ref in zip(side_in, side_out):
        wo_ref[...] = wi_ref[...].astype(BF16)


def _inproj(x, nw, sc, sh, w, wdt, layer, n_out, tm, tn, side=(), w_layer=None):
    b, l, d = x.shape
    tm = min(tm, l)
    n_i, n_j = l // tm, n_out // tn
    w_layer = layer if w_layer is None else w_layer
    side_specs, side_shapes = [], []
    for arr, rows in side:
        nblk = arr.shape[0] // rows
        assert arr.shape[0] % rows == 0 and nblk <= b * n_i * n_j
        imap = lambda bi, i, j, nblk=nblk: (jnp.minimum((bi * n_i + i) * n_j + j, nblk - 1), 0)
        side_specs.append(pl.BlockSpec((rows, arr.shape[1]), imap))
        side_shapes.append(jax.ShapeDtypeStruct(arr.shape, BF16))
    sem = ("arbitrary",) * 3 if side else ("parallel", "parallel", "arbitrary")
    return pl.pallas_call(
        functools.partial(_inproj_kernel, n_side=len(side)),
        grid=(b, n_i, n_j),
        in_specs=[pl.BlockSpec((1, tm, d), lambda bi, i, j: (bi, i, 0)),
                  pl.BlockSpec((1, d), lambda bi, i, j: (0, 0)),
                  pl.BlockSpec((1, 1, d), lambda bi, i, j: (bi, 0, 0)),
                  pl.BlockSpec((1, 1, d), lambda bi, i, j: (bi, 0, 0)),
                  pl.BlockSpec((1, d, tn), lambda bi, i, j: (w_layer, 0, j)),
                  _resident((1, d, DT_PAD), lambda bi, i, j: (layer, 0, 0))] + side_specs,
        out_specs=[pl.BlockSpec((1, tm, tn), lambda bi, i, j: (bi, i, j)),
                   pl.BlockSpec((1, tm, DT_PAD), lambda bi, i, j: (bi, i, 0))] + side_specs,
        out_shape=[jax.ShapeDtypeStruct((b, l, n_out), BF16),
                   jax.ShapeDtypeStruct((b, l, DT_PAD), F32)] + side_shapes,
        scratch_shapes=[pltpu.VMEM((tm, d), BF16)],
        compiler_params=_cparams(sem),
        name="inproj",
    )(x, nw.reshape(1, d), sc, sh, w, wdt, *[arr for arr, _ in side])


def _merge_kernel(d_ref, f_ref, yn_ref, gp_ref, gf_ref, gs_ref, x_ref, gm_ref,
                  wp_ref, ps_ref, wf_ref, ws_ref, wo_ref, o_ref):
    d = d_ref[0]
    y_pool = jnp.concatenate(
        [_dot(d[:, g * POOL_GC:(g + 1) * POOL_GC], wp_ref[0, g * POOL_GC:(g + 1) * POOL_GC, :])
         for g in range(POOL_GROUPS)], axis=1)
    m = _sigmoid(gp_ref[0].astype(F32)) * (y_pool * ps_ref[0])
    m = m + _sigmoid(gf_ref[0].astype(F32)) * _dot(f_ref[0], wf_ref[0])
    m = m + _sigmoid(gs_ref[0].astype(F32)) * _dot(yn_ref[0], ws_ref[0])
    o_ref[0] = x_ref[0] + gm_ref[0] * _dot(m.astype(BF16), wo_ref[0])


def _merge_out(d, f, yn, p, x, g_m, wts, layer, tm):
    b, l, dm = x.shape
    tm = min(tm, l)
    gblk = P_GATE // dm
    row = lambda bi, i: (bi, i, 0)
    lyr = lambda bi, i: (layer, 0, 0)
    return pl.pallas_call(
        _merge_kernel,
        grid=(b, l // tm),
        in_specs=[pl.BlockSpec((1, tm, POOL_W), row),
                  pl.BlockSpec((1, tm, FOUR_W), row),
                  pl.BlockSpec((1, tm, D_SSD), row),
                  pl.BlockSpec((1, tm, dm), lambda bi, i: (bi, i, gblk)),
                  pl.BlockSpec((1, tm, dm), lambda bi, i: (bi, i, gblk + 1)),
                  pl.BlockSpec((1, tm, dm), lambda bi, i: (bi, i, gblk + 2)),
                  pl.BlockSpec((1, tm, dm), row),
                  pl.BlockSpec((1, 1, dm), lambda bi, i: (bi, 0, 0)),
                  _resident((1, POOL_W, POOL_OUT), lyr),
                  _resident((1, 1, dm), lyr),
                  _resident((1, FOUR_W, dm), lyr),
                  _resident((1, D_SSD, dm), lyr),
                  _resident((1, dm, dm), lyr)],
        out_specs=pl.BlockSpec((1, tm, dm), row),
        out_shape=jax.ShapeDtypeStruct((b, l, dm), F32),
        compiler_params=_cparams(("parallel", "parallel")),
        name="merge_out",
    )(d, f, yn, p, p, p, x, g_m, wts["w_pool"], wts["pool_scale"], wts["w_fourier"], wts["w_ssd_out"],
      wts["w_out"])


WIN_SIDE_CHUNK = 512


def _ffn_kernel(x_ref, nw_ref, sc_ref, sh_ref, wg_ref, wu_ref, wd_ref, gate_ref, fnw_ref, *rest, final_norm, side):
    if side:
        wi_ref, o_ref, wo_ref, h_ref, acc_ref = rest
        wo_ref[0] = wi_ref[0].T.astype(BF16)
    else:
        o_ref, h_ref, acc_ref = rest
    j = pl.program_id(2)

    @pl.when(j == 0)
    def _():
        h_ref[...] = _norm_mod(x_ref[0], nw_ref[...], sc_ref[0], sh_ref[0]).astype(BF16)
        acc_ref[...] = jnp.zeros_like(acc_ref)

    h = h_ref[...]
    a = _silu(_dot(h, wg_ref[0])) * _dot(h, wu_ref[0])
    acc_ref[...] += _dot(a.astype(BF16), wd_ref[0])

    @pl.when(j == pl.num_programs(2) - 1)
    def _():
        o = x_ref[0] + gate_ref[0] * acc_ref[...]
        if final_norm:
            ms = jnp.mean(o * o, axis=-1, keepdims=True)
            o = (o * lax.rsqrt(ms + EPS)) * fnw_ref[...]
        o_ref[0] = o


def _ffn(x, nw, sc, sh, wg, wu, wd, layer, gate, fnw, final_norm, tm, tf, side_w=None):
    b, l, d = x.shape
    ff = wg.shape[2]
    tm = min(tm, l)
    n_i, n_j = l // tm, ff // tf
    vec = lambda bi, i, j: (bi, 0, 0)
    side_in, side_out, side_shape, side_args = [], [], [], []
    if side_w is not None:
        nchunk = P_TOT // WIN_SIDE_CHUNK
        assert nchunk <= b * n_i * n_j
        step = lambda bi, i, j: jnp.minimum((bi * n_i + i) * n_j + j, nchunk - 1)
        side_in = [pl.BlockSpec((pl.Element(1), pl.Element(WIN_SIDE_CHUNK), pl.Element(d)),
                                lambda bi, i, j: (side_w[1], _win_src(step(bi, i, j), WIN_SIDE_CHUNK), 0))]
        side_out = [pl.BlockSpec((1, d, WIN_SIDE_CHUNK), lambda bi, i, j: (0, 0, step(bi, i, j)))]
        side_shape = [jax.ShapeDtypeStruct((1, d, P_TOT), BF16)]
        side_args = [jnp.swapaxes(side_w[0], 1, 2)]
    sem = ("arbitrary",) * 3 if side_w is not None else ("parallel", "parallel", "arbitrary")
    return pl.pallas_call(
        functools.partial(_ffn_kernel, final_norm=final_norm, side=side_w is not None),
        grid=(b, n_i, n_j),
        in_specs=[pl.BlockSpec((1, tm, d), lambda bi, i, j: (bi, i, 0)),
                  pl.BlockSpec((1, d), lambda bi, i, j: (0, 0)),
                  pl.BlockSpec((1, 1, d), vec),
                  pl.BlockSpec((1, 1, d), vec),
                  pl.BlockSpec((1, d, tf), lambda bi, i, j: (layer, 0, j)),
                  pl.BlockSpec((1, d, tf), lambda bi, i, j: (layer, 0, j)),
                  pl.BlockSpec((1, tf, d), lambda bi, i, j: (layer, j, 0)),
                  pl.BlockSpec((1, 1, d), vec),
                  pl.BlockSpec((1, d), lambda bi, i, j: (0, 0))] + side_in,
        out_specs=[pl.BlockSpec((1, tm, d), lambda bi, i, j: (bi, i, 0))] + side_out,
        out_shape=[jax.ShapeDtypeStruct((b, l, d), F32)] + side_shape,
        scratch_shapes=[pltpu.VMEM((tm, d), BF16), pltpu.VMEM((tm, d), F32)],
        compiler_params=_cparams(sem),
        name="ffn",
    )(x, nw.reshape(1, d), sc, sh, wg, wu, wd, gate, fnw.reshape(1, d), *side_args)


def _box_matrix(n, w):
    idx = np.arange(n)
    lo = np.clip(idx - w // 2, 0, n)
    hi = np.clip(idx + (w - w // 2), 0, n)
    m = ((idx[None, :] >= lo[:, None]) & (idx[None, :] < hi[:, None])).astype(np.float64)
    return m / (hi - lo)[:, None]


POOL_TB = 256
POOL_PAD = 16


def _pool_kernel(u_ref, mh_ref, ml_ref, o_ref, *scratch, grid_rows):
    g = pl.program_id(1)
    l = u_ref.shape[1]
    mh = mh_ref[0]
    ml = ml_ref[0]
    if grid_rows is None:
        v = u_ref[0]
        o_ref[0] = (_dot(mh, v) + _dot(ml, v) - v.astype(F32)).astype(o_ref.dtype)
        return

    cp_ref, = scratch
    pad = POOL_PAD * GRID_W
    cp_ref[0:pad, :] = jnp.zeros((pad, POOL_GC), F32)
    cp_ref[pad + l:pad + l + pad, :] = jnp.zeros((pad, POOL_GC), F32)
    for i in range(l // POOL_TB):
        v = u_ref[0, i * POOL_TB:(i + 1) * POOL_TB, :]
        cp_ref[pad + i * POOL_TB:pad + (i + 1) * POOL_TB, :] = _dot(mh, v) + _dot(ml, v)

    for gi, w in enumerate(POOL_WINDOWS):
        lo_off, hi_off = -(w // 2), w - w // 2

        @pl.when(g == gi)
        def _(lo_off=lo_off, hi_off=hi_off):
            def slab(row):
                return cp_ref[pl.ds(pl.multiple_of(pad + row * GRID_W, GRID_W), GRID_W), :]

            def body(r, s):
                s = s + slab(r + hi_off - 1) - slab(r + lo_off - 1)
                cnt = jnp.minimum(r + hi_off, grid_rows) - jnp.maximum(r + lo_off, 0)
                cntv = jnp.full((GRID_W, POOL_GC), cnt, jnp.int32).astype(F32)
                t0 = pl.multiple_of(r * GRID_W, GRID_W)
                v = u_ref[0, pl.ds(t0, GRID_W), :].astype(F32)
                o_ref[0, pl.ds(t0, GRID_W), :] = (s / cntv - v).astype(o_ref.dtype)
                return s

            s0 = jnp.zeros((GRID_W, POOL_GC), F32)
            for o in range(lo_off - 1, hi_off - 1):
                s0 = s0 + cp_ref[pad + o * GRID_W:pad + (o + 1) * GRID_W, :]
            lax.fori_loop(0, grid_rows, body, s0)


def _pool_diff(p, grid_rows):
    b, l, _ = p.shape
    mats = []
    for w in POOL_WINDOWS:
        if grid_rows is None:
            mats.append(_box_matrix(l, w))
        else:
            mats.append(np.kron(np.eye(POOL_TB // GRID_W), _box_matrix(GRID_W, w)))
    mh, ml = _np_split2(np.stack(mats))
    tb = mats[0].shape[0]
    scratch = []
    if grid_rows is not None:
        scratch = [pltpu.VMEM((l + 2 * POOL_PAD * GRID_W, POOL_GC), F32)]
    cblk = P_POOL // POOL_GC
    return pl.pallas_call(
        functools.partial(_pool_kernel, grid_rows=grid_rows),
        grid=(b, POOL_GROUPS),
        in_specs=[pl.BlockSpec((1, l, POOL_GC), lambda bi, g: (bi, 0, cblk + g)),
                  pl.BlockSpec((1, tb, tb), lambda bi, g: (g, 0, 0)),
                  pl.BlockSpec((1, tb, tb), lambda bi, g: (g, 0, 0))],
        out_specs=pl.BlockSpec((1, l, POOL_GC), lambda bi, g: (bi, 0, g)),
        out_shape=jax.ShapeDtypeStruct((b, l, POOL_W), BF16),
        scratch_shapes=scratch,
        compiler_params=_cparams(("parallel", "parallel")),
        name="pool_diff",
    )(p, mh, ml)


FLIP_ROWS = 512


def _flip_matrix(t):
    m = np.zeros((t, t), np.float32)
    m[np.arange(1, t), t - np.arange(1, t)] = 1.0
    return jnp.asarray(m).astype(BF16)


def _chdft_fold_kernel(u_ref, ua_ref, ub_ref, w_ref, flip_ref, o_ref, nyq_ref):
    i = pl.program_id(2)
    c = FOUR_GC
    tm = u_ref.shape[1]
    w = w_ref[...]
    urev = _dot(flip_ref[...], ua_ref[0])
    row = lax.broadcasted_iota(jnp.int32, (tm, c), 0)
    urev = jnp.where(row == 0, jnp.broadcast_to(ub_ref[0, 0:1, :].astype(F32), (tm, c)), urev).astype(BF16)
    z = _dot(u_ref[0], w)
    zrev = _dot(urev, w)
    first = (row + jnp.minimum(i, 1)) == 0
    o_ref[0, 0, :, :c] = (z[:, :c] + jnp.where(first, 0.0, zrev[:, :c])).astype(o_ref.dtype)
    o_ref[0, 0, :, c:] = (z[:, c:] - zrev[:, c:]).astype(o_ref.dtype)

    @pl.when(i == 0)
    def _():
        nyq_ref[0, 0] = zrev[0:8, :c]


def _chdft_fold(p):
    b, l, _ = p.shape
    half = l // 2
    tm = min(FLIP_ROWS, half)
    nb = l // tm
    k = np.arange(FOUR_GC)
    ang = -2.0 * np.pi * ((k[:, None] * k[None, :]) % FOUR_GC) / FOUR_GC
    w = jnp.asarray(np.concatenate([np.cos(ang), np.sin(ang)], axis=1).astype(np.float32)).astype(BF16)
    cblk = P_FOUR // FOUR_GC
    return pl.pallas_call(
        _chdft_fold_kernel,
        grid=(b, FOUR_GROUPS, half // tm),
        in_specs=[pl.BlockSpec((1, tm, FOUR_GC), lambda bi, g, i: (bi, i, cblk + g)),
                  pl.BlockSpec((1, tm, FOUR_GC), lambda bi, g, i: (bi, nb - 1 - i, cblk + g)),
                  pl.BlockSpec((1, tm, FOUR_GC), lambda bi, g, i: (bi, jnp.where(i == 0, nb // 2, nb - i), cblk + g)),
                  _resident((FOUR_GC, 2 * FOUR_GC), lambda bi, g, i: (0, 0)),
                  _resident((tm, tm), lambda bi, g, i: (0, 0))],
        out_specs=[pl.BlockSpec((1, 1, tm, 2 * FOUR_GC), lambda bi, g, i: (bi, g, i, 0)),
                   pl.BlockSpec((1, 1, 8, FOUR_GC), lambda bi, g, i: (bi, g, 0, 0))],
        out_shape=[jax.ShapeDtypeStruct((b, FOUR_GROUPS, half, 2 * FOUR_GC), BF16),
                   jax.ShapeDtypeStruct((b, FOUR_GROUPS, 8, FOUR_GC), F32)],
        compiler_params=_cparams(("parallel", "parallel", "arbitrary")),
        name="chdft",
    )(p, p, p, w, _flip_matrix(tm))


def _seq_dft_tables(l):
    half = l // 2
    r = int(round(math.sqrt(l)))
    assert r * r == l and r % 2 == 0
    col = np.arange(half)
    hi = 2.0 * np.pi * ((np.arange(r // 2)[:, None] * r * col[None, :]) % l) / l
    lo = 2.0 * np.pi * ((np.arange(r)[:, None] * col[None, :]) % l) / l
    ch, sh = [jnp.asarray(f(hi).astype(np.float32))[:, None, :] for f in (np.cos, np.sin)]
    cl, sl = [jnp.asarray(f(lo).astype(np.float32))[None, :, :] for f in (np.cos, np.sin)]
    cos = (ch * cl - sh * sl).reshape(half, half)
    sin = (sh * cl + ch * sl).reshape(half, half)
    return cos.astype(BF16), sin.astype(BF16)


def _seqdft_kernel(c_ref, s_ref, alt_ref, flip_ref, e_ref, nyq_ref, o_ref, *, scale):
    c = FOUR_GC
    half = e_ref.shape[2]
    ft = flip_ref.shape[0]
    nblk = half // ft
    e = e_ref[0, 0]
    p = _dot(c_ref[...], e[:, :c])
    q = _dot(s_ref[...], e[:, c:])
    zn = nyq_ref[0, 0, 0:1, :]
    k = lax.broadcasted_iota(jnp.int32, (half, c), 0)
    n = jnp.where((k & 1) == 0, zn, -zn)
    o_ref[0, 0:half, :] = ((p + q + n) * scale).astype(o_ref.dtype)
    mir = (p - q + n) * scale
    mir_b = mir.astype(BF16)
    y_half = (_dot(alt_ref[...], e[:, :c])[0:1] + zn) * scale
    row = lax.broadcasted_iota(jnp.int32, (ft, c), 0)
    for jb in range(nblk):
        src = (nblk - 1 - jb) * ft
        hi = _dot(flip_ref[...], mir_b[src:src + ft])
        first = y_half if jb == 0 else mir[src + ft:src + ft + 1]
        hi = jnp.where(row == 0, jnp.broadcast_to(first, (ft, c)), hi)
        o_ref[0, half + jb * ft:half + (jb + 1) * ft, :] = hi.astype(o_ref.dtype)


def _fourier(p):
    b, l, _ = p.shape
    half = l // 2
    ft = min(FLIP_ROWS, half)
    e, nyq = _chdft_fold(p)
    cos, sin = _seq_dft_tables(l)
    alt = jnp.asarray(np.broadcast_to(1.0 - 2.0 * (np.arange(half) % 2), (16, half)).astype(np.float32)).astype(BF16)
    scale = 1.0 / math.sqrt(l * FOUR_GC)
    const2 = lambda bi, g: (0, 0)
    return pl.pallas_call(
        functools.partial(_seqdft_kernel, scale=scale),
        grid=(b, FOUR_GROUPS),
        in_specs=[_resident((half, half), const2),
                  _resident((half, half), const2),
                  _resident((16, half), const2),
                  _resident((ft, ft), const2),
                  pl.BlockSpec((1, 1, half, 2 * FOUR_GC), lambda bi, g: (bi, g, 0, 0)),
                  pl.BlockSpec((1, 1, 8, FOUR_GC), lambda bi, g: (bi, g, 0, 0))],
        out_specs=pl.BlockSpec((1, l, FOUR_GC), lambda bi, g: (bi, 0, g)),
        out_shape=jax.ShapeDtypeStruct((b, l, FOUR_W), BF16),
        compiler_params=_cparams(("parallel", "parallel")),
        name="seqdft",
    )(cos, sin, alt, _flip_matrix(ft), e, nyq)


CONV_CB = 512
_SHIFT_TAPS = tuple(k for k in range(CONV_W) if k != CONV_W // 2)


def _shift_matrix():
    m = np.zeros((len(_SHIFT_TAPS) * CHUNK, CHUNK + 2 * HALO), np.float32)
    t = np.arange(CHUNK)
    for q, k in enumerate(_SHIFT_TAPS):
        m[q * CHUNK + t, HALO - CONV_W // 2 + k + t] = 1.0
    return jnp.asarray(m).astype(BF16)


def _ssd_kernel(*refs, fwd, nc, epilogue):
    if fwd:
        xc_ref, dt_ref, dtb_ref, alog_ref, e_ref, h0_ref = refs[:6]
        rest = refs[6:]
        if epilogue:
            z_ref, yb_ref, dsk_ref, nw_ref, y_ref, hout_ref, st_ref = rest
        else:
            y_ref, hout_ref, st_ref = rest
    else:
        (xm_ref, xp_ref, xn_ref, dt_ref, cw_ref, cb_ref, shift_ref, dtb_ref, alog_ref, e_ref, h0_ref,
         y_ref, xc_ref, hout_ref, st_ref, ext_ref) = refs
    c = pl.program_id(0)
    cc = c if fwd else nc - 1 - c
    T = CHUNK
    nb = dt_ref.shape[0]

    @pl.when(c == 0)
    def _():
        st_ref[...] = h0_ref[...]

    if not fwd:
        @pl.when(cc > 0)
        def _():
            ext_ref[:, 0:HALO, :] = xp_ref[...]

        @pl.when(cc == 0)
        def _():
            ext_ref[:, 0:HALO, :] = jnp.zeros((nb, HALO, XBC), ext_ref.dtype)

        ext_ref[:, HALO:HALO + T, :] = xm_ref[...]

        @pl.when(cc < nc - 1)
        def _():
            ext_ref[:, HALO + T:HALO + T + HALO, :] = xn_ref[...]

        @pl.when(cc == nc - 1)
        def _():
            ext_ref[:, HALO + T:HALO + T + HALO, :] = jnp.zeros((nb, HALO, XBC), ext_ref.dtype)

        shift = shift_ref[...]
        for j in range(XBC // CONV_CB):
            cs = slice(j * CONV_CB, (j + 1) * CONV_CB)
            for bi in range(nb):
                sh = _dot(shift, ext_ref[bi, :, cs])
                acc = cb_ref[:, cs] + xm_ref[bi, :, cs].astype(F32) * cw_ref[CONV_W // 2:CONV_W // 2 + 1, cs]
                for q, k in enumerate(_SHIFT_TAPS):
                    acc = acc + sh[q * T:(q + 1) * T] * cw_ref[k:k + 1, cs]
                xc_ref[bi, :, cs] = _silu(acc).astype(xc_ref.dtype)

    row = lax.broadcasted_iota(jnp.int32, (T, T), 0)
    col = lax.broadcasted_iota(jnp.int32, (T, T), 1)
    causal = (col <= row) if fwd else (col >= row)
    tri = jnp.where(causal, 1.0, 0.0).astype(BF16)
    lane = lax.broadcasted_iota(jnp.int32, (T, 2 * HEAD_DIM), 1)
    hbase = 0 if fwd else SSD_HEADS
    e = e_ref[...]
    a = -jnp.exp(alog_ref[...])

    def head_factors(bi):
        xdt = dt_ref[bi] + dtb_ref[...]
        dt = jnp.maximum(xdt, 0.0) + jnp.log1p(jnp.exp(-jnp.abs(xdt)))
        da = dt * a
        d1 = da.astype(BF16)
        r1 = da - d1.astype(F32)
        d2 = r1.astype(BF16)
        d3 = (r1 - d2.astype(F32)).astype(BF16)
        acs = _dot(tri, d1) + _dot(tri, d2) + _dot(tri, d3)
        tot = acs[T - 1:T, :] if fwd else acs[0:1, :]
        eh, el = _split2(jnp.broadcast_to(jnp.exp(tot), (16, T)))
        ex = _dot(jnp.concatenate([jnp.exp(acs).astype(BF16), (dt * jnp.exp(tot - acs)).astype(BF16), eh, el],
                                  axis=0), e)
        return dict(acs=acs, arow_t=(acs - jnp.log(dt)).T, eacs_x=ex[0:T], w2=ex[T:2 * T],
                    etot_x=ex[2 * T:2 * T + 1] + ex[2 * T + 16:2 * T + 17])

    hf = [head_factors(bi) for bi in range(nb)]

    for g in range(SSD_GROUPS):
        gs = slice(g * GROUP_W, (g + 1) * GROUP_W)
        bs = slice(D_SSD + g * D_STATE, D_SSD + (g + 1) * D_STATE)
        cs_ = slice(D_SSD + (SSD_GROUPS + g) * D_STATE, D_SSD + (SSD_GROUPS + g + 1) * D_STATE)
        for bi in range(nb):
            f = hf[bi]
            bb = xc_ref[bi, :, bs]
            cbf = xc_ref[bi, :, cs_]
            cb = lax.dot_general(cbf, bb, (((1,), (1,)), ((), ())), preferred_element_type=F32)
            st = st_ref[bi, g]
            yoff = _dot(cbf, st.astype(BF16)) * f["eacs_x"][:, gs]
            xb = xc_ref[bi, :, gs]
            ys = []
            for jp in range(HEADS_PER_GROUP // 2):
                ms = []
                for jj in range(2):
                    hc = hbase + g * HEADS_PER_GROUP + 2 * jp + jj
                    diff = f["acs"][:, hc:hc + 1] - f["arow_t"][hc:hc + 1, :]
                    ms.append((cb * jnp.exp(jnp.where(causal, diff, -1e30))).astype(BF16))
                r = _dot(jnp.concatenate(ms, axis=0), xb[:, jp * 2 * HEAD_DIM:(jp + 1) * 2 * HEAD_DIM])
                ys.append(jnp.where(lane < HEAD_DIM, r[:T], r[T:]))
            yg = jnp.concatenate(ys, axis=1) + yoff

            xg = xb.astype(F32)
            xds = (xg * f["w2"][:, gs]).astype(BF16)
            st_ref[bi, g] = st * f["etot_x"][:, gs] + _dot(bb.astype(F32).T.astype(BF16), xds)

            if epilogue:
                yt = yg + yb_ref[bi, :, gs] + dsk_ref[:, gs] * xg
                v = yt * _silu(z_ref[bi, :, gs].astype(F32))
                ms_ = jnp.mean(v * v, axis=-1, keepdims=True)
                y_ref[bi, :, gs] = ((v * lax.rsqrt(ms_ + EPS)) * nw_ref[:, gs]).astype(y_ref.dtype)
            else:
                y_ref[bi, :, gs] = yg

    @pl.when(c == nc - 1)
    def _():
        hout_ref[...] = st_ref[...]


_STATE_TAIL = (SSD_GROUPS, D_STATE, GROUP_W)


def _ssd_bwd(p, pdt, lw, h0):
    b, l, _ = p.shape
    nc = l // CHUNK
    nh = l // HALO
    per = CHUNK // HALO
    cidx = lambda c: nc - 1 - c
    chunk = lambda c: (0, cidx(c), 0)
    const2 = lambda c: (0, 0)
    state = lambda c: (0, 0, 0, 0)
    state_block = (b,) + _STATE_TAIL
    return pl.pallas_call(
        functools.partial(_ssd_kernel, fwd=False, nc=nc, epilogue=False),
        grid=(nc,),
        in_specs=[pl.BlockSpec((b, CHUNK, XBC), chunk),
                  pl.BlockSpec((b, HALO, XBC), lambda c: (0, jnp.maximum(cidx(c) * per - 1, 0), 0)),
                  pl.BlockSpec((b, HALO, XBC), lambda c: (0, jnp.minimum((cidx(c) + 1) * per, nh - 1), 0)),
                  pl.BlockSpec((b, CHUNK, DT_PAD), chunk),
                  _resident((8, XBC), const2),
                  _resident((1, XBC), const2),
                  _resident((len(_SHIFT_TAPS) * CHUNK, CHUNK + 2 * HALO), const2),
                  _resident((1, DT_PAD), const2),
                  _resident((1, DT_PAD), const2),
                  _resident((DT_PAD, D_SSD), const2),
                  _resident(state_block, state)],
        out_specs=[pl.BlockSpec((b, CHUNK, D_SSD), chunk),
                   pl.BlockSpec((b, CHUNK, XBC), chunk),
                   pl.BlockSpec(state_block, state)],
        out_shape=[jax.ShapeDtypeStruct((b, l, D_SSD), F32),
                   jax.ShapeDtypeStruct((b, l, XBC), BF16),
                   jax.ShapeDtypeStruct(state_block, F32)],
        scratch_shapes=[pltpu.VMEM(state_block, F32),
                        pltpu.VMEM((b, CHUNK + 2 * HALO, XBC), BF16)],
        compiler_params=_cparams(("arbitrary",)),
        name="ssd_bwd",
    )(p, p, p, pdt, lw["conv_w"], lw["conv_b"], _shift_matrix(), lw["dt_bias"], lw["a_log"], lw["expand_b"], h0)


def _ssd_fwd(xc, pdt, lw, h0, p=None, yb=None):
    b, l, _ = xc.shape
    nc = l // CHUNK
    chunk = lambda c: (0, c, 0)
    const2 = lambda c: (0, 0)
    state = lambda c: (0, 0, 0, 0)
    state_block = (b,) + _STATE_TAIL
    epilogue = p is not None
    in_specs = [pl.BlockSpec((b, CHUNK, XBC), chunk),
                pl.BlockSpec((b, CHUNK, DT_PAD), chunk),
                _resident((1, DT_PAD), const2),
                _resident((1, DT_PAD), const2),
                _resident((DT_PAD, D_SSD), const2),
                _resident(state_block, state)]
    args = [xc, pdt, lw["dt_bias"], lw["a_log"], lw["expand_f"], h0]
    if epilogue:
        z_blk = P_Z // D_SSD
        in_specs += [pl.BlockSpec((b, CHUNK, D_SSD), lambda c: (0, c, z_blk)),
                     pl.BlockSpec((b, CHUNK, D_SSD), chunk),
                     _resident((1, D_SSD), const2),
                     _resident((1, D_SSD), const2)]
        args += [p, yb, lw["d_skip"], lw["ssd_norm_w"]]
    return pl.pallas_call(
        functools.partial(_ssd_kernel, fwd=True, nc=nc, epilogue=epilogue),
        grid=(nc,),
        in_specs=in_specs,
        out_specs=[pl.BlockSpec((b, CHUNK, D_SSD), chunk),
                   pl.BlockSpec(state_block, state)],
        out_shape=[jax.ShapeDtypeStruct((b, l, D_SSD), BF16 if epilogue else F32),
                   jax.ShapeDtypeStruct(state_block, F32)],
        scratch_shapes=[pltpu.VMEM(state_block, F32)],
        compiler_params=_cparams(("arbitrary",)),
        name="ssd_fwd",
    )(*args)


def _ssd(p, pdt, lw, h0f, h0b):
    yb, xc, hb = _ssd_bwd(p, pdt, lw, h0b)
    y, hf = _ssd_fwd(xc, pdt, lw, h0f, p, yb)
    return y, hf, hb


def _expand_matrix(offset):
    e = np.zeros((DT_PAD, D_SSD), np.float32)
    for h in range(SSD_HEADS):
        e[offset + h, h * HEAD_DIM:(h + 1) * HEAD_DIM] = 1.0
    return jnp.asarray(e).astype(BF16)


def _layer_weights(l, conv_w, conv_b, a_log, dt_bias, d_skip, ssd_norm_w):
    pad_h = (0, DT_PAD - 2 * SSD_HEADS)
    return {
        "conv_w": jnp.pad(conv_w[l], ((0, 8 - CONV_W), (0, 0))),
        "conv_b": conv_b[l].reshape(1, XBC),
        "dt_bias": jnp.pad(dt_bias[l].reshape(-1), pad_h).reshape(1, DT_PAD),
        "a_log": jnp.pad(a_log[l].reshape(-1), pad_h).reshape(1, DT_PAD),
        "d_skip": jnp.repeat(d_skip[l], HEAD_DIM).reshape(1, D_SSD),
        "ssd_norm_w": ssd_norm_w[l].reshape(1, D_SSD),
        "expand_f": _expand_matrix(0),
        "expand_b": _expand_matrix(SSD_HEADS),
    }


def _mixer(x, nw, sc, sh, g_m, wts, lw, layer, grid_rows, h0f, h0b, proj=None):
    b, l, dm = x.shape
    tok = (lambda a: a.reshape(1, b * l, a.shape[-1])) if sc.shape[0] == 1 else (lambda a: a)
    seq = lambda a: a.reshape(b, l, a.shape[-1])
    p, pdt = proj or _inproj(tok(x), nw, sc, sh, wts["w_main"], wts["w_dt"], layer, P_TOT, INPROJ_TM, INPROJ_TN,
                             w_layer=0)
    y_n, hf, hb = _ssd(seq(p), seq(pdt), lw, h0f, h0b)
    d = _pool_diff(seq(p), grid_rows)
    f = _fourier(seq(p))
    x = seq(_merge_out(tok(d), tok(f), tok(y_n), p, tok(x), g_m, wts, layer, MERGE_TM))
    return x, hf, hb


def kernel(x, c, ctx, c_ctx, w_ada, b_ada, norm_mix_w, norm_ffn_w, w_in, conv_w, conv_b, a_log, dt_bias,
           d_skip, ssd_norm_w, w_ssd_out, w_pool, pool_scale, w_fourier, w_out, w_ffn_gate, w_ffn_up,
           w_ffn_down, final_norm_w):
    b, seq, d = x.shape
    rows = seq // GRID_W
    mod = _adaln(jnp.concatenate([c, c_ctx[None, :]], axis=0), w_ada, b_ada)
    h0 = jnp.zeros((b,) + _STATE_TAIL, F32)
    lc = ctx.shape[1]
    w_main = _winprep(w_in, 0)
    w_dt = jnp.pad(w_in[:, :, XBC:SSD_IN], ((0, 0), (0, 0), (0, DT_PAD - 2 * SSD_HEADS))).astype(BF16)
    wts = {"w_main": w_main, "w_dt": w_dt, "pool_scale": pool_scale.reshape(DEPTH, 1, D_MODEL)}
    ff = w_ffn_gate.shape[2]
    wg = wu = wd = None
    for l in range(DEPTH):
        last = l == DEPTH - 1
        lw = _layer_weights(l, conv_w, conv_b, a_log, dt_bias, d_skip, ssd_norm_w)
        sh_m, sc_m, g_m, sh_f, sc_f, g_f = [mod[l, :b, i * d:(i + 1) * d].reshape(b, 1, d) for i in range(6)]
        csh_m, csc_m, cg_m, csh_f, csc_f, cg_f = [mod[l, b, i * d:(i + 1) * d].reshape(1, 1, d) for i in range(6)]

        proj = None
        if l == 0:
            r = FFN_CAST_ROWS
            side = [(w_ffn_gate.reshape(DEPTH * d, ff), r), (w_ffn_up.reshape(DEPTH * d, ff), r),
                    (w_ffn_down.reshape(DEPTH * ff, d), 2 * r),
                    (w_out.reshape(DEPTH * d, d), r), (w_ssd_out.reshape(DEPTH * D_SSD, d), r),
                    (w_fourier.reshape(DEPTH * FOUR_W, d), r // 2), (w_pool.reshape(DEPTH * POOL_W, POOL_OUT), r // 2)]
            p0, pdt0, wg, wu, wd, wo, ws, wf, wp = _inproj(x, norm_mix_w[l], sc_m, sh_m, w_main, w_dt, l, P_TOT,
                                                           INPROJ_TM, INPROJ_TN, side, w_layer=0)
            proj = (p0, pdt0)
            wg, wu, wd = wg.reshape(DEPTH, d, ff), wu.reshape(DEPTH, d, ff), wd.reshape(DEPTH, ff, d)
            wts.update(w_out=wo.reshape(DEPTH, d, d), w_ssd_out=ws.reshape(DEPTH, D_SSD, d),
                       w_fourier=wf.reshape(DEPTH, FOUR_W, d), w_pool=wp.reshape(DEPTH, POOL_W, POOL_OUT))

        if last:
            pc, pdtc = _inproj(ctx.reshape(1, b * lc, d), norm_mix_w[l], csc_m, csh_m, wts["w_main"], w_dt, l,
                               XBC, INPROJ_TM, INPROJ_TN, w_layer=0)
            pdtc = pdtc.reshape(b, lc, DT_PAD)
            _, xcc, hb = _ssd_bwd(pc.reshape(b, lc, XBC), pdtc, lw, h0)
            _, hf = _ssd_fwd(xcc, pdtc, lw, h0)
        else:
            ctx, hf, hb = _mixer(ctx, norm_mix_w[l], csc_m, csh_m, cg_m, wts, lw, l, None, h0, h0)
            ctx = _ffn(ctx.reshape(1, b * lc, d), norm_ffn_w[l], csc_f, csh_f, wg, wu, wd, l,
                       cg_f, final_norm_w, False, FFN_TM, FFN_TF)[0].reshape(b, lc, d)

        x, _, _ = _mixer(x, norm_mix_w[l], sc_m, sh_m, g_m, wts, lw, l, rows, hf, hb, proj)
        if last:
            x, = _ffn(x, norm_ffn_w[l], sc_f, sh_f, wg, wu, wd, l, g_f, final_norm_w, True, FFN_TM, FFN_TF)
        else:
            x, wts["w_main"] = _ffn(x, norm_ffn_w[l], sc_f, sh_f, wg, wu, wd, l, g_f, final_norm_w, False,
                                    FFN_TM, FFN_TF, side_w=(w_in, l + 1))
    return x
```
